```python
import jax
import jax.numpy as jnp
from jax import lax
import numpy as np

D_MODEL = 2048
BATCH = 32
SEQ = 256
DEPTH = 2
DEC_BATCH = 2
DEC_SEQ = 4096
PAST_LEN = 256

GRID_W = 64
WA = D_MODEL // 2
WB = D_MODEL // 4
WC = D_MODEL - WA - WB
HEAD_DIM_A = 64
N_HEADS_A = WA // HEAD_DIM_A
HEAD_DIM_B = 64
N_HEADS_B = WB // HEAD_DIM_B
POOL_WINDOWS = (2, 4, 8, 16)
N_POOL_GROUPS = len(POOL_WINDOWS)
POOL_GROUP_DIM = WC // N_POOL_GROUPS
NA_ROWS = 8
NA_COLS = 16
Q_BLOCK = 128
DECAY_LORA = max(32, int(round(1.8 * D_MODEL ** 0.5 / 32)) * 32)
AAA_LORA = max(32, int(round(1.8 * D_MODEL ** 0.5 / 32)) * 32)
GATE_LORA = max(32, int(round(0.6 * D_MODEL ** 0.8 / 32)) * 32)
CONV_W = 3
OFF_Q = 0
OFF_K = WA
OFF_V = 2 * WA
OFF_RKV = 3 * WA
OFF_WL = OFF_RKV + 3 * WB
OFF_AL = OFF_WL + 2 * DECAY_LORA
OFF_GL = OFF_AL + 2 * AAA_LORA
OFF_POOL = OFF_GL + GATE_LORA
P_IN = OFF_POOL + WC
D_FF = ((8 * D_MODEL // 3 + 255) // 256) * 256
N_EXPERTS = 8
TOP_K = 2
D_FF_EXPERT = 7 * D_MODEL // 2
N_DENSE = (DEPTH + 1) // 2
N_MOE = DEPTH // 2
RMS_EPS = 1e-6
GN_EPS = 64e-5
NEG_INF = -1e30

kernel_name = "hybrid_dit_prefix_natten_rwkv7_pool_step"


def rms_norm(x, g):
    xf = x.astype(jnp.float32)
    y = xf * lax.rsqrt(jnp.mean(xf * xf, axis=-1, keepdims=True) + RMS_EPS)
    return (y * g.astype(jnp.float32)).astype(x.dtype)


def ada_mod(cond, w, b):
    m = (jax.nn.silu(cond) @ w + b)[..., None, :]
    return jnp.split(m, 6, axis=-1)


def modulate(x, g, shift, scale):
    return rms_norm(x, g) * (1.0 + scale) + shift


def short_conv(x, w):
    T = x.shape[1]
    pad = CONV_W // 2
    xp = jnp.pad(x, ((0, 0), (pad, CONV_W - 1 - pad), (0, 0)))
    y = xp[:, 0:T] * w[0]
    for i in range(1, CONV_W):
        y = y + xp[:, i:i + T] * w[i]
    return y


def attn_context(q, k, v):
    B, S, _ = q.shape
    qh = q.reshape(B, S // Q_BLOCK, Q_BLOCK, N_HEADS_A, HEAD_DIM_A)
    kh = k.reshape(B, S, N_HEADS_A, HEAD_DIM_A)
    vh = v.reshape(B, S, N_HEADS_A, HEAD_DIM_A)

    def blk(qb):
        s = jnp.einsum('bqhd,blhd->bhql', qb, kh, preferred_element_type=jnp.float32) * HEAD_DIM_A ** -0.5
        p = jax.nn.softmax(s, axis=-1).astype(vh.dtype)
        return jnp.einsum('bhql,blhd->bqhd', p, vh)

    o = lax.map(blk, jnp.moveaxis(qh, 1, 0))
    o = jnp.moveaxis(o, 0, 1).reshape(B, S, WA)
    return o, kh, vh


def neighbourhood_attention(q, k, v, ck, cv, rpb):
    B, T, _ = q.shape
    rows = T // GRID_W
    kr = min(NA_ROWS, rows)
    qg = q.reshape(B, rows, GRID_W, N_HEADS_A, HEAD_DIM_A)
    kg = k.reshape(B, rows, GRID_W, N_HEADS_A, HEAD_DIM_A)
    vg = v.reshape(B, rows, GRID_W, N_HEADS_A, HEAD_DIM_A)
    r = jnp.arange(rows)
    rs = jnp.clip(r - kr // 2, 0, rows - kr)
    ridx = rs[:, None] + jnp.arange(kr)[None, :]
    kb = kg[:, ridx]
    vb = vg[:, ridx]
    scale = HEAD_DIM_A ** -0.5
    s_loc = jnp.einsum('brchd,brkmhd->bhrckm', qg, kb, preferred_element_type=jnp.float32) * scale
    col = jnp.arange(GRID_W)
    cs = jnp.clip(col - NA_COLS // 2, 0, GRID_W - NA_COLS)
    valid = (col[None, :] >= cs[:, None]) & (col[None, :] < cs[:, None] + NA_COLS)
    dci = jnp.clip(col[None, :] - col[:, None] + NA_COLS - 1, 0, 2 * NA_COLS - 2)
    dri = ridx - r[:, None] + NA_ROWS - 1
    bias = rpb.astype(jnp.float32)[:, dri[:, None, :, None], dci[None, :, None, :]]
    s_loc = jnp.where(valid[:, None, :], s_loc + bias, NEG_INF)
    n_loc = kr * GRID_W
    s_loc = s_loc.reshape(B, N_HEADS_A, rows, GRID_W, n_loc)
    s_ctx = jnp.einsum('brchd,blhd->bhrcl', qg, ck, preferred_element_type=jnp.float32) * scale
    p = jax.nn.softmax(jnp.concatenate([s_loc, s_ctx], axis=-1), axis=-1)
    p_loc = p[..., :n_loc].reshape(B, N_HEADS_A, rows, GRID_W, kr, GRID_W).astype(v.dtype)
    p_ctx = p[..., n_loc:].astype(v.dtype)
    o = jnp.einsum('bhrckm,brkmhd->brchd', p_loc, vb) + jnp.einsum('bhrcl,blhd->brchd', p_ctx, cv.astype(v.dtype))
    return o.reshape(B, T, WA)


def rwkv_scan(s0, r, decay, kk, bvec, kd, v, reverse):
    B, T, _ = r.shape

    def heads(z):
        return jnp.moveaxis(z.astype(jnp.float32).reshape(B, T, N_HEADS_B, HEAD_DIM_B), 1, 0)

    def step(S, inp):
        r_t, w_t, kk_t, b_t, k_t, v_t = inp
        sa = jnp.einsum('bhij,bhj->bhi', S, -kk_t)
        S = S * w_t[:, :, None, :] + sa[..., None] * b_t[:, :, None, :] + v_t[..., None] * k_t[:, :, None, :]
        return S, jnp.einsum('bhij,bhj->bhi', S, r_t)

    xs = (heads(r), heads(decay), heads(kk), heads(bvec), heads(kd), heads(v))
    s_fin, o = lax.scan(step, s0.astype(jnp.float32), xs, reverse=reverse)
    return jnp.moveaxis(o, 0, 1), s_fin


def rwkv_mix(p, s_fwd0, s_bwd0, conv_w, w0, w2, a0, a2, g2, k_k, k_a, r_k, ln_g, ln_b):
    B, T, _ = p.shape
    f32 = jnp.float32
    rkv = short_conv(p[..., OFF_RKV:OFF_WL], conv_w).astype(f32)
    r, k, v = jnp.split(rkv, 3, axis=-1)
    xw = p[..., OFF_WL:OFF_AL].astype(f32).reshape(B, T, 2, DECAY_LORA)
    xa = p[..., OFF_AL:OFF_GL].astype(f32).reshape(B, T, 2, AAA_LORA)
    xg = p[..., OFF_GL:OFF_POOL].astype(f32)
    wl = w0.astype(f32) + jnp.einsum('btdr,drc->btdc', jnp.tanh(xw), w2.astype(f32))
    decay = jnp.exp(-jnp.exp(-jax.nn.softplus(-wl) - 0.5))
    a = jax.nn.sigmoid(a0.astype(f32) + jnp.einsum('btdr,drc->btdc', xa, a2.astype(f32)))
    g = jax.nn.sigmoid(xg) @ g2.astype(f32)
    kk = (k * k_k.astype(f32)).reshape(B, T, N_HEADS_B, HEAD_DIM_B)
    kk = (kk * lax.rsqrt(jnp.sum(kk * kk, axis=-1, keepdims=True) + 1e-12)).reshape(B, T, WB)
    kd = k[:, :, None, :] * (1.0 + (a - 1.0) * k_a.astype(f32))
    bvec = kk[:, :, None, :] * a
    o_f, s_f = rwkv_scan(s_fwd0, r, decay[:, :, 0], kk, bvec[:, :, 0], kd[:, :, 0], v, False)
    o_b, s_b = rwkv_scan(s_bwd0, r, decay[:, :, 1], kk, bvec[:, :, 1], kd[:, :, 1], v, True)
    o = o_f + o_b
    mu = jnp.mean(o, axis=-1, keepdims=True)
    var = jnp.mean(jnp.square(o - mu), axis=-1, keepdims=True)
    o = ((o - mu) * lax.rsqrt(var + GN_EPS)).reshape(B, T, WB) * ln_g.astype(f32) + ln_b.astype(f32)
    bonus = jnp.sum((r * k * r_k.astype(f32)).reshape(B, T, N_HEADS_B, HEAD_DIM_B), axis=-1, keepdims=True) \
        * v.reshape(B, T, N_HEADS_B, HEAD_DIM_B)
    o = (o + bonus.reshape(B, T, WB)) * g
    return o.astype(p.dtype), s_f, s_b


def multiscale_pool(u, pool_w, pool_scale):
    B, T, _ = u.shape
    uf = u.astype(jnp.float32).reshape(B, T, N_POOL_GROUPS, POOL_GROUP_DIM)
    csum = jnp.concatenate([jnp.zeros((B, 1, N_POOL_GROUPS, POOL_GROUP_DIM), jnp.float32),
                            jnp.cumsum(uf, axis=1)], axis=1)
    t = jnp.arange(T)
    outs = []
    for gi, win in enumerate(POOL_WINDOWS):
        lo = jnp.maximum(t - win // 2, 0)
        hi = jnp.minimum(t + win - win // 2, T)
        cg = csum[:, :, gi]
        cnt = (hi - lo).astype(jnp.float32)[None, :, None]
        outs.append((cg[:, hi] - cg[:, lo]) / cnt - uf[:, :, gi])
    pooled = jnp.stack(outs, axis=2)
    y = jnp.einsum('btgc,gcd->btgd', pooled, pool_w.astype(jnp.float32)).reshape(B, T, WC)
    return (y * pool_scale.astype(jnp.float32)).astype(u.dtype)


def mixers_context(h, w_in, w_out, rw, pool_w, pool_scale):
    p = h @ w_in
    B = h.shape[0]
    oa, kh, vh = attn_context(p[..., OFF_Q:OFF_K], p[..., OFF_K:OFF_V], p[..., OFF_V:OFF_RKV])
    s0 = jnp.zeros((B, N_HEADS_B, HEAD_DIM_B, HEAD_DIM_B), jnp.float32)
    ob, s_f, s_b = rwkv_mix(p, s0, s0, *rw)
    oc = multiscale_pool(p[..., OFF_POOL:P_IN], pool_w, pool_scale)
    y = jnp.concatenate([oa, ob, oc], axis=-1) @ w_out
    return y, kh, vh, s_f, s_b


def mixers_latent(h, ck, cv, s_f0, s_b0, w_in, w_out, rpb, rw, pool_w, pool_scale):
    p = h @ w_in
    oa = neighbourhood_attention(p[..., OFF_Q:OFF_K], p[..., OFF_K:OFF_V], p[..., OFF_V:OFF_RKV], ck, cv, rpb)
    ob, _, _ = rwkv_mix(p, s_f0, s_b0, *rw)
    oc = multiscale_pool(p[..., OFF_POOL:P_IN], pool_w, pool_scale)
    return jnp.concatenate([oa, ob, oc], axis=-1) @ w_out


def swiglu(h, w1, w3, w2):
    return (jax.nn.silu(h @ w1) * (h @ w3)) @ w2


def moe_swiglu(h, router, w1, w3, w2):
    logits = (h @ router).astype(jnp.float32)
    top_v, top_i = lax.top_k(logits, TOP_K)
    top_p = jax.nn.softmax(top_v, axis=-1)
    gates = jnp.sum(jax.nn.one_hot(top_i, N_EXPERTS, dtype=jnp.float32) * top_p[..., None], axis=-2)
    gates = gates.astype(h.dtype)
    y = gates[..., 0:1] * swiglu(h, w1[0], w3[0], w2[0])
    for e in range(1, N_EXPERTS):
        y = y + gates[..., e:e + 1] * swiglu(h, w1[e], w3[e], w2[e])
    return y


def setup_inputs(seed: int = 0) -> dict:
    key = jax.random.key(seed)
    ks = iter(jax.random.split(key, 48))

    def nrm(shape, s=1.0):
        return jax.random.normal(next(ks), shape, jnp.float32) * s

    d = D_MODEL
    centre = (jnp.arange(CONV_W) == CONV_W // 2).astype(jnp.float32)[None, :, None]
    return {
        "x_prompt": nrm((BATCH, SEQ, d)),
        "x_sample": nrm((DEC_BATCH, DEC_SEQ, d)),
        "cache_attn_k": nrm((DEC_BATCH, DEPTH, PAST_LEN, N_HEADS_A, HEAD_DIM_A)),
        "cache_attn_v": nrm((DEC_BATCH, DEPTH, PAST_LEN, N_HEADS_A, HEAD_DIM_A)),
        "state_rwkv_fwd": nrm((DEC_BATCH, DEPTH, N_HEADS_B, HEAD_DIM_B, HEAD_DIM_B), 0.5),
        "state_rwkv_bwd": nrm((DEC_BATCH, DEPTH, N_HEADS_B, HEAD_DIM_B, HEAD_DIM_B), 0.5),
        "c": nrm((DEC_BATCH, d)),
        "c_ctx": nrm((d,)),
        "norm1_g": 1.0 + nrm((DEPTH, d), 0.02),
        "norm2_g": 1.0 + nrm((DEPTH, d), 0.02),
        "ada_w": nrm((DEPTH, d, 6 * d), 0.5 * d ** -0.5),
        "ada_b": nrm((DEPTH, 6 * d), 0.02),
        "w_in": nrm((DEPTH, d, P_IN), d ** -0.5),
        "w_out": nrm((DEPTH, d, d), d ** -0.5),
        "na_rpb": nrm((DEPTH, N_HEADS_A, 2 * NA_ROWS - 1, 2 * NA_COLS - 1), 0.1),
        "rw_conv": centre + nrm((DEPTH, CONV_W, 3 * WB), 0.2),
        "rw_w0": -1.5 + nrm((DEPTH, 2, WB), 0.5),
        "rw_w2": nrm((DEPTH, 2, DECAY_LORA, WB), 0.5 * DECAY_LORA ** -0.5),
        "rw_a0": nrm((DEPTH, 2, WB), 0.5),
        "rw_a2": nrm((DEPTH, 2, AAA_LORA, WB), AAA_LORA ** -0.5),
        "rw_g2": nrm((DEPTH, GATE_LORA, WB), GATE_LORA ** -0.5),
        "rw_kk": 0.85 + nrm((DEPTH, WB), 0.05),
        "rw_ka": 1.0 + nrm((DEPTH, WB), 0.05),
        "rw_rk": nrm((DEPTH, WB), 0.1),
        "rw_ln_g": 1.0 + nrm((DEPTH, WB), 0.02),
        "rw_ln_b": nrm((DEPTH, WB), 0.02),
        "pool_w": nrm((DEPTH, N_POOL_GROUPS, POOL_GROUP_DIM, POOL_GROUP_DIM), POOL_GROUP_DIM ** -0.5),
        "pool_scale": 1.0 + nrm((DEPTH, WC), 0.1),
        "ffn_w1": nrm((N_DENSE, d, D_FF), d ** -0.5),
        "ffn_w3": nrm((N_DENSE, d, D_FF), d ** -0.5),
        "ffn_w2": nrm((N_DENSE, D_FF, d), D_FF ** -0.5),
        "moe_router": nrm((N_MOE, d, N_EXPERTS), d ** -0.5),
        "moe_w1": nrm((N_MOE, N_EXPERTS, d, D_FF_EXPERT), d ** -0.5),
        "moe_w3": nrm((N_MOE, N_EXPERTS, d, D_FF_EXPERT), d ** -0.5),
        "moe_w2": nrm((N_MOE, N_EXPERTS, D_FF_EXPERT, d), D_FF_EXPERT ** -0.5),
        "final_g": 1.0 + nrm((d,), 0.02),
    }


def reference(x_prompt, x_sample, cache_attn_k, cache_attn_v, state_rwkv_fwd, state_rwkv_bwd, c, c_ctx,
              norm1_g, norm2_g, ada_w, ada_b, w_in, w_out, na_rpb, rw_conv, rw_w0, rw_w2, rw_a0, rw_a2,
              rw_g2, rw_kk, rw_ka, rw_rk, rw_ln_g, rw_ln_b, pool_w, pool_scale, ffn_w1, ffn_w3, ffn_w2,
              moe_router, moe_w1, moe_w3, moe_w2, final_g):
    xp = x_prompt
    xs = x_sample
    ks_new, vs_new, sf_new, sb_new = [], [], [], []
    for l in range(DEPTH):
        rw = (rw_conv[l], rw_w0[l], rw_w2[l], rw_a0[l], rw_a2[l], rw_g2[l], rw_kk[l], rw_ka[l], rw_rk[l],
              rw_ln_g[l], rw_ln_b[l])
        i = l // 2
        sh1, sc1, gt1, sh2, sc2, gt2 = ada_mod(c_ctx, ada_w[l], ada_b[l])
        y, kh, vh, s_f, s_b = mixers_context(modulate(xp, norm1_g[l], sh1, sc1), w_in[l], w_out[l], rw,
                                             pool_w[l], pool_scale[l])
        xp = xp + gt1 * y
        h = modulate(xp, norm2_g[l], sh2, sc2)
        if l % 2 == 0:
            f = swiglu(h, ffn_w1[i], ffn_w3[i], ffn_w2[i])
        else:
            f = moe_swiglu(h, moe_router[i], moe_w1[i], moe_w3[i], moe_w2[i])
        xp = xp + gt2 * f
        ks_new.append(kh)
        vs_new.append(vh)
        sf_new.append(s_f)
        sb_new.append(s_b)
        sh1, sc1, gt1, sh2, sc2, gt2 = ada_mod(c, ada_w[l], ada_b[l])
        y = mixers_latent(modulate(xs, norm1_g[l], sh1, sc1), cache_attn_k[:, l], cache_attn_v[:, l],
                          state_rwkv_fwd[:, l], state_rwkv_bwd[:, l], w_in[l], w_out[l], na_rpb[l], rw,
                          pool_w[l], pool_scale[l])
        xs = xs + gt1 * y
        h = modulate(xs, norm2_g[l], sh2, sc2)
        if l % 2 == 0:
            f = swiglu(h, ffn_w1[i], ffn_w3[i], ffn_w2[i])
        else:
            f = moe_swiglu(h, moe_router[i], moe_w1[i], moe_w3[i], moe_w2[i])
        xs = xs + gt2 * f
    y_prompt = rms_norm(xp, final_g)
    y_sample = rms_norm(xs, final_g)
    new_attn_k = jnp.stack(ks_new, axis=1)
    new_attn_v = jnp.stack(vs_new, axis=1)
    new_state_fwd = jnp.stack(sf_new, axis=1)
    new_state_bwd = jnp.stack(sb_new, axis=1)
    return (y_prompt, y_sample, new_attn_k, new_attn_v, new_state_fwd, new_state_bwd)
```

```python
import functools

import jax
import jax.numpy as jnp
from jax import lax
from jax.experimental import pallas as pl
from jax.experimental.pallas import tpu as pltpu

F32 = jnp.float32
BF16 = jnp.bfloat16
HI = lax.Precision.HIGHEST

D_MODEL = 2048
DEPTH = 2
GRID_W = 64
WA = D_MODEL // 2
WB = D_MODEL // 4
WC = D_MODEL - WA - WB
HEAD_DIM_A = 64
N_HEADS_A = WA // HEAD_DIM_A
HEAD_DIM_B = 64
N_HEADS_B = WB // HEAD_DIM_B
POOL_WINDOWS = (2, 4, 8, 16)
POOL_GROUP_DIM = WC // len(POOL_WINDOWS)
NA_ROWS = 8
NA_COLS = 16
DECAY_LORA = 96
AAA_LORA = 96
GATE_LORA = 256
OFF_Q = 0
OFF_K = WA
OFF_V = 2 * WA
OFF_RKV = 3 * WA
OFF_WL = OFF_RKV + 3 * WB
OFF_AL = OFF_WL + 2 * DECAY_LORA
OFF_GL = OFF_AL + 2 * AAA_LORA
OFF_POOL = OFF_GL + GATE_LORA
P_IN = OFF_POOL + WC
N_EXPERTS = 8
RMS_EPS = 1e-6
GN_EPS = 64e-5
NEG_INF = -1e30

LANE = 128
VMEM_LIMIT = 56 * 1024 * 1024
SCAN_CHUNK = 64


def _cparams(sem):
    return pltpu.CompilerParams(dimension_semantics=sem, vmem_limit_bytes=VMEM_LIMIT)


def _dot(a, b, precision=None):
    return jnp.dot(a, b, preferred_element_type=F32, precision=precision)


def _dot_nt(a, b, precision=None):
    return lax.dot_general(a, b, (((1,), (1,)), ((), ())), preferred_element_type=F32, precision=precision)


def _dot_tn(a, b, precision=None):
    return lax.dot_general(a, b, (((0,), (0,)), ((), ())), preferred_element_type=F32, precision=precision)


def _sigmoid(x):
    return 1.0 / (1.0 + jnp.exp(-x))


def _ada_kernel(c_ref, w_ref, b_ref, o_ref):
    c = c_ref[...]
    s = c * _sigmoid(c)
    o_ref[0] = _dot(s.astype(BF16), w_ref[0].astype(BF16)) + b_ref[0]


def _ada(cond8, ada_w, ada_b):
    depth, d, n6 = ada_w.shape
    tn = 768
    return pl.pallas_call(
        _ada_kernel,
        grid=(depth, n6 // tn),
        in_specs=[pl.BlockSpec((8, d), lambda l, j: (0, 0)),
                  pl.BlockSpec((1, d, tn), lambda l, j: (l, 0, j)),
                  pl.BlockSpec((1, 1, tn), lambda l, j: (l, 0, j))],
        out_specs=pl.BlockSpec((1, 8, tn), lambda l, j: (l, 0, j)),
        out_shape=jax.ShapeDtypeStruct((depth, 8, n6), F32),
        compiler_params=_cparams(("parallel", "parallel")),
        name="ada_mod",
    )(cond8, ada_w, ada_b.reshape(depth, 1, n6))


def _modulated(x, g, sc, sh):
    y = x * lax.rsqrt(jnp.mean(x * x, axis=-1, keepdims=True) + RMS_EPS)
    return (y * g) * (1.0 + sc) + sh


def _group_of_rows(row0, n_ctx, lat_len):
    return jnp.where(row0 < n_ctx, 0, 1 + (row0 - n_ctx) // lat_len)


def _inproj_kernel(x_ref, g_ref, sc_ref, sh_ref, w_ref, o_ref, h_ref):
    @pl.when(pl.program_id(1) == 0)
    def _():
        h_ref[...] = _modulated(x_ref[...], g_ref[...], sc_ref[0], sh_ref[0]).astype(BF16)

    o_ref[...] = _dot(h_ref[...], w_ref[...].astype(BF16))


def _in_proj(x, g, sc, sh, w, n_ctx, lat_len, tm, tn):
    n, d = x.shape
    pin = w.shape[1]
    grp = lambda i, j: (_group_of_rows(i * tm, n_ctx, lat_len), 0, 0)
    return pl.pallas_call(
        _inproj_kernel,
        grid=(n // tm, pin // tn),
        in_specs=[pl.BlockSpec((tm, d), lambda i, j: (i, 0)),
                  pl.BlockSpec((1, d), lambda i, j: (0, 0)),
                  pl.BlockSpec((1, 1, d), grp),
                  pl.BlockSpec((1, 1, d), grp),
                  pl.BlockSpec((d, tn), lambda i, j: (0, j))],
        out_specs=pl.BlockSpec((tm, tn), lambda i, j: (i, j)),
        out_shape=jax.ShapeDtypeStruct((n, pin), F32),
        scratch_shapes=[pltpu.VMEM((tm, d), BF16)],
        compiler_params=_cparams(("parallel", "arbitrary")),
        name="in_proj",
    )(x, g.reshape(1, d), sc, sh, w)


def _normmod_kernel(x_ref, g_ref, sc_ref, sh_ref, h_ref):
    h_ref[...] = _modulated(x_ref[...], g_ref[...], sc_ref[0], sh_ref[0])


def _normmod_router_kernel(x_ref, g_ref, sc_ref, sh_ref, rt_ref, h_ref, gate_ref, idx_ref):
    h = _modulated(x_ref[...], g_ref[...], sc_ref[0], sh_ref[0])
    h_ref[...] = h
    logits = _dot(h, rt_ref[...], precision=HI)
    lane = lax.broadcasted_iota(jnp.int32, logits.shape, 1)
    lanef = lane.astype(F32)
    logits = jnp.where(lane < N_EXPERTS, logits, -jnp.inf)
    m1 = jnp.max(logits, axis=-1, keepdims=True)
    i1 = jnp.min(jnp.where(logits == m1, lanef, float(LANE)), axis=-1, keepdims=True)
    rest = jnp.where(lanef == i1, -jnp.inf, logits)
    m2 = jnp.max(rest, axis=-1, keepdims=True)
    i2 = jnp.min(jnp.where(rest == m2, lanef, float(LANE)), axis=-1, keepdims=True)
    e2 = jnp.exp(m2 - m1)
    p1 = 1.0 / (1.0 + e2)
    p2 = e2 / (1.0 + e2)
    gate_ref[...] = jnp.where(lane == 0, p1, jnp.where(lane == 1, p2, 0.0))
    idx_ref[...] = jnp.where(lane == 0, i1, jnp.where(lane == 1, i2, 0.0)).astype(jnp.int32)


def _norm_mod(x, g, sc, sh, n_ctx, lat_len, tm, router=None):
    n, d = x.shape
    grp = lambda i: (_group_of_rows(i * tm, n_ctx, lat_len), 0, 0)
    in_specs = [pl.BlockSpec((tm, d), lambda i: (i, 0)),
                pl.BlockSpec((1, d), lambda i: (0, 0)),
                pl.BlockSpec((1, 1, d), grp),
                pl.BlockSpec((1, 1, d), grp)]
    row_spec = pl.BlockSpec((tm, d), lambda i: (i, 0))
    if router is None:
        return pl.pallas_call(
            _normmod_kernel, grid=(n // tm,), in_specs=in_specs, out_specs=row_spec,
            out_shape=jax.ShapeDtypeStruct((n, d), F32),
            compiler_params=_cparams(("parallel",)), name="norm_mod",
        )(x, g.reshape(1, d), sc, sh)
    rt = jnp.zeros((d, LANE), F32).at[:, :N_EXPERTS].set(router)
    lane_spec = pl.BlockSpec((tm, LANE), lambda i: (i, 0))
    return pl.pallas_call(
        _normmod_router_kernel, grid=(n // tm,),
        in_specs=in_specs + [pl.BlockSpec((d, LANE), lambda i: (0, 0))],
        out_specs=[row_spec, lane_spec, lane_spec],
        out_shape=[jax.ShapeDtypeStruct((n, d), F32),
                   jax.ShapeDtypeStruct((n, LANE), F32),
                   jax.ShapeDtypeStruct((n, LANE), jnp.int32)],
        compiler_params=_cparams(("parallel",)), name="norm_mod_router",
    )(x, g.reshape(1, d), sc, sh, rt)


def _final_norm_kernel(x_ref, g_ref, o_ref):
    x = x_ref[...]
    o_ref[...] = (x * lax.rsqrt(jnp.mean(x * x, axis=-1, keepdims=True) + RMS_EPS)) * g_ref[...]


def _final_norm(x, g, tm):
    n, d = x.shape
    return pl.pallas_call(
        _final_norm_kernel, grid=(n // tm,),
        in_specs=[pl.BlockSpec((tm, d), lambda i: (i, 0)), pl.BlockSpec((1, d), lambda i: (0, 0))],
        out_specs=pl.BlockSpec((tm, d), lambda i: (i, 0)),
        out_shape=jax.ShapeDtypeStruct((n, d), F32),
        compiler_params=_cparams(("parallel",)), name="final_norm",
    )(x, g.reshape(1, d))


def _softmax_rows(parts):
    m = functools.reduce(jnp.maximum, [jnp.max(s, axis=-1, keepdims=True) for s in parts])
    es = [jnp.exp(s - m) for s in parts]
    inv = 1.0 / functools.reduce(lambda a, b: a + b, [jnp.sum(e, axis=-1, keepdims=True) for e in es])
    return [e * inv for e in es]


def _ctx_attn_kernel(q_ref, k_ref, v_ref, o_ref):
    scale = HEAD_DIM_A ** -0.5
    q, k, v = q_ref[...], k_ref[...], v_ref[...]
    outs = []
    for h in range(LANE // HEAD_DIM_A):
        sl = slice(h * HEAD_DIM_A, (h + 1) * HEAD_DIM_A)
        s = _dot_nt(q[:, sl].astype(BF16), k[:, sl].astype(BF16)) * scale
        (p,) = _softmax_rows([s])
        outs.append(_dot(p.astype(BF16), v[:, sl].astype(BF16)))
    o_ref[...] = jnp.concatenate(outs, axis=-1).astype(o_ref.dtype)


def _ctx_attention(p, n_seq, seq_len):
    nb = WA // LANE
    return pl.pallas_call(
        _ctx_attn_kernel, grid=(n_seq, nb),
        in_specs=[pl.BlockSpec((seq_len, LANE), lambda b, h: (b, OFF_Q // LANE + h)),
                  pl.BlockSpec((seq_len, LANE), lambda b, h: (b, OFF_K // LANE + h)),
                  pl.BlockSpec((seq_len, LANE), lambda b, h: (b, OFF_V // LANE + h))],
        out_specs=pl.BlockSpec((seq_len, LANE), lambda b, h: (b, h)),
        out_shape=jax.ShapeDtypeStruct((n_seq * seq_len, WA), BF16),
        compiler_params=_cparams(("parallel", "parallel")), name="ctx_attention",
    )(p, p, p)


def _na_bias_table(rpb):
    col = jnp.arange(GRID_W)
    cs = jnp.clip(col - NA_COLS // 2, 0, GRID_W - NA_COLS)
    valid = (col[None, :] >= cs[:, None]) & (col[None, :] < cs[:, None] + NA_COLS)
    dci = jnp.clip(col[None, :] - col[:, None] + NA_COLS - 1, 0, 2 * NA_COLS - 2)
    dri = jnp.arange(NA_ROWS)[:, None] + jnp.arange(NA_ROWS)[None, :]
    tab = rpb.astype(F32)[:, dri[:, None, :, None], dci[None, :, None, :]]
    tab = jnp.where(valid[None, None, :, None, :], tab, NEG_INF)
    return tab.reshape(rpb.shape[0], NA_ROWS, GRID_W, NA_ROWS * GRID_W)


def _na_kernel(q_ref, k_ref, v_ref, ck_ref, cv_ref, tab_ref, o_ref, *, rows):
    scale = HEAD_DIM_A ** -0.5
    band = NA_ROWS * GRID_W
    ck, cv = ck_ref[0], cv_ref[0]

    def body(r, carry):
        rs = jnp.clip(r - NA_ROWS // 2, 0, rows - NA_ROWS)
        d0 = rs - r + NA_ROWS - 1
        q = q_ref[pl.ds(pl.multiple_of(r * GRID_W, GRID_W), GRID_W), :]
        kb = k_ref[pl.ds(pl.multiple_of(rs * GRID_W, GRID_W), band), :]
        vb = v_ref[pl.ds(pl.multiple_of(rs * GRID_W, GRID_W), band), :]
        outs = []
        for h in range(LANE // HEAD_DIM_A):
            sl = slice(h * HEAD_DIM_A, (h + 1) * HEAD_DIM_A)
            qh = q[:, sl].astype(BF16)
            s_loc = _dot_nt(qh, kb[:, sl].astype(BF16)) * scale + tab_ref[h, d0]
            s_ctx = _dot_nt(qh, ck[:, sl].astype(BF16)) * scale
            p_loc, p_ctx = _softmax_rows([s_loc, s_ctx])
            outs.append(_dot(p_loc.astype(BF16), vb[:, sl].astype(BF16))
                        + _dot(p_ctx.astype(BF16), cv[:, sl].astype(BF16)))
        o_ref[pl.ds(pl.multiple_of(r * GRID_W, GRID_W), GRID_W), :] = (
            jnp.concatenate(outs, axis=-1).astype(o_ref.dtype))
        return carry

    lax.fori_loop(0, rows, body, 0)


def _nbr_attention(p, ck, cv, tab, n_ctx, n_seq, seq_len):
    nb = WA // LANE
    past = ck.shape[1]
    rb0 = n_ctx // seq_len
    hp = LANE // HEAD_DIM_A
    return pl.pallas_call(
        functools.partial(_na_kernel, rows=seq_len // GRID_W), grid=(n_seq, nb),
        in_specs=[pl.BlockSpec((seq_len, LANE), lambda b, h: (rb0 + b, OFF_Q // LANE + h)),
                  pl.BlockSpec((seq_len, LANE), lambda b, h: (rb0 + b, OFF_K // LANE + h)),
                  pl.BlockSpec((seq_len, LANE), lambda b, h: (rb0 + b, OFF_V // LANE + h)),
                  pl.BlockSpec((1, past, LANE), lambda b, h: (b, 0, h)),
                  pl.BlockSpec((1, past, LANE), lambda b, h: (b, 0, h)),
                  pl.BlockSpec((hp, NA_ROWS, GRID_W, NA_ROWS * GRID_W), lambda b, h: (h, 0, 0, 0))],
        out_specs=pl.BlockSpec((seq_len, LANE), lambda b, h: (b, h)),
        out_shape=jax.ShapeDtypeStruct((n_seq * seq_len, WA), BF16),
        compiler_params=_cparams(("parallel", "parallel")), name="nbr_attention",
    )(p, p, p, ck, cv, tab)


def _seq_pos(shape, row0, seq_len):
    return (row0 + lax.broadcasted_iota(jnp.int32, shape, 0)) & (seq_len - 1)


def _shifted(x, t, d, seq_len):
    n = x.shape[0]
    y = pltpu.roll(x, (-d) % n, 0)
    return jnp.where((t + d >= 0) & (t + d < seq_len), y, 0.0)


def _conv_kernel(x_ref, w_ref, o_ref, *, n_ctx, ctx_len, lat_len):
    rb = x_ref.shape[0]
    row0 = pl.program_id(0) * rb
    seq_len = jnp.where(row0 < n_ctx, ctx_len, lat_len)
    x = x_ref[...]
    t = _seq_pos(x.shape, row0, seq_len)
    w = w_ref[...]
    o_ref[...] = (_shifted(x, t, -1, seq_len) * w[0:1] + x * w[1:2] + _shifted(x, t, 1, seq_len) * w[2:3])


def _short_conv(p, conv_w, n_ctx, ctx_len, lat_len):
    n = p.shape[0]
    rb = lat_len
    c = conv_w.shape[1]
    w8 = jnp.zeros((8, c), F32).at[:3].set(conv_w)
    return pl.pallas_call(
        functools.partial(_conv_kernel, n_ctx=n_ctx, ctx_len=ctx_len, lat_len=lat_len),
        grid=(n // rb, c // LANE),
        in_specs=[pl.BlockSpec((rb, LANE), lambda i, j: (i, OFF_RKV // LANE + j)),
                  pl.BlockSpec((8, LANE), lambda i, j: (0, j))],
        out_specs=pl.BlockSpec((rb, LANE), lambda i, j: (i, j)),
        out_shape=jax.ShapeDtypeStruct((n, c), F32),
        compiler_params=_cparams(("parallel", "parallel")), name="short_conv",
    )(p, w8)


def _pool_kernel(u0_ref, u1_ref, u2_ref, u3_ref, w_ref, sc_ref, o_ref, *, n_ctx, ctx_len, lat_len):
    rb = o_ref.shape[0]
    row0 = pl.program_id(0) * rb
    seq_len = jnp.where(row0 < n_ctx, ctx_len, lat_len)
    t = _seq_pos((rb, POOL_GROUP_DIM), row0, seq_len)
    for gi, (win, u_ref) in enumerate(zip(POOL_WINDOWS, (u0_ref, u1_ref, u2_ref, u3_ref))):
        sl = slice(gi * POOL_GROUP_DIM, (gi + 1) * POOL_GROUP_DIM)
        u = u_ref[...]
        acc = u
        for d in range(-(win // 2), win - win // 2):
            if d != 0:
                acc = acc + _shifted(u, t, d, seq_len)
        lo = jnp.maximum(t - win // 2, 0)
        hi = jnp.minimum(t + win - win // 2, seq_len)
        pooled = acc / (hi - lo).astype(F32) - u
        y = _dot(pooled.astype(BF16), w_ref[gi].astype(BF16))
        o_ref[:, sl] = (y * sc_ref[:, sl]).astype(o_ref.dtype)


def _multiscale_pool(p, pool_w, pool_scale, n_ctx, ctx_len, lat_len):
    n = p.shape[0]
    rb = lat_len
    gd = POOL_GROUP_DIM
    group = lambda gi: pl.BlockSpec((rb, gd), lambda i: (i, OFF_POOL // gd + gi))
    return pl.pallas_call(
        functools.partial(_pool_kernel, n_ctx=n_ctx, ctx_len=ctx_len, lat_len=lat_len),
        grid=(n // rb,),
        in_specs=[group(0), group(1), group(2), group(3),
                  pl.BlockSpec(pool_w.shape, lambda i: (0, 0, 0)),
                  pl.BlockSpec((1, WC), lambda i: (0, 0))],
        out_specs=pl.BlockSpec((rb, WC), lambda i: (i, 0)),
        out_shape=jax.ShapeDtypeStruct((n, WC), BF16),
        compiler_params=_cparams(("parallel",)), name="multiscale_pool",
    )(p, p, p, p, pool_w, pool_scale.reshape(1, WC))


def _head_ones():
    r = jnp.arange(LANE) // HEAD_DIM_B
    return (r[:, None] == r[None, :]).astype(F32)


def _rwkv_prep_kernel(r_ref, k_ref, v_ref, xwa_ref, xg0_ref, xg1_ref, wl_ref, g2a_ref, g2b_ref, pt_ref,
                      ones_ref, kk_o, lw_o, b_o, kd_o, g_o, bonus_o):
    r, k, v = r_ref[...], k_ref[...], v_ref[...]
    xwa = xwa_ref[...]
    lane = lax.broadcasted_iota(jnp.int32, xwa.shape, 1)
    act = jnp.where(lane < 2 * DECAY_LORA, jnp.tanh(xwa), xwa)
    lora = _dot(act.astype(BF16), wl_ref[...].astype(BF16))
    pt = pt_ref[...]
    ones = ones_ref[...]
    kkr = k * pt[4:5]
    kk = kkr * lax.rsqrt(_dot(kkr * kkr, ones, precision=HI) + 1e-12)
    kk_o[...] = kk
    for d in range(2):
        z = -(pt[d:d + 1] + lora[:, d * LANE:(d + 1) * LANE])
        softplus = jnp.maximum(z, 0.0) + jnp.log(1.0 + jnp.exp(-jnp.abs(z)))
        lw_o[d] = -jnp.exp(-softplus - 0.5)
        a = _sigmoid(pt[2 + d:3 + d] + lora[:, (2 + d) * LANE:(3 + d) * LANE])
        kd_o[d] = k * (1.0 + (a - 1.0) * pt[5:6])
        b_o[d] = kk * a
    g_o[...] = (_dot(_sigmoid(xg0_ref[...]).astype(BF16), g2a_ref[...].astype(BF16))
                + _dot(_sigmoid(xg1_ref[...]).astype(BF16), g2b_ref[...].astype(BF16)))
    bonus_o[...] = _dot(r * k * pt[6:7], ones, precision=HI) * v


def _rwkv_prep(p, rkv, w0, w2, a0, a2, g2, k_k, k_a, r_k, rb):
    n = p.shape[0]
    nhp = WB // LANE
    wl = jnp.zeros((4, DECAY_LORA, 4, WB), F32)
    wl = wl.at[0, :, 0].set(w2[0]).at[1, :, 1].set(w2[1]).at[2, :, 2].set(a2[0]).at[3, :, 3].set(a2[1])
    wl = wl.reshape(4 * DECAY_LORA, 4, nhp, LANE).transpose(0, 2, 1, 3).reshape(4 * DECAY_LORA, nhp * 4 * LANE)
    pt = jnp.stack([w0[0], w0[1], a0[0], a0[1], k_k, k_a, r_k, jnp.zeros_like(k_k)]).astype(F32)
    nlo = 4 * DECAY_LORA
    col = lambda off: (lambda i, h: (i, off + h))
    tok = jax.ShapeDtypeStruct((n, WB), F32)
    tok2 = jax.ShapeDtypeStruct((2, n, WB), F32)
    spec1 = pl.BlockSpec((rb, LANE), lambda i, h: (i, h))
    spec2 = pl.BlockSpec((2, rb, LANE), lambda i, h: (0, i, h))
    return pl.pallas_call(
        _rwkv_prep_kernel, grid=(n // rb, nhp),
        in_specs=[pl.BlockSpec((rb, LANE), col(0)),
                  pl.BlockSpec((rb, LANE), col(nhp)),
                  pl.BlockSpec((rb, LANE), col(2 * nhp)),
                  pl.BlockSpec((rb, nlo), lambda i, h: (i, OFF_WL // nlo)),
                  pl.BlockSpec((rb, LANE), lambda i, h: (i, OFF_GL // LANE)),
                  pl.BlockSpec((rb, LANE), lambda i, h: (i, OFF_GL // LANE + 1)),
                  pl.BlockSpec((nlo, 4 * LANE), lambda i, h: (0, h)),
                  pl.BlockSpec((LANE, LANE), lambda i, h: (0, h)),
                  pl.BlockSpec((LANE, LANE), lambda i, h: (1, h)),
                  pl.BlockSpec((8, LANE), lambda i, h: (0, h)),
                  pl.BlockSpec((LANE, LANE), lambda i, h: (0, 0))],
        out_specs=[spec1, spec2, spec2, spec2, spec1, spec1],
        out_shape=[tok, tok2, tok2, tok2, tok, tok],
        compiler_params=_cparams(("parallel", "parallel")), name="rwkv_prep",
    )(rkv, rkv, rkv, p, p, p, wl, g2, g2, pt, _head_ones())


def _scan_kernel(*refs, n_chunks, has_s0):
    if has_s0:
        r_ref, v_ref, kk_ref, lw_ref, b_ref, kd_ref, s0_ref, o_ref, se_ref, s_ref = refs
    else:
        r_ref, v_ref, kk_ref, lw_ref, b_ref, kd_ref, o_ref, se_ref, s_ref = refs
    c = SCAN_CHUNK
    hd = HEAD_DIM_B
    d = pl.program_id(0)
    ci = pl.program_id(2)

    @pl.when(ci == 0)
    def _():
        s_ref[...] = s0_ref[0, 0] if has_s0 else jnp.zeros(s_ref.shape, F32)

    row = lax.broadcasted_iota(jnp.int32, (c, c), 0)
    col = lax.broadcasted_iota(jnp.int32, (c, c), 1)
    strict = (col - row) * (1 - 2 * d) < 0
    incl = strict | (col == row)
    eye = (col == row).astype(F32)

    lw = lw_ref[0]
    cum = _dot(incl.astype(F32), lw, precision=HI)
    tot = jnp.sum(lw, axis=0, keepdims=True)
    r, v, kk, b, kd = r_ref[...], v_ref[...], kk_ref[...], b_ref[0], kd_ref[0]
    e_neg = jnp.exp(-cum)
    e_end = jnp.exp(tot - cum)
    kk_h = kk * jnp.exp(cum - lw)
    r_h = r * jnp.exp(cum)
    b_t, k_t = b * e_neg, kd * e_neg
    b_e, k_e = b * e_end, kd * e_end
    g_end = jnp.exp(tot)

    outs = []
    for h in range(N_HEADS_B):
        sl = slice(h * hd, (h + 1) * hd)
        z = jnp.concatenate([kk_h[:, sl], r_h[:, sl]], axis=0)
        y = jnp.concatenate([b_t[:, sl], k_t[:, sl]], axis=0)
        g = _dot_nt(z, y, precision=HI)
        m_b = jnp.where(strict, g[:c, :c], 0.0)
        m_k = jnp.where(strict, g[:c, c:], 0.0)
        p_b = jnp.where(incl, g[c:, :c], 0.0)
        p_k = jnp.where(incl, g[c:, c:], 0.0)
        x = -m_b
        tinv = eye + x
        for _ in range(c.bit_length() - 2):
            x = _dot(x, x, precision=HI)
            tinv = tinv + _dot(tinv, x, precision=HI)
        s = s_ref[h]
        zs = _dot_nt(z, s, precision=HI)
        vh = v[:, sl]
        u = -_dot(tinv, zs[:c] + _dot(m_k, vh, precision=HI), precision=HI)
        outs.append(zs[c:] + _dot(p_b, u, precision=HI) + _dot(p_k, vh, precision=HI))
        w = jnp.concatenate([u, vh], axis=0)
        ye = jnp.concatenate([b_e[:, sl], k_e[:, sl]], axis=0)
        s_ref[h] = s * g_end[:, sl] + _dot_tn(w, ye, precision=HI)
    o_ref[0] = jnp.concatenate(outs, axis=-1)

    @pl.when(ci == n_chunks - 1)
    def _():
        se_ref[0, 0] = s_ref[...]


def _rwkv_scan(rkv, kk, lw, b, kd, s0, row0, n_seq, seq_len):
    c = SCAN_CHUNK
    n_chunks = seq_len // c
    blk0 = row0 // c
    nb = WB // WB

    def rb(d, s, ci):
        return blk0 + s * n_chunks + ci + d * (n_chunks - 1 - 2 * ci)

    tok = lambda colblk: pl.BlockSpec((c, WB), lambda d, s, ci: (rb(d, s, ci), colblk))
    tok2 = pl.BlockSpec((1, c, WB), lambda d, s, ci: (d, rb(d, s, ci), 0))
    st = pl.BlockSpec((1, 1, N_HEADS_B, HEAD_DIM_B, HEAD_DIM_B), lambda d, s, ci: (d, s, 0, 0, 0))
    in_specs = [tok(0), tok(2 * nb), tok(0), tok2, tok2, tok2]
    args = [rkv, rkv, kk, lw, b, kd]
    if s0 is not None:
        in_specs.append(st)
        args.append(s0)
    return pl.pallas_call(
        functools.partial(_scan_kernel, n_chunks=n_chunks, has_s0=s0 is not None),
        grid=(2, n_seq, n_chunks),
        in_specs=in_specs,
        out_specs=[pl.BlockSpec((1, c, WB), lambda d, s, ci: (d, rb(d, s, ci) - blk0, 0)), st],
        out_shape=[jax.ShapeDtypeStruct((2, n_seq * seq_len, WB), F32),
                   jax.ShapeDtypeStruct((2, n_seq, N_HEADS_B, HEAD_DIM_B, HEAD_DIM_B), F32)],
        scratch_shapes=[pltpu.VMEM((N_HEADS_B, HEAD_DIM_B, HEAD_DIM_B), F32)],
        compiler_params=_cparams(("parallel", "parallel", "arbitrary")), name="rwkv_scan",
    )(*args)


def _rwkv_post_kernel(o_ref, bonus_ref, g_ref, lng_ref, lnb_ref, ones_ref, y_ref):
    o = o_ref[0] + o_ref[1]
    mean_mat = ones_ref[...] * (1.0 / HEAD_DIM_B)
    mu = _dot(o, mean_mat, precision=HI)
    var = _dot(jnp.square(o - mu), mean_mat, precision=HI)
    y = (o - mu) * lax.rsqrt(var + GN_EPS) * lng_ref[...] + lnb_ref[...]
    y_ref[...] = ((y + bonus_ref[...]) * g_ref[...]).astype(y_ref.dtype)


def _rwkv_post(o2, bonus, g, ln_g, ln_b, rb):
    n = bonus.shape[0]
    nhp = WB // LANE
    spec = pl.BlockSpec((rb, LANE), lambda i, h: (i, h))
    vec = pl.BlockSpec((1, LANE), lambda i, h: (0, h))
    return pl.pallas_call(
        _rwkv_post_kernel, grid=(n // rb, nhp),
        in_specs=[pl.BlockSpec((2, rb, LANE), lambda i, h: (0, i, h)), spec, spec, vec, vec,
                  pl.BlockSpec((LANE, LANE), lambda i, h: (0, 0))],
        out_specs=spec,
        out_shape=jax.ShapeDtypeStruct((n, WB), BF16),
        compiler_params=_cparams(("parallel", "parallel")), name="rwkv_post",
    )(o2, bonus, g, ln_g.reshape(1, WB), ln_b.reshape(1, WB), _head_ones())


def _outproj_kernel(a_ref, b_ref, c_ref, w_ref, x_ref, gt_ref, o_ref):
    y = (_dot(a_ref[...], w_ref[0:WA].astype(BF16))
         + _dot(b_ref[...], w_ref[WA:WA + WB].astype(BF16))
         + _dot(c_ref[...], w_ref[WA + WB:].astype(BF16)))
    o_ref[...] = x_ref[...] + gt_ref[0] * y


def _out_proj(oa, ob, oc, w, x, gt, n_ctx, lat_len, tm, tn):
    n, d = x.shape
    grp = lambda i, j: (_group_of_rows(i * tm, n_ctx, lat_len), 0, j)
    return pl.pallas_call(
        _outproj_kernel, grid=(n // tm, d // tn),
        in_specs=[pl.BlockSpec((tm, WA), lambda i, j: (i, 0)),
                  pl.BlockSpec((tm, WB), lambda i, j: (i, 0)),
                  pl.BlockSpec((tm, WC), lambda i, j: (i, 0)),
                  pl.BlockSpec((d, tn), lambda i, j: (0, j)),
                  pl.BlockSpec((tm, tn), lambda i, j: (i, j)),
                  pl.BlockSpec((1, 1, tn), grp)],
        out_specs=pl.BlockSpec((tm, tn), lambda i, j: (i, j)),
        out_shape=jax.ShapeDtypeStruct((n, d), F32),
        compiler_params=_cparams(("parallel", "parallel")), name="out_proj",
    )(oa, ob, oc, w, x, gt)


def _swiglu_kernel(te_ref, nt_ref, h_ref, gate_ref, w1_ref, w3_ref, w2_ref, o_ref, *, n_f):
    i = pl.program_id(0)
    f = pl.program_id(1)

    @pl.when((i >= nt_ref[0]) & (f == 0))
    def _():
        o_ref[...] = jnp.zeros(o_ref.shape, F32)

    @pl.when(i < nt_ref[0])
    def _():
        h = h_ref[...].astype(BF16)
        a = _dot(h, w1_ref[0].astype(BF16))
        b = _dot(h, w3_ref[0].astype(BF16))
        act = (a * _sigmoid(a) * b).astype(BF16)
        y = _dot(act, w2_ref[0].astype(BF16))

        @pl.when(f == 0)
        def _():
            o_ref[...] = y

        @pl.when(f > 0)
        def _():
            o_ref[...] += y

        @pl.when(f == n_f - 1)
        def _():
            o_ref[...] *= gate_ref[...]


def _swiglu(h, gate, tile_expert, n_tiles_used, w1, w3, w2, tm, tf):
    n, d = h.shape
    ff = w1.shape[2]
    n_f = ff // tf
    return pl.pallas_call(
        functools.partial(_swiglu_kernel, n_f=n_f),
        grid_spec=pltpu.PrefetchScalarGridSpec(
            num_scalar_prefetch=2, grid=(n // tm, n_f),
            in_specs=[pl.BlockSpec((tm, d), lambda i, f, te, nt: (i, 0)),
                      pl.BlockSpec((tm, 1), lambda i, f, te, nt: (i, 0)),
                      pl.BlockSpec((1, d, tf), lambda i, f, te, nt: (te[i], 0, f)),
                      pl.BlockSpec((1, d, tf), lambda i, f, te, nt: (te[i], 0, f)),
                      pl.BlockSpec((1, tf, d), lambda i, f, te, nt: (te[i], f, 0))],
            out_specs=pl.BlockSpec((tm, d), lambda i, f, te, nt: (i, 0))),
        out_shape=jax.ShapeDtypeStruct((n, d), F32),
        compiler_params=_cparams(("parallel", "arbitrary")), name="swiglu",
    )(tile_expert, n_tiles_used, h, gate, w1, w3, w2)


def _residual_kernel(x_ref, f_ref, gt_ref, o_ref):
    o_ref[...] = x_ref[...] + gt_ref[0] * f_ref[...]


def _residual(x, f, gt, n_ctx, lat_len, tm):
    n, d = x.shape
    row = pl.BlockSpec((tm, d), lambda i: (i, 0))
    return pl.pallas_call(
        _residual_kernel, grid=(n // tm,),
        in_specs=[row, row, pl.BlockSpec((1, 1, d), lambda i: (_group_of_rows(i * tm, n_ctx, lat_len), 0, 0))],
        out_specs=row, out_shape=jax.ShapeDtypeStruct((n, d), F32),
        compiler_params=_cparams(("parallel",)), name="residual",
    )(x, f, gt)


def _gather_rows_kernel(idx_ref, src_ref, o_ref, sem):
    rt = o_ref.shape[0]
    base = pl.program_id(0) * rt

    def copy(r):
        return pltpu.make_async_copy(src_ref.at[pl.ds(idx_ref[base + r], 1)], o_ref.at[pl.ds(r, 1)], sem)

    def start(r, carry):
        copy(r).start()
        return carry

    def wait(r, carry):
        copy(r).wait()
        return carry

    lax.fori_loop(0, rt, start, 0)
    lax.fori_loop(0, rt, wait, 0)


def _gather_rows(src, idx, rt):
    n_out = idx.shape[0]
    d = src.shape[1]
    return pl.pallas_call(
        _gather_rows_kernel,
        grid_spec=pltpu.PrefetchScalarGridSpec(
            num_scalar_prefetch=1, grid=(n_out // rt,),
            in_specs=[pl.BlockSpec(memory_space=pl.ANY)],
            out_specs=pl.BlockSpec((rt, d), lambda i, idx: (i, 0)),
            scratch_shapes=[pltpu.SemaphoreType.DMA(())]),
        out_shape=jax.ShapeDtypeStruct((n_out, d), src.dtype),
        compiler_params=_cparams(("arbitrary",)), name="gather_rows",
    )(idx, src)


def _combine_kernel(x_ref, y1_ref, y2_ref, gt_ref, o_ref):
    o_ref[...] = x_ref[...] + gt_ref[0] * (y1_ref[...] + y2_ref[...])


def _combine(x, y12, gt, n_ctx, lat_len, tm):
    n, d = x.shape
    nb = n // tm
    row = pl.BlockSpec((tm, d), lambda i: (i, 0))
    return pl.pallas_call(
        _combine_kernel, grid=(nb,),
        in_specs=[row, row, pl.BlockSpec((tm, d), lambda i: (nb + i, 0)),
                  pl.BlockSpec((1, 1, d), lambda i: (_group_of_rows(i * tm, n_ctx, lat_len), 0, 0))],
        out_specs=row, out_shape=jax.ShapeDtypeStruct((n, d), F32),
        compiler_params=_cparams(("parallel",)), name="moe_combine",
    )(x, y12, y12, gt)


def _moe(x, h, gates, idx, w1, w3, w2, gt, n_ctx, lat_len, tm, tf, rt):
    n, d = x.shape
    n_e = w1.shape[0]
    e = jnp.concatenate([idx[:, 0], idx[:, 1]])
    gsel = jnp.concatenate([gates[:, 0], gates[:, 1]])
    onehot = (e[:, None] == jnp.arange(n_e)[None, :]).astype(jnp.int32)
    rank = jnp.take_along_axis(jnp.cumsum(onehot, axis=0), e[:, None], axis=1)[:, 0] - 1
    counts = jnp.sum(onehot, axis=0)
    tiles = (counts + tm - 1) // tm
    tile_end = jnp.cumsum(tiles)
    start = (tile_end - tiles) * tm
    pos = (start[e] + rank).astype(jnp.int32)
    n_rows = 2 * n + n_e * tm
    src = jnp.zeros((n_rows,), jnp.int32).at[pos].set(jnp.tile(jnp.arange(n, dtype=jnp.int32), 2))
    gate_rows = jnp.zeros((n_rows,), F32).at[pos].set(gsel).reshape(n_rows, 1)
    tile_ids = jnp.arange(n_rows // tm)
    tile_expert = jnp.minimum(jnp.sum(tile_ids[:, None] >= tile_end[None, :], axis=1), n_e - 1).astype(jnp.int32)
    n_used = tile_end[-1:].astype(jnp.int32)

    hs = _gather_rows(h, src, rt)
    ys = _swiglu(hs, gate_rows, tile_expert, n_used, w1, w3, w2, tm, tf)
    y12 = _gather_rows(ys, pos, rt)
    return _combine(x, y12, gt, n_ctx, lat_len, rt)


def _forward(x_prompt, x_sample, cache_attn_k, cache_attn_v, state_rwkv_fwd, state_rwkv_bwd, c, c_ctx,
             norm1_g, norm2_g, ada_w, ada_b, w_in, w_out, na_rpb, rw_conv, rw_w0, rw_w2, rw_a0, rw_a2,
             rw_g2, rw_kk, rw_ka, rw_rk, rw_ln_g, rw_ln_b, pool_w, pool_scale, ffn_w1, ffn_w3, ffn_w2,
             moe_router, moe_w1, moe_w3, moe_w2, final_g, *, tm, tn_in, tn_out, tf_dense, tm_moe, tf_moe, rt, rb):
    n_cs, ctx_len, d = x_prompt.shape
    n_ls, lat_len, _ = x_sample.shape
    n_ctx = n_cs * ctx_len
    n_lat = n_ls * lat_len
    n = n_ctx + n_lat
    depth = w_in.shape[0]
    past = cache_attn_k.shape[2]
    assert ctx_len & (ctx_len - 1) == 0 and lat_len & (lat_len - 1) == 0
    assert lat_len % ctx_len == 0 and n_ctx % lat_len == 0 and lat_len % GRID_W == 0

    x = jnp.concatenate([x_prompt.reshape(n_ctx, d), x_sample.reshape(n_lat, d)], axis=0)
    cond8 = jnp.zeros((8, d), F32).at[0].set(c_ctx).at[1:1 + n_ls].set(c)
    mods = _ada(cond8, ada_w, ada_b).reshape(depth, 8, 6, 1, d)

    ks_new, vs_new, sf_new, sb_new = [], [], [], []
    ones_gate = jnp.ones((n, 1), F32)
    dense_tiles = jnp.zeros((n // tm,), jnp.int32)
    dense_used = jnp.full((1,), n // tm, jnp.int32)
    for l in range(depth):
        m = [mods[l, :1 + n_ls, k] for k in range(6)]
        p = _in_proj(x, norm1_g[l], m[1], m[0], w_in[l], n_ctx, lat_len, tm, tn_in)

        oa_ctx = _ctx_attention(p, n_cs, ctx_len)
        ck = cache_attn_k[:, l].reshape(n_ls, past, WA)
        cv = cache_attn_v[:, l].reshape(n_ls, past, WA)
        oa_lat = _nbr_attention(p, ck, cv, _na_bias_table(na_rpb[l]), n_ctx, n_ls, lat_len)
        oa = jnp.concatenate([oa_ctx, oa_lat], axis=0)

        rkv = _short_conv(p, rw_conv[l], n_ctx, ctx_len, lat_len)
        kk, lw, b, kd, g, bonus = _rwkv_prep(p, rkv, rw_w0[l], rw_w2[l], rw_a0[l], rw_a2[l], rw_g2[l],
                                             rw_kk[l], rw_ka[l], rw_rk[l], rb)
        o_ctx, s_ctx = _rwkv_scan(rkv, kk, lw, b, kd, None, 0, n_cs, ctx_len)
        s0 = jnp.stack([state_rwkv_fwd[:, l], state_rwkv_bwd[:, l]]).astype(F32)
        o_lat, _ = _rwkv_scan(rkv, kk, lw, b, kd, s0, n_ctx, n_ls, lat_len)
        ob = _rwkv_post(jnp.concatenate([o_ctx, o_lat], axis=1), bonus, g, rw_ln_g[l], rw_ln_b[l], rb)

        oc = _multiscale_pool(p, pool_w[l], pool_scale[l], n_ctx, ctx_len, lat_len)
        x = _out_proj(oa, ob, oc, w_out[l], x, m[2], n_ctx, lat_len, tm, tn_out)

        ks_new.append(p[:n_ctx, OFF_K:OFF_V].reshape(n_cs, ctx_len, N_HEADS_A, HEAD_DIM_A))
        vs_new.append(p[:n_ctx, OFF_V:OFF_RKV].reshape(n_cs, ctx_len, N_HEADS_A, HEAD_DIM_A))
        sf_new.append(s_ctx[0])
        sb_new.append(s_ctx[1])

        i = l // 2
        if l % 2 == 0:
            h = _norm_mod(x, norm2_g[l], m[4], m[3], n_ctx, lat_len, tm)
            f = _swiglu(h, ones_gate, dense_tiles, dense_used, ffn_w1[i:i + 1], ffn_w3[i:i + 1], ffn_w2[i:i + 1],
                        tm, tf_dense)
            x = _residual(x, f, m[5], n_ctx, lat_len, tm)
        else:
            h, gates, idx = _norm_mod(x, norm2_g[l], m[4], m[3], n_ctx, lat_len, tm, router=moe_router[i])
            x = _moe(x, h, gates, idx, moe_w1[i], moe_w3[i], moe_w2[i], m[5], n_ctx, lat_len, tm_moe, tf_moe, rt)

    y = _final_norm(x, final_g, tm)
    return (y[:n_ctx].reshape(n_cs, ctx_len, d), y[n_ctx:].reshape(n_ls, lat_len, d),
            jnp.stack(ks_new, axis=1), jnp.stack(vs_new, axis=1),
            jnp.stack(sf_new, axis=1), jnp.stack(sb_new, axis=1))


def kernel(x_prompt, x_sample, cache_attn_k, cache_attn_v, state_rwkv_fwd, state_rwkv_bwd, c, c_ctx, norm1_g, norm2_g, ada_w, ada_b, w_in, w_out, na_rpb, rw_conv, rw_w0, rw_w2, rw_a0, rw_a2, rw_g2, rw_kk, rw_ka, rw_rk, rw_ln_g, rw_ln_b, pool_w, pool_scale, ffn_w1, ffn_w3, ffn_w2, moe_router, moe_w1, moe_w3, moe_w2, final_g):
    return _forward(x_prompt, x_sample, cache_attn_k, cache_attn_v, state_rwkv_fwd, state_rwkv_bwd, c, c_ctx,
                    norm1_g, norm2_g, ada_w, ada_b, w_in, w_out, na_rpb, rw_conv, rw_w0, rw_w2, rw_a0, rw_a2,
                    rw_g2, rw_kk, rw_ka, rw_rk, rw_ln_g, rw_ln_b, pool_w, pool_scale, ffn_w1, ffn_w3, ffn_w2,
                    moe_router, moe_w1, moe_w3, moe_w2, final_g,
                    tm=512, tn_in=640, tn_out=512, tf_dense=256, tm_moe=1024, tf_moe=256, rt=256, rb=512)
```

```python
import functools

import jax
import jax.numpy as jnp
from jax import lax
from jax.experimental import pallas as pl
from jax.experimental.pallas import tpu as pltpu

F32 = jnp.float32
BF16 = jnp.bfloat16
HI = lax.Precision.HIGHEST

D_MODEL = 2048
DEPTH = 2
GRID_W = 64
WA = D_MODEL // 2
WB = D_MODEL // 4
WC = D_MODEL - WA - WB
HEAD_DIM_A = 64
N_HEADS_A = WA // HEAD_DIM_A
HEAD_DIM_B = 64
N_HEADS_B = WB // HEAD_DIM_B
POOL_WINDOWS = (2, 4, 8, 16)
POOL_GROUP_DIM = WC // len(POOL_WINDOWS)
NA_ROWS = 8
NA_COLS = 16
DECAY_LORA = 96
AAA_LORA = 96
GATE_LORA = 256
OFF_Q = 0
OFF_K = WA
OFF_V = 2 * WA
OFF_RKV = 3 * WA
OFF_WL = OFF_RKV + 3 * WB
OFF_AL = OFF_WL + 2 * DECAY_LORA
OFF_GL = OFF_AL + 2 * AAA_LORA
OFF_POOL = OFF_GL + GATE_LORA
P_IN = OFF_POOL + WC
N_EXPERTS = 8
RMS_EPS = 1e-6
GN_EPS = 64e-5
NEG_INF = -1e30

LANE = 128
VMEM_LIMIT = 56 * 1024 * 1024
SCAN_CHUNK = 64
NORM_ROWS = 256


def _cparams(sem):
    return pltpu.CompilerParams(dimension_semantics=sem, vmem_limit_bytes=VMEM_LIMIT)


def _dot(a, b, precision=None):
    return jnp.dot(a, b, preferred_element_type=F32, precision=precision)


def _dot_nt(a, b, precision=None):
    return lax.dot_general(a, b, (((1,), (1,)), ((), ())), preferred_element_type=F32, precision=precision)


def _dot_tn(a, b, precision=None):
    return lax.dot_general(a, b, (((0,), (0,)), ((), ())), preferred_element_type=F32, precision=precision)


def _sigmoid(x):
    return 1.0 / (1.0 + jnp.exp(-x))


def _ada_kernel(c_ref, w_ref, b_ref, o_ref):
    c = c_ref[...]
    s = c * _sigmoid(c)
    o_ref[0] = _dot(s.astype(BF16), w_ref[0].astype(BF16)) + b_ref[0]


def _ada(cond8, ada_w, ada_b):
    depth, d, n6 = ada_w.shape
    tn = 768
    return pl.pallas_call(
        _ada_kernel,
        grid=(depth, n6 // tn),
        in_specs=[pl.BlockSpec((8, d), lambda l, j: (0, 0)),
                  pl.BlockSpec((1, d, tn), lambda l, j: (l, 0, j)),
                  pl.BlockSpec((1, 1, tn), lambda l, j: (l, 0, j))],
        out_specs=pl.BlockSpec((1, 8, tn), lambda l, j: (l, 0, j)),
        out_shape=jax.ShapeDtypeStruct((depth, 8, n6), F32),
        compiler_params=_cparams(("parallel", "parallel")),
        name="ada_mod",
    )(cond8, ada_w, ada_b.reshape(depth, 1, n6))


def _modulated(x, g, sc, sh):
    y = x * lax.rsqrt(jnp.mean(x * x, axis=-1, keepdims=True) + RMS_EPS)
    return (y * g) * (1.0 + sc) + sh


def _group_of_rows(row0, n_ctx, lat_len):
    return jnp.where(row0 < n_ctx, 0, 1 + (row0 - n_ctx) // lat_len)


def _inproj_kernel(x_ref, g_ref, sc_ref, sh_ref, w_ref, o_ref, h_ref):
    @pl.when(pl.program_id(1) == 0)
    def _():
        sub = NORM_ROWS

        def body(k, carry):
            rows = pl.ds(pl.multiple_of(k * sub, sub), sub)
            h_ref[rows, :] = _modulated(x_ref[rows, :], g_ref[...], sc_ref[0], sh_ref[0]).astype(BF16)
            return carry

        lax.fori_loop(0, x_ref.shape[0] // sub, body, 0)

    o_ref[...] = _dot(h_ref[...], w_ref[...])


def _in_proj(x, g, sc, sh, w, n_ctx, lat_len, tm, tn):
    n, d = x.shape
    pin = w.shape[1]
    grp = lambda i, j: (_group_of_rows(i * tm, n_ctx, lat_len), 0, 0)
    return pl.pallas_call(
        _inproj_kernel,
        grid=(n // tm, pin // tn),
        in_specs=[pl.BlockSpec((tm, d), lambda i, j: (i, 0)),
                  pl.BlockSpec((1, d), lambda i, j: (0, 0)),
                  pl.BlockSpec((1, 1, d), grp),
                  pl.BlockSpec((1, 1, d), grp),
                  pl.BlockSpec((d, tn), lambda i, j: (0, j))],
        out_specs=pl.BlockSpec((tm, tn), lambda i, j: (i, j)),
        out_shape=jax.ShapeDtypeStruct((n, pin), F32),
        scratch_shapes=[pltpu.VMEM((tm, d), BF16)],
        compiler_params=_cparams(("parallel", "arbitrary")),
        name="in_proj",
    )(x, g.reshape(1, d), sc, sh, w)


def _normmod_kernel(x_ref, g_ref, sc_ref, sh_ref, h_ref):
    h_ref[...] = _modulated(x_ref[...], g_ref[...], sc_ref[0], sh_ref[0]).astype(h_ref.dtype)


def _normmod_router_kernel(x_ref, g_ref, sc_ref, sh_ref, rt_ref, h_ref, gate_ref, idx_ref):
    h = _modulated(x_ref[...], g_ref[...], sc_ref[0], sh_ref[0])
    h_ref[...] = h
    logits = _dot(h, rt_ref[...], precision=HI)
    lane = lax.broadcasted_iota(jnp.int32, logits.shape, 1)
    lanef = lane.astype(F32)
    logits = jnp.where(lane < N_EXPERTS, logits, -jnp.inf)
    m1 = jnp.max(logits, axis=-1, keepdims=True)
    i1 = jnp.min(jnp.where(logits == m1, lanef, float(LANE)), axis=-1, keepdims=True)
    rest = jnp.where(lanef == i1, -jnp.inf, logits)
    m2 = jnp.max(rest, axis=-1, keepdims=True)
    i2 = jnp.min(jnp.where(rest == m2, lanef, float(LANE)), axis=-1, keepdims=True)
    e2 = jnp.exp(m2 - m1)
    p1 = 1.0 / (1.0 + e2)
    p2 = e2 / (1.0 + e2)
    gate_ref[...] = jnp.where(lane == 0, p1, jnp.where(lane == 1, p2, 0.0))
    idx_ref[...] = jnp.where(lane == 0, i1, jnp.where(lane == 1, i2, 0.0)).astype(jnp.int32)


def _norm_mod(x, g, sc, sh, n_ctx, lat_len, tm, router=None):
    n, d = x.shape
    grp = lambda i: (_group_of_rows(i * tm, n_ctx, lat_len), 0, 0)
    in_specs = [pl.BlockSpec((tm, d), lambda i: (i, 0)),
                pl.BlockSpec((1, d), lambda i: (0, 0)),
                pl.BlockSpec((1, 1, d), grp),
                pl.BlockSpec((1, 1, d), grp)]
    row_spec = pl.BlockSpec((tm, d), lambda i: (i, 0))
    if router is None:
        return pl.pallas_call(
            _normmod_kernel, grid=(n // tm,), in_specs=in_specs, out_specs=row_spec,
            out_shape=jax.ShapeDtypeStruct((n, d), BF16),
            compiler_params=_cparams(("parallel",)), name="norm_mod",
        )(x, g.reshape(1, d), sc, sh)
    rt = jnp.zeros((d, LANE), F32).at[:, :N_EXPERTS].set(router)
    lane_spec = pl.BlockSpec((tm, LANE), lambda i: (i, 0))
    return pl.pallas_call(
        _normmod_router_kernel, grid=(n // tm,),
        in_specs=in_specs + [pl.BlockSpec((d, LANE), lambda i: (0, 0))],
        out_specs=[row_spec, lane_spec, lane_spec],
        out_shape=[jax.ShapeDtypeStruct((n, d), F32),
                   jax.ShapeDtypeStruct((n, LANE), F32),
                   jax.ShapeDtypeStruct((n, LANE), jnp.int32)],
        compiler_params=_cparams(("parallel",)), name="norm_mod_router",
    )(x, g.reshape(1, d), sc, sh, rt)


def _final_norm_kernel(x_ref, g_ref, o_ref):
    x = x_ref[...]
    o_ref[...] = (x * lax.rsqrt(jnp.mean(x * x, axis=-1, keepdims=True) + RMS_EPS)) * g_ref[...]


def _final_norm(x, g, tm):
    n, d = x.shape
    return pl.pallas_call(
        _final_norm_kernel, grid=(n // tm,),
        in_specs=[pl.BlockSpec((tm, d), lambda i: (i, 0)), pl.BlockSpec((1, d), lambda i: (0, 0))],
        out_specs=pl.BlockSpec((tm, d), lambda i: (i, 0)),
        out_shape=jax.ShapeDtypeStruct((n, d), F32),
        compiler_params=_cparams(("parallel",)), name="final_norm",
    )(x, g.reshape(1, d))


def _softmax_rows(parts):
    m = functools.reduce(jnp.maximum, [jnp.max(s, axis=-1, keepdims=True) for s in parts])
    es = [jnp.exp(s - m) for s in parts]
    inv = 1.0 / functools.reduce(lambda a, b: a + b, [jnp.sum(e, axis=-1, keepdims=True) for e in es])
    return [e * inv for e in es]


def _ctx_attn_kernel(q_ref, k_ref, v_ref, o_ref):
    scale = HEAD_DIM_A ** -0.5
    q, k, v = q_ref[...], k_ref[...], v_ref[...]
    outs = []
    for h in range(LANE // HEAD_DIM_A):
        sl = slice(h * HEAD_DIM_A, (h + 1) * HEAD_DIM_A)
        s = _dot_nt(q[:, sl].astype(BF16), k[:, sl].astype(BF16)) * scale
        (p,) = _softmax_rows([s])
        outs.append(_dot(p.astype(BF16), v[:, sl].astype(BF16)))
    o_ref[...] = jnp.concatenate(outs, axis=-1).astype(o_ref.dtype)


def _ctx_attention(p, n_seq, seq_len):
    nb = WA // LANE
    return pl.pallas_call(
        _ctx_attn_kernel, grid=(n_seq, nb),
        in_specs=[pl.BlockSpec((seq_len, LANE), lambda b, h: (b, OFF_Q // LANE + h)),
                  pl.BlockSpec((seq_len, LANE), lambda b, h: (b, OFF_K // LANE + h)),
                  pl.BlockSpec((seq_len, LANE), lambda b, h: (b, OFF_V // LANE + h))],
        out_specs=pl.BlockSpec((seq_len, LANE), lambda b, h: (b, h)),
        out_shape=jax.ShapeDtypeStruct((n_seq * seq_len, WA), BF16),
        compiler_params=_cparams(("parallel", "parallel")), name="ctx_attention",
    )(p, p, p)


def _na_bias_table(rpb):
    nh, nr, nc = rpb.shape
    w = GRID_W
    rpb = rpb.astype(F32)
    lo = jnp.broadcast_to(rpb[..., :1], (nh, nr, w - NA_COLS))
    hi = jnp.broadcast_to(rpb[..., -1:], (nh, nr, 2 * w - (w - NA_COLS) - nc))
    ext = jnp.concatenate([lo, rpb, hi], axis=-1)
    toep = jnp.tile(ext, (1, 1, w))[..., :w * (2 * w - 1)].reshape(nh, nr, w, 2 * w - 1)[..., w - 1:]
    col = jnp.arange(w)
    cs = jnp.clip(col - NA_COLS // 2, 0, w - NA_COLS)
    valid = (col[None, :] >= cs[:, None]) & (col[None, :] < cs[:, None] + NA_COLS)
    toep = jnp.where(valid, toep, NEG_INF)
    tab = jnp.stack([toep[:, d0:d0 + NA_ROWS] for d0 in range(NA_ROWS)], axis=1)
    return tab.transpose(0, 1, 3, 2, 4).reshape(nh, NA_ROWS, w, NA_ROWS * w)


def _na_kernel(q_ref, k_ref, v_ref, ck_ref, cv_ref, tab_ref, o_ref, *, rows):
    scale = HEAD_DIM_A ** -0.5
    band = NA_ROWS * GRID_W
    ck, cv = ck_ref[0], cv_ref[0]

    def body(r, carry):
        rs = jnp.clip(r - NA_ROWS // 2, 0, rows - NA_ROWS)
        d0 = rs - r + NA_ROWS - 1
        q = q_ref[pl.ds(pl.multiple_of(r * GRID_W, GRID_W), GRID_W), :]
        kb = k_ref[pl.ds(pl.multiple_of(rs * GRID_W, GRID_W), band), :]
        vb = v_ref[pl.ds(pl.multiple_of(rs * GRID_W, GRID_W), band), :]
        outs = []
        for h in range(LANE // HEAD_DIM_A):
            sl = slice(h * HEAD_DIM_A, (h + 1) * HEAD_DIM_A)
            qh = q[:, sl].astype(BF16)
            s_loc = _dot_nt(qh, kb[:, sl].astype(BF16)) * scale + tab_ref[h, d0]
            s_ctx = _dot_nt(qh, ck[:, sl].astype(BF16)) * scale
            p_loc, p_ctx = _softmax_rows([s_loc, s_ctx])
            outs.append(_dot(p_loc.astype(BF16), vb[:, sl].astype(BF16))
                        + _dot(p_ctx.astype(BF16), cv[:, sl].astype(BF16)))
        o_ref[pl.ds(pl.multiple_of(r * GRID_W, GRID_W), GRID_W), :] = (
            jnp.concatenate(outs, axis=-1).astype(o_ref.dtype))
        return carry

    lax.fori_loop(0, rows, body, 0, unroll=2)


def _nbr_attention(p, ck, cv, tab, n_ctx, n_seq, seq_len):
    nb = WA // LANE
    past = ck.shape[1]
    rb0 = n_ctx // seq_len
    hp = LANE // HEAD_DIM_A
    return pl.pallas_call(
        functools.partial(_na_kernel, rows=seq_len // GRID_W), grid=(n_seq, nb),
        in_specs=[pl.BlockSpec((seq_len, LANE), lambda b, h: (rb0 + b, OFF_Q // LANE + h)),
                  pl.BlockSpec((seq_len, LANE), lambda b, h: (rb0 + b, OFF_K // LANE + h)),
                  pl.BlockSpec((seq_len, LANE), lambda b, h: (rb0 + b, OFF_V // LANE + h)),
                  pl.BlockSpec((1, past, LANE), lambda b, h: (b, 0, h)),
                  pl.BlockSpec((1, past, LANE), lambda b, h: (b, 0, h)),
                  pl.BlockSpec((hp, NA_ROWS, GRID_W, NA_ROWS * GRID_W), lambda b, h: (h, 0, 0, 0))],
        out_specs=pl.BlockSpec((seq_len, LANE), lambda b, h: (b, h)),
        out_shape=jax.ShapeDtypeStruct((n_seq * seq_len, WA), BF16),
        compiler_params=_cparams(("parallel", "parallel")), name="nbr_attention",
    )(p, p, p, ck, cv, tab)


def _seq_pos(shape, row0, seq_len):
    return (row0 + lax.broadcasted_iota(jnp.int32, shape, 0)) & (seq_len - 1)


def _shifted(x, t, d, seq_len):
    n = x.shape[0]
    y = pltpu.roll(x, (-d) % n, 0)
    return jnp.where((t + d >= 0) & (t + d < seq_len), y, 0.0)


def _conv_kernel(x_ref, w_ref, o_ref, *, n_ctx, ctx_len, lat_len):
    rb = x_ref.shape[0]
    row0 = pl.program_id(0) * rb
    seq_len = jnp.where(row0 < n_ctx, ctx_len, lat_len)
    x = x_ref[...]
    t = _seq_pos(x.shape, row0, seq_len)
    w = w_ref[...]
    o_ref[...] = (_shifted(x, t, -1, seq_len) * w[0:1] + x * w[1:2] + _shifted(x, t, 1, seq_len) * w[2:3])


def _short_conv(p, conv_w, n_ctx, ctx_len, lat_len):
    n = p.shape[0]
    rb = lat_len
    c = conv_w.shape[1]
    w8 = jnp.zeros((8, c), F32).at[:3].set(conv_w)
    return pl.pallas_call(
        functools.partial(_conv_kernel, n_ctx=n_ctx, ctx_len=ctx_len, lat_len=lat_len),
        grid=(n // rb, c // LANE),
        in_specs=[pl.BlockSpec((rb, LANE), lambda i, j: (i, OFF_RKV // LANE + j)),
                  pl.BlockSpec((8, LANE), lambda i, j: (0, j))],
        out_specs=pl.BlockSpec((rb, LANE), lambda i, j: (i, j)),
        out_shape=jax.ShapeDtypeStruct((n, c), F32),
        compiler_params=_cparams(("parallel", "parallel")), name="short_conv",
    )(p, w8)


def _pool_kernel(u0_ref, u1_ref, u2_ref, u3_ref, w_ref, sc_ref, o_ref, *, n_ctx, ctx_len, lat_len):
    rb = o_ref.shape[0]
    row0 = pl.program_id(0) * rb
    seq_len = jnp.where(row0 < n_ctx, ctx_len, lat_len)
    t = _seq_pos((rb, POOL_GROUP_DIM), row0, seq_len)
    for gi, (win, u_ref) in enumerate(zip(POOL_WINDOWS, (u0_ref, u1_ref, u2_ref, u3_ref))):
        sl = slice(gi * POOL_GROUP_DIM, (gi + 1) * POOL_GROUP_DIM)
        u = u_ref[...]
        acc = u
        for d in range(-(win // 2), win - win // 2):
            if d != 0:
                acc = acc + _shifted(u, t, d, seq_len)
        lo = jnp.maximum(t - win // 2, 0)
        hi = jnp.minimum(t + win - win // 2, seq_len)
        pooled = acc / (hi - lo).astype(F32) - u
        y = _dot(pooled.astype(BF16), w_ref[gi].astype(BF16))
        o_ref[:, sl] = (y * sc_ref[:, sl]).astype(o_ref.dtype)


def _multiscale_pool(p, pool_w, pool_scale, n_ctx, ctx_len, lat_len):
    n = p.shape[0]
    rb = lat_len
    gd = POOL_GROUP_DIM
    group = lambda gi: pl.BlockSpec((rb, gd), lambda i: (i, OFF_POOL // gd + gi))
    return pl.pallas_call(
        functools.partial(_pool_kernel, n_ctx=n_ctx, ctx_len=ctx_len, lat_len=lat_len),
        grid=(n // rb,),
        in_specs=[group(0), group(1), group(2), group(3),
                  pl.BlockSpec(pool_w.shape, lambda i: (0, 0, 0)),
                  pl.BlockSpec((1, WC), lambda i: (0, 0))],
        out_specs=pl.BlockSpec((rb, WC), lambda i: (i, 0)),
        out_shape=jax.ShapeDtypeStruct((n, WC), BF16),
        compiler_params=_cparams(("parallel",)), name="multiscale_pool",
    )(p, p, p, p, pool_w, pool_scale.reshape(1, WC))


def _head_ones():
    r = jnp.arange(LANE) // HEAD_DIM_B
    return (r[:, None] == r[None, :]).astype(F32)


def _rwkv_prep_kernel(r_ref, k_ref, v_ref, xwa_ref, xg0_ref, xg1_ref, wl_ref, g2a_ref, g2b_ref, pt_ref,
                      ones_ref, kk_o, lw_o, b_o, kd_o, g_o, bonus_o):
    r, k, v = r_ref[...], k_ref[...], v_ref[...]
    xwa = xwa_ref[...]
    lane = lax.broadcasted_iota(jnp.int32, xwa.shape, 1)
    act = jnp.where(lane < 2 * DECAY_LORA, jnp.tanh(xwa), xwa)
    lora = _dot(act.astype(BF16), wl_ref[...].astype(BF16))
    pt = pt_ref[...]
    ones = ones_ref[...]
    kkr = k * pt[4:5]
    kk = kkr * lax.rsqrt(_dot(kkr * kkr, ones, precision=HI) + 1e-12)
    kk_o[...] = kk
    for d in range(2):
        z = -(pt[d:d + 1] + lora[:, d * LANE:(d + 1) * LANE])
        softplus = jnp.maximum(z, 0.0) + jnp.log(1.0 + jnp.exp(-jnp.abs(z)))
        lw_o[d] = -jnp.exp(-softplus - 0.5)
        a = _sigmoid(pt[2 + d:3 + d] + lora[:, (2 + d) * LANE:(3 + d) * LANE])
        kd_o[d] = k * (1.0 + (a - 1.0) * pt[5:6])
        b_o[d] = kk * a
    g_o[...] = (_dot(_sigmoid(xg0_ref[...]).astype(BF16), g2a_ref[...].astype(BF16))
                + _dot(_sigmoid(xg1_ref[...]).astype(BF16), g2b_ref[...].astype(BF16)))
    bonus_o[...] = _dot(r * k * pt[6:7], ones, precision=HI) * v


def _rwkv_prep(p, rkv, w0, w2, a0, a2, g2, k_k, k_a, r_k, rb):
    n = p.shape[0]
    nhp = WB // LANE
    wl = jnp.zeros((4, DECAY_LORA, 4, WB), F32)
    wl = wl.at[0, :, 0].set(w2[0]).at[1, :, 1].set(w2[1]).at[2, :, 2].set(a2[0]).at[3, :, 3].set(a2[1])
    wl = wl.reshape(4 * DECAY_LORA, 4, nhp, LANE).transpose(0, 2, 1, 3).reshape(4 * DECAY_LORA, nhp * 4 * LANE)
    pt = jnp.stack([w0[0], w0[1], a0[0], a0[1], k_k, k_a, r_k, jnp.zeros_like(k_k)]).astype(F32)
    nlo = 4 * DECAY_LORA
    col = lambda off: (lambda i, h: (i, off + h))
    tok = jax.ShapeDtypeStruct((n, WB), F32)
    tok2 = jax.ShapeDtypeStruct((2, n, WB), F32)
    spec1 = pl.BlockSpec((rb, LANE), lambda i, h: (i, h))
    spec2 = pl.BlockSpec((2, rb, LANE), lambda i, h: (0, i, h))
    return pl.pallas_call(
        _rwkv_prep_kernel, grid=(n // rb, nhp),
        in_specs=[pl.BlockSpec((rb, LANE), col(0)),
                  pl.BlockSpec((rb, LANE), col(nhp)),
                  pl.BlockSpec((rb, LANE), col(2 * nhp)),
                  pl.BlockSpec((rb, nlo), lambda i, h: (i, OFF_WL // nlo)),
                  pl.BlockSpec((rb, LANE), lambda i, h: (i, OFF_GL // LANE)),
                  pl.BlockSpec((rb, LANE), lambda i, h: (i, OFF_GL // LANE + 1)),
                  pl.BlockSpec((nlo, 4 * LANE), lambda i, h: (0, h)),
                  pl.BlockSpec((LANE, LANE), lambda i, h: (0, h)),
                  pl.BlockSpec((LANE, LANE), lambda i, h: (1, h)),
                  pl.BlockSpec((8, LANE), lambda i, h: (0, h)),
                  pl.BlockSpec((LANE, LANE), lambda i, h: (0, 0))],
        out_specs=[spec1, spec2, spec2, spec2, spec1, spec1],
        out_shape=[tok, tok2, tok2, tok2, tok, tok],
        compiler_params=_cparams(("parallel", "parallel")), name="rwkv_prep",
    )(rkv, rkv, rkv, p, p, p, wl, g2, g2, pt, _head_ones())


def _split_bf16(a):
    hi = pltpu.bitcast(pltpu.bitcast(a, jnp.uint32) & jnp.uint32(0xFFFF0000), F32)
    return hi.astype(BF16), (a - hi).astype(BF16)


_NN = (((1,), (0,)), ((), ()))
_NT = (((1,), (1,)), ((), ()))
_TN = (((0,), (0,)), ((), ()))


def _mm(a, b, passes, dims=_NN):
    if passes == 6:
        return lax.dot_general(a, b, dims, preferred_element_type=F32, precision=HI)
    dg = lambda p, q: lax.dot_general(p, q, dims, preferred_element_type=F32)
    if passes == 1:
        return dg(a.astype(BF16), b.astype(BF16))
    ah, al = _split_bf16(a)
    bh, bl = _split_bf16(b)
    return dg(ah, bh) + (dg(ah, bl) + dg(al, bh))


def _scan_kernel(*refs, n_chunks, has_s0, passes):
    if has_s0:
        r_ref, v_ref, kk_ref, lw_ref, b_ref, kd_ref, s0_ref, o_ref, se_ref, s_ref = refs
    else:
        r_ref, v_ref, kk_ref, lw_ref, b_ref, kd_ref, o_ref, se_ref, s_ref = refs
    p_gram, p_inv, p_state = passes
    c = SCAN_CHUNK
    hd = HEAD_DIM_B
    d = pl.program_id(0)
    ci = pl.program_id(2)

    @pl.when(ci == 0)
    def _():
        s_ref[...] = s0_ref[0, 0] if has_s0 else jnp.zeros(s_ref.shape, F32)

    row = lax.broadcasted_iota(jnp.int32, (c, c), 0)
    col = lax.broadcasted_iota(jnp.int32, (c, c), 1)
    strict = (col - row) * (1 - 2 * d) < 0
    incl = strict | (col == row)
    eye = (col == row).astype(F32)

    lw = lw_ref[0]
    cum = _dot(incl.astype(F32), lw, precision=HI)
    tot = jnp.sum(lw, axis=0, keepdims=True)
    r, v, kk, b, kd = r_ref[...], v_ref[...], kk_ref[...], b_ref[0], kd_ref[0]
    e_neg = jnp.exp(-cum)
    e_end = jnp.exp(tot - cum)
    kk_h = kk * jnp.exp(cum - lw)
    r_h = r * jnp.exp(cum)
    b_t, k_t = b * e_neg, kd * e_neg
    b_e, k_e = b * e_end, kd * e_end
    g_end = jnp.exp(tot)

    heads = range(N_HEADS_B)
    sls = [slice(h * hd, (h + 1) * hd) for h in heads]
    z = [jnp.concatenate([kk_h[:, sl], r_h[:, sl]], axis=0) for sl in sls]
    y = [jnp.concatenate([b_t[:, sl], k_t[:, sl]], axis=0) for sl in sls]
    g = [_mm(z[h], y[h], p_gram, _NT) for h in heads]
    x = [jnp.where(strict, -g[h][:c, :c], 0.0) for h in heads]
    m_k = [jnp.where(strict, g[h][:c, c:], 0.0) for h in heads]
    row2 = lax.broadcasted_iota(jnp.int32, (c, 2 * c), 0)
    col2 = lax.broadcasted_iota(jnp.int32, (c, 2 * c), 1) & (c - 1)
    incl2 = (col2 - row2) * (1 - 2 * d) <= 0
    p_bk = [jnp.where(incl2, g[h][c:], 0.0) for h in heads]
    tinv = [eye + x[h] for h in heads]
    for _ in range(c.bit_length() - 2):
        x = [_mm(x[h], x[h], p_inv) for h in heads]
        tinv = [tinv[h] + _mm(tinv[h], x[h], p_inv) for h in heads]
    s = [s_ref[h] for h in heads]
    vh = [v[:, sl] for sl in sls]
    zs = [_mm(z[h], s[h], p_state, _NT) for h in heads]
    mkv = [_mm(m_k[h], vh[h], p_state) for h in heads]
    u = [-_mm(tinv[h], zs[h][:c] + mkv[h], p_state) for h in heads]
    w = [jnp.concatenate([u[h], vh[h]], axis=0) for h in heads]
    o_ref[0] = jnp.concatenate([zs[h][c:] + _mm(p_bk[h], w[h], p_state) for h in heads], axis=-1)
    for h in heads:
        ye = jnp.concatenate([b_e[:, sls[h]], k_e[:, sls[h]]], axis=0)
        s_ref[h] = s[h] * g_end[:, sls[h]] + _mm(w[h], ye, p_state, _TN)

    @pl.when(ci == n_chunks - 1)
    def _():
        se_ref[0, 0] = s_ref[...]


def _rwkv_scan(rkv, kk, lw, b, kd, s0, row0, n_seq, seq_len, passes=(3, 1, 3)):
    c = SCAN_CHUNK
    n_chunks = seq_len // c
    blk0 = row0 // c
    nb = WB // WB

    def rb(d, s, ci):
        return blk0 + s * n_chunks + ci + d * (n_chunks - 1 - 2 * ci)

    tok = lambda colblk: pl.BlockSpec((c, WB), lambda d, s, ci: (rb(d, s, ci), colblk))
    tok2 = pl.BlockSpec((1, c, WB), lambda d, s, ci: (d, rb(d, s, ci), 0))
    st = pl.BlockSpec((1, 1, N_HEADS_B, HEAD_DIM_B, HEAD_DIM_B), lambda d, s, ci: (d, s, 0, 0, 0))
    in_specs = [tok(0), tok(2 * nb), tok(0), tok2, tok2, tok2]
    args = [rkv, rkv, kk, lw, b, kd]
    if s0 is not None:
        in_specs.append(st)
        args.append(s0)
    return pl.pallas_call(
        functools.partial(_scan_kernel, n_chunks=n_chunks, has_s0=s0 is not None, passes=passes),
        grid=(2, n_seq, n_chunks),
        in_specs=in_specs,
        out_specs=[pl.BlockSpec((1, c, WB), lambda d, s, ci: (d, rb(d, s, ci) - blk0, 0)), st],
        out_shape=[jax.ShapeDtypeStruct((2, n_seq * seq_len, WB), F32),
                   jax.ShapeDtypeStruct((2, n_seq, N_HEADS_B, HEAD_DIM_B, HEAD_DIM_B), F32)],
        scratch_shapes=[pltpu.VMEM((N_HEADS_B, HEAD_DIM_B, HEAD_DIM_B), F32)],
        compiler_params=_cparams(("parallel", "parallel", "arbitrary")), name="rwkv_scan",
    )(*args)


def _rwkv_post_kernel(o_ref, bonus_ref, g_ref, lng_ref, lnb_ref, ones_ref, y_ref):
    o = o_ref[0] + o_ref[1]
    mean_mat = ones_ref[...] * (1.0 / HEAD_DIM_B)
    mu = _dot(o, mean_mat, precision=HI)
    var = _dot(jnp.square(o - mu), mean_mat, precision=HI)
    y = (o - mu) * lax.rsqrt(var + GN_EPS) * lng_ref[...] + lnb_ref[...]
    y_ref[...] = ((y + bonus_ref[...]) * g_ref[...]).astype(y_ref.dtype)


def _rwkv_post(o2, bonus, g, ln_g, ln_b, rb):
    n = bonus.shape[0]
    nhp = WB // LANE
    spec = pl.BlockSpec((rb, LANE), lambda i, h: (i, h))
    vec = pl.BlockSpec((1, LANE), lambda i, h: (0, h))
    return pl.pallas_call(
        _rwkv_post_kernel, grid=(n // rb, nhp),
        in_specs=[pl.BlockSpec((2, rb, LANE), lambda i, h: (0, i, h)), spec, spec, vec, vec,
                  pl.BlockSpec((LANE, LANE), lambda i, h: (0, 0))],
        out_specs=spec,
        out_shape=jax.ShapeDtypeStruct((n, WB), BF16),
        compiler_params=_cparams(("parallel", "parallel")), name="rwkv_post",
    )(o2, bonus, g, ln_g.reshape(1, WB), ln_b.reshape(1, WB), _head_ones())


def _outproj_kernel(a_ref, b_ref, c_ref, w_ref, x_ref, gt_ref, o_ref):
    y = (_dot(a_ref[...], w_ref[0:WA])
         + _dot(b_ref[...], w_ref[WA:WA + WB])
         + _dot(c_ref[...], w_ref[WA + WB:]))
    o_ref[...] = x_ref[...] + gt_ref[0] * y


def _out_proj(oa, ob, oc, w, x, gt, n_ctx, lat_len, tm, tn):
    n, d = x.shape
    grp = lambda i, j: (_group_of_rows(i * tm, n_ctx, lat_len), 0, j)
    return pl.pallas_call(
        _outproj_kernel, grid=(n // tm, d // tn),
        in_specs=[pl.BlockSpec((tm, WA), lambda i, j: (i, 0)),
                  pl.BlockSpec((tm, WB), lambda i, j: (i, 0)),
                  pl.BlockSpec((tm, WC), lambda i, j: (i, 0)),
                  pl.BlockSpec((d, tn), lambda i, j: (0, j)),
                  pl.BlockSpec((tm, tn), lambda i, j: (i, j)),
                  pl.BlockSpec((1, 1, tn), grp)],
        out_specs=pl.BlockSpec((tm, tn), lambda i, j: (i, j)),
        out_shape=jax.ShapeDtypeStruct((n, d), F32),
        compiler_params=_cparams(("parallel", "parallel")), name="out_proj",
    )(oa, ob, oc, w, x, gt)


def _swiglu_kernel(te_ref, nt_ref, h_ref, gate_ref, w1_ref, w3_ref, w2_ref, o_ref, *, n_f):
    i = pl.program_id(0)
    f = pl.program_id(1)

    @pl.when(f == 0)
    def _():
        o_ref[...] = jnp.zeros(o_ref.shape, F32)

    @pl.when(i < nt_ref[0])
    def _():
        h = h_ref[...]
        a = _dot(h, w1_ref[0].astype(BF16))
        b = _dot(h, w3_ref[0].astype(BF16))
        act = (a * _sigmoid(a) * b).astype(BF16)
        o_ref[...] += _dot(act, w2_ref[0].astype(BF16))

        @pl.when(f == n_f - 1)
        def _():
            o_ref[...] *= gate_ref[...]


def _swiglu(h, gate, tile_expert, n_tiles_used, w1, w3, w2, tm, tf):
    n, d = h.shape
    ff = w1.shape[2]
    n_f = ff // tf
    return pl.pallas_call(
        functools.partial(_swiglu_kernel, n_f=n_f),
        grid_spec=pltpu.PrefetchScalarGridSpec(
            num_scalar_prefetch=2, grid=(n // tm, n_f),
            in_specs=[pl.BlockSpec((tm, d), lambda i, f, te, nt: (i, 0)),
                      pl.BlockSpec((tm, 1), lambda i, f, te, nt: (i, 0)),
                      pl.BlockSpec((1, d, tf), lambda i, f, te, nt: (te[i], 0, f)),
                      pl.BlockSpec((1, d, tf), lambda i, f, te, nt: (te[i], 0, f)),
                      pl.BlockSpec((1, tf, d), lambda i, f, te, nt: (te[i], f, 0))],
            out_specs=pl.BlockSpec((tm, d), lambda i, f, te, nt: (i, 0))),
        out_shape=jax.ShapeDtypeStruct((n, d), F32),
        compiler_params=_cparams(("parallel", "arbitrary")), name="swiglu",
    )(tile_expert, n_tiles_used, h, gate, w1, w3, w2)


def _residual_kernel(x_ref, f_ref, gt_ref, o_ref):
    o_ref[...] = x_ref[...] + gt_ref[0] * f_ref[...]


def _residual(x, f, gt, n_ctx, lat_len, tm):
    n, d = x.shape
    row = pl.BlockSpec((tm, d), lambda i: (i, 0))
    return pl.pallas_call(
        _residual_kernel, grid=(n // tm,),
        in_specs=[row, row, pl.BlockSpec((1, 1, d), lambda i: (_group_of_rows(i * tm, n_ctx, lat_len), 0, 0))],
        out_specs=row, out_shape=jax.ShapeDtypeStruct((n, d), F32),
        compiler_params=_cparams(("parallel",)), name="residual",
    )(x, f, gt)


def _gather_rows_kernel(idx_ref, src_ref, o_ref, buf_ref, sem):
    rt = o_ref.shape[0]
    base = pl.program_id(0) * rt

    def copy(r):
        return pltpu.make_async_copy(src_ref.at[pl.ds(idx_ref[base + r], 1)], buf_ref.at[pl.ds(r, 1)], sem)

    def start(r, carry):
        copy(r).start()
        return carry

    def wait(r, carry):
        copy(r).wait()
        return carry

    lax.fori_loop(0, rt, start, 0)
    lax.fori_loop(0, rt, wait, 0)
    o_ref[...] = buf_ref[...].astype(o_ref.dtype)


def _gather_rows(src, idx, rt, out_dtype):
    n_out = idx.shape[0]
    d = src.shape[1]
    return pl.pallas_call(
        _gather_rows_kernel,
        grid_spec=pltpu.PrefetchScalarGridSpec(
            num_scalar_prefetch=1, grid=(n_out // rt,),
            in_specs=[pl.BlockSpec(memory_space=pl.ANY)],
            out_specs=pl.BlockSpec((rt, d), lambda i, idx: (i, 0)),
            scratch_shapes=[pltpu.VMEM((rt, d), src.dtype), pltpu.SemaphoreType.DMA(())]),
        out_shape=jax.ShapeDtypeStruct((n_out, d), out_dtype),
        compiler_params=_cparams(("arbitrary",)), name="gather_rows",
    )(idx, src)


def _combine_kernel(x_ref, y1_ref, y2_ref, gt_ref, o_ref):
    o_ref[...] = x_ref[...] + gt_ref[0] * (y1_ref[...] + y2_ref[...])


def _combine(x, y12, gt, n_ctx, lat_len, tm):
    n, d = x.shape
    nb = n // tm
    row = pl.BlockSpec((tm, d), lambda i: (i, 0))
    return pl.pallas_call(
        _combine_kernel, grid=(nb,),
        in_specs=[row, row, pl.BlockSpec((tm, d), lambda i: (nb + i, 0)),
                  pl.BlockSpec((1, 1, d), lambda i: (_group_of_rows(i * tm, n_ctx, lat_len), 0, 0))],
        out_specs=row, out_shape=jax.ShapeDtypeStruct((n, d), F32),
        compiler_params=_cparams(("parallel",)), name="moe_combine",
    )(x, y12, y12, gt)


def _moe(x, h, gates, idx, w1, w3, w2, gt, n_ctx, lat_len, tm, tf, rt):
    n, d = x.shape
    n_e = w1.shape[0]
    e = jnp.concatenate([idx[:, 0], idx[:, 1]])
    gsel = jnp.concatenate([gates[:, 0], gates[:, 1]])
    onehot = (e[:, None] == jnp.arange(n_e)[None, :]).astype(jnp.int32)
    rank = jnp.take_along_axis(jnp.cumsum(onehot, axis=0), e[:, None], axis=1)[:, 0] - 1
    counts = jnp.sum(onehot, axis=0)
    tiles = (counts + tm - 1) // tm
    tile_end = jnp.cumsum(tiles)
    start = (tile_end - tiles) * tm
    pos = (start[e] + rank).astype(jnp.int32)
    n_rows = 2 * n + n_e * tm
    src = jnp.zeros((n_rows,), jnp.int32).at[pos].set(jnp.tile(jnp.arange(n, dtype=jnp.int32), 2))
    gate_rows = jnp.zeros((n_rows,), F32).at[pos].set(gsel).reshape(n_rows, 1)
    tile_ids = jnp.arange(n_rows // tm)
    tile_expert = jnp.minimum(jnp.sum(tile_ids[:, None] >= tile_end[None, :], axis=1), n_e - 1).astype(jnp.int32)
    n_used = tile_end[-1:].astype(jnp.int32)

    hs = _gather_rows(h, src, rt, BF16)
    ys = _swiglu(hs, gate_rows, tile_expert, n_used, w1, w3, w2, tm, tf)
    y12 = _gather_rows(ys, pos, rt, F32)
    return _combine(x, y12, gt, n_ctx, lat_len, rt)


def _forward(x_prompt, x_sample, cache_attn_k, cache_attn_v, state_rwkv_fwd, state_rwkv_bwd, c, c_ctx,
             norm1_g, norm2_g, ada_w, ada_b, w_in, w_out, na_rpb, rw_conv, rw_w0, rw_w2, rw_a0, rw_a2,
             rw_g2, rw_kk, rw_ka, rw_rk, rw_ln_g, rw_ln_b, pool_w, pool_scale, ffn_w1, ffn_w3, ffn_w2,
             moe_router, moe_w1, moe_w3, moe_w2, final_g, *, tm, tm_proj, tn_in, tn_out, tm_dense, tf_dense,
             tm_moe, tf_moe, rt, rb):
    n_cs, ctx_len, d = x_prompt.shape
    n_ls, lat_len, _ = x_sample.shape
    n_ctx = n_cs * ctx_len
    n_lat = n_ls * lat_len
    n = n_ctx + n_lat
    depth = w_in.shape[0]
    past = cache_attn_k.shape[2]
    assert ctx_len & (ctx_len - 1) == 0 and lat_len & (lat_len - 1) == 0
    assert lat_len % ctx_len == 0 and n_ctx % lat_len == 0 and lat_len % GRID_W == 0

    x = jnp.concatenate([x_prompt.reshape(n_ctx, d), x_sample.reshape(n_lat, d)], axis=0)
    cond8 = jnp.zeros((8, d), F32).at[0].set(c_ctx).at[1:1 + n_ls].set(c)
    mods = _ada(cond8, ada_w, ada_b).reshape(depth, 8, 6, 1, d)

    ks_new, vs_new, sf_new, sb_new = [], [], [], []
    ones_gate = jnp.ones((n, 1), F32)
    dense_tiles = jnp.zeros((n // tm_dense,), jnp.int32)
    dense_used = jnp.full((1,), n // tm_dense, jnp.int32)
    for l in range(depth):
        m = [mods[l, :1 + n_ls, k] for k in range(6)]
        p = _in_proj(x, norm1_g[l], m[1], m[0], w_in[l].astype(BF16), n_ctx, lat_len, tm_proj, tn_in)

        oa_ctx = _ctx_attention(p, n_cs, ctx_len)
        ck = cache_attn_k[:, l].reshape(n_ls, past, WA)
        cv = cache_attn_v[:, l].reshape(n_ls, past, WA)
        oa_lat = _nbr_attention(p, ck, cv, _na_bias_table(na_rpb[l]), n_ctx, n_ls, lat_len)
        oa = jnp.concatenate([oa_ctx, oa_lat], axis=0)

        rkv = _short_conv(p, rw_conv[l], n_ctx, ctx_len, lat_len)
        kk, lw, b, kd, g, bonus = _rwkv_prep(p, rkv, rw_w0[l], rw_w2[l], rw_a0[l], rw_a2[l], rw_g2[l],
                                             rw_kk[l], rw_ka[l], rw_rk[l], rb)
        o_ctx, s_ctx = _rwkv_scan(rkv, kk, lw, b, kd, None, 0, n_cs, ctx_len)
        s0 = jnp.stack([state_rwkv_fwd[:, l], state_rwkv_bwd[:, l]]).astype(F32)
        o_lat, _ = _rwkv_scan(rkv, kk, lw, b, kd, s0, n_ctx, n_ls, lat_len)
        ob = _rwkv_post(jnp.concatenate([o_ctx, o_lat], axis=1), bonus, g, rw_ln_g[l], rw_ln_b[l], rb)

        oc = _multiscale_pool(p, pool_w[l], pool_scale[l], n_ctx, ctx_len, lat_len)
        x = _out_proj(oa, ob, oc, w_out[l].astype(BF16), x, m[2], n_ctx, lat_len, tm_proj, tn_out)

        ks_new.append(p[:n_ctx, OFF_K:OFF_V].reshape(n_cs, ctx_len, N_HEADS_A, HEAD_DIM_A))
        vs_new.append(p[:n_ctx, OFF_V:OFF_RKV].reshape(n_cs, ctx_len, N_HEADS_A, HEAD_DIM_A))
        sf_new.append(s_ctx[0])
        sb_new.append(s_ctx[1])

        i = l // 2
        if l % 2 == 0:
            h = _norm_mod(x, norm2_g[l], m[4], m[3], n_ctx, lat_len, tm)
            f = _swiglu(h, ones_gate, dense_tiles, dense_used, ffn_w1[i:i + 1].astype(BF16),
                        ffn_w3[i:i + 1].astype(BF16), ffn_w2[i:i + 1].astype(BF16), tm_dense, tf_dense)
            x = _residual(x, f, m[5], n_ctx, lat_len, tm)
        else:
            h, gates, idx = _norm_mod(x, norm2_g[l], m[4], m[3], n_ctx, lat_len, tm, router=moe_router[i])
            x = _moe(x, h, gates, idx, moe_w1[i], moe_w3[i], moe_w2[i], m[5], n_ctx, lat_len, tm_moe, tf_moe, rt)

    y = _final_norm(x, final_g, tm)
    return (y[:n_ctx].reshape(n_cs, ctx_len, d), y[n_ctx:].reshape(n_ls, lat_len, d),
            jnp.stack(ks_new, axis=1), jnp.stack(vs_new, axis=1),
            jnp.stack(sf_new, axis=1), jnp.stack(sb_new, axis=1))


def kernel(x_prompt, x_sample, cache_attn_k, cache_attn_v, state_rwkv_fwd, state_rwkv_bwd, c, c_ctx, norm1_g, norm2_g, ada_w, ada_b, w_in, w_out, na_rpb, rw_conv, rw_w0, rw_w2, rw_a0, rw_a2, rw_g2, rw_kk, rw_ka, rw_rk, rw_ln_g, rw_ln_b, pool_w, pool_scale, ffn_w1, ffn_w3, ffn_w2, moe_router, moe_w1, moe_w3, moe_w2, final_g):
    return _forward(x_prompt, x_sample, cache_attn_k, cache_attn_v, state_rwkv_fwd, state_rwkv_bwd, c, c_ctx,
                    norm1_g, norm2_g, ada_w, ada_b, w_in, w_out, na_rpb, rw_conv, rw_w0, rw_w2, rw_a0, rw_a2,
                    rw_g2, rw_kk, rw_ka, rw_rk, rw_ln_g, rw_ln_b, pool_w, pool_scale, ffn_w1, ffn_w3, ffn_w2,
                    moe_router, moe_w1, moe_w3, moe_w2, final_g,
                    tm=512, tm_proj=1024, tn_in=1152, tn_out=1024, tm_dense=1024, tf_dense=512,
                    tm_moe=1024, tf_moe=256, rt=256, rb=512)
```

```python
import functools

import jax
import jax.numpy as jnp
from jax import lax
from jax.experimental import pallas as pl
from jax.experimental.pallas import tpu as pltpu

F32 = jnp.float32
BF16 = jnp.bfloat16
HI = lax.Precision.HIGHEST

D_MODEL = 2048
DEPTH = 2
GRID_W = 64
WA = D_MODEL // 2
WB = D_MODEL // 4
WC = D_MODEL - WA - WB
HEAD_DIM_A = 64
N_HEADS_A = WA // HEAD_DIM_A
HEAD_DIM_B = 64
N_HEADS_B = WB // HEAD_DIM_B
POOL_WINDOWS = (2, 4, 8, 16)
POOL_GROUP_DIM = WC // len(POOL_WINDOWS)
NA_ROWS = 8
NA_COLS = 16
DECAY_LORA = 96
AAA_LORA = 96
GATE_LORA = 256
OFF_Q = 0
OFF_K = WA
OFF_V = 2 * WA
OFF_RKV = 3 * WA
OFF_WL = OFF_RKV + 3 * WB
OFF_AL = OFF_WL + 2 * DECAY_LORA
OFF_GL = OFF_AL + 2 * AAA_LORA
OFF_POOL = OFF_GL + GATE_LORA
P_IN = OFF_POOL + WC
N_EXPERTS = 8
RMS_EPS = 1e-6
GN_EPS = 64e-5
NEG_INF = -1e30

LANE = 128
VMEM_LIMIT = 56 * 1024 * 1024
SCAN_CHUNK = 64
NORM_ROWS = 256


def _cparams(sem):
    return pltpu.CompilerParams(dimension_semantics=sem, vmem_limit_bytes=VMEM_LIMIT)


def _dot(a, b, precision=None):
    return jnp.dot(a, b, preferred_element_type=F32, precision=precision)


def _dot_nt(a, b, precision=None):
    return lax.dot_general(a, b, (((1,), (1,)), ((), ())), preferred_element_type=F32, precision=precision)


def _dot_tn(a, b, precision=None):
    return lax.dot_general(a, b, (((0,), (0,)), ((), ())), preferred_element_type=F32, precision=precision)


def _sigmoid(x):
    return 1.0 / (1.0 + jnp.exp(-x))


def _ada_kernel(c_ref, w_ref, b_ref, o_ref):
    c = c_ref[...]
    s = c * _sigmoid(c)
    o_ref[0] = _dot(s.astype(BF16), w_ref[0].astype(BF16)) + b_ref[0]


def _ada(cond8, ada_w, ada_b):
    depth, d, n6 = ada_w.shape
    tn = 768
    return pl.pallas_call(
        _ada_kernel,
        grid=(depth, n6 // tn),
        in_specs=[pl.BlockSpec((8, d), lambda l, j: (0, 0)),
                  pl.BlockSpec((1, d, tn), lambda l, j: (l, 0, j)),
                  pl.BlockSpec((1, 1, tn), lambda l, j: (l, 0, j))],
        out_specs=pl.BlockSpec((1, 8, tn), lambda l, j: (l, 0, j)),
        out_shape=jax.ShapeDtypeStruct((depth, 8, n6), F32),
        compiler_params=_cparams(("parallel", "parallel")),
        name="ada_mod",
    )(cond8, ada_w, ada_b.reshape(depth, 1, n6))


def _modulated(x, g, sc, sh):
    y = x * lax.rsqrt(jnp.mean(x * x, axis=-1, keepdims=True) + RMS_EPS)
    return (y * g) * (1.0 + sc) + sh


def _group_of_rows(row0, n_ctx, lat_len):
    return jnp.where(row0 < n_ctx, 0, 1 + (row0 - n_ctx) // lat_len)


def _inproj_kernel(x_ref, g_ref, sc_ref, sh_ref, w_ref, o_ref, h_ref):
    @pl.when(pl.program_id(1) == 0)
    def _():
        sub = NORM_ROWS

        def body(k, carry):
            rows = pl.ds(pl.multiple_of(k * sub, sub), sub)
            h_ref[rows, :] = _modulated(x_ref[rows, :], g_ref[...], sc_ref[0], sh_ref[0]).astype(BF16)
            return carry

        lax.fori_loop(0, x_ref.shape[0] // sub, body, 0)

    o_ref[...] = _dot(h_ref[...], w_ref[...])


def _in_proj(x, g, sc, sh, w, n_ctx, lat_len, tm, tn):
    n, d = x.shape
    pin = w.shape[1]
    grp = lambda i, j: (_group_of_rows(i * tm, n_ctx, lat_len), 0, 0)
    return pl.pallas_call(
        _inproj_kernel,
        grid=(n // tm, pin // tn),
        in_specs=[pl.BlockSpec((tm, d), lambda i, j: (i, 0)),
                  pl.BlockSpec((1, d), lambda i, j: (0, 0)),
                  pl.BlockSpec((1, 1, d), grp),
                  pl.BlockSpec((1, 1, d), grp),
                  pl.BlockSpec((d, tn), lambda i, j: (0, j))],
        out_specs=pl.BlockSpec((tm, tn), lambda i, j: (i, j)),
        out_shape=jax.ShapeDtypeStruct((n, pin), F32),
        scratch_shapes=[pltpu.VMEM((tm, d), BF16)],
        compiler_params=_cparams(("parallel", "arbitrary")),
        name="in_proj",
    )(x, g.reshape(1, d), sc, sh, w)


def _normmod_kernel(x_ref, g_ref, sc_ref, sh_ref, h_ref):
    h_ref[...] = _modulated(x_ref[...], g_ref[...], sc_ref[0], sh_ref[0]).astype(h_ref.dtype)


def _normmod_router_kernel(x_ref, g_ref, sc_ref, sh_ref, rt_ref, h_ref, gate_ref, idx_ref):
    h = _modulated(x_ref[...], g_ref[...], sc_ref[0], sh_ref[0])
    h_ref[...] = h
    logits = _dot(h, rt_ref[...], precision=HI)
    lane = lax.broadcasted_iota(jnp.int32, logits.shape, 1)
    lanef = lane.astype(F32)
    logits = jnp.where(lane < N_EXPERTS, logits, -jnp.inf)
    m1 = jnp.max(logits, axis=-1, keepdims=True)
    i1 = jnp.min(jnp.where(logits == m1, lanef, float(LANE)), axis=-1, keepdims=True)
    rest = jnp.where(lanef == i1, -jnp.inf, logits)
    m2 = jnp.max(rest, axis=-1, keepdims=True)
    i2 = jnp.min(jnp.where(rest == m2, lanef, float(LANE)), axis=-1, keepdims=True)
    e2 = jnp.exp(m2 - m1)
    p1 = 1.0 / (1.0 + e2)
    p2 = e2 / (1.0 + e2)
    gate_ref[...] = jnp.where(lane == 0, p1, jnp.where(lane == 1, p2, 0.0))
    idx_ref[...] = jnp.where(lane == 0, i1, jnp.where(lane == 1, i2, 0.0)).astype(jnp.int32)


def _norm_mod(x, g, sc, sh, n_ctx, lat_len, tm, router=None):
    n, d = x.shape
    grp = lambda i: (_group_of_rows(i * tm, n_ctx, lat_len), 0, 0)
    in_specs = [pl.BlockSpec((tm, d), lambda i: (i, 0)),
                pl.BlockSpec((1, d), lambda i: (0, 0)),
                pl.BlockSpec((1, 1, d), grp),
                pl.BlockSpec((1, 1, d), grp)]
    row_spec = pl.BlockSpec((tm, d), lambda i: (i, 0))
    if router is None:
        return pl.pallas_call(
            _normmod_kernel, grid=(n // tm,), in_specs=in_specs, out_specs=row_spec,
            out_shape=jax.ShapeDtypeStruct((n, d), BF16),
            compiler_params=_cparams(("parallel",)), name="norm_mod",
        )(x, g.reshape(1, d), sc, sh)
    rt = jnp.zeros((d, LANE), F32).at[:, :N_EXPERTS].set(router)
    lane_spec = pl.BlockSpec((tm, LANE), lambda i: (i, 0))
    return pl.pallas_call(
        _normmod_router_kernel, grid=(n // tm,),
        in_specs=in_specs + [pl.BlockSpec((d, LANE), lambda i: (0, 0))],
        out_specs=[row_spec, lane_spec, lane_spec],
        out_shape=[jax.ShapeDtypeStruct((n, d), F32),
                   jax.ShapeDtypeStruct((n, LANE), F32),
                   jax.ShapeDtypeStruct((n, LANE), jnp.int32)],
        compiler_params=_cparams(("parallel",)), name="norm_mod_router",
    )(x, g.reshape(1, d), sc, sh, rt)


def _final_norm_kernel(x_ref, g_ref, o_ref):
    x = x_ref[...]
    o_ref[...] = (x * lax.rsqrt(jnp.mean(x * x, axis=-1, keepdims=True) + RMS_EPS)) * g_ref[...]


def _final_norm(x, g, tm):
    n, d = x.shape
    return pl.pallas_call(
        _final_norm_kernel, grid=(n // tm,),
        in_specs=[pl.BlockSpec((tm, d), lambda i: (i, 0)), pl.BlockSpec((1, d), lambda i: (0, 0))],
        out_specs=pl.BlockSpec((tm, d), lambda i: (i, 0)),
        out_shape=jax.ShapeDtypeStruct((n, d), F32),
        compiler_params=_cparams(("parallel",)), name="final_norm",
    )(x, g.reshape(1, d))


def _softmax_rows(parts):
    m = functools.reduce(jnp.maximum, [jnp.max(s, axis=-1, keepdims=True) for s in parts])
    es = [jnp.exp(s - m) for s in parts]
    inv = 1.0 / functools.reduce(lambda a, b: a + b, [jnp.sum(e, axis=-1, keepdims=True) for e in es])
    return [e * inv for e in es]


def _ctx_attn_kernel(q_ref, k_ref, v_ref, oa_ref, o_ref):
    del oa_ref
    scale = HEAD_DIM_A ** -0.5
    q, k, v = q_ref[...], k_ref[...], v_ref[...]
    outs = []
    for h in range(LANE // HEAD_DIM_A):
        sl = slice(h * HEAD_DIM_A, (h + 1) * HEAD_DIM_A)
        s = _dot_nt(q[:, sl].astype(BF16), k[:, sl].astype(BF16)) * scale
        (p,) = _softmax_rows([s])
        outs.append(_dot(p.astype(BF16), v[:, sl].astype(BF16)))
    o_ref[...] = jnp.concatenate(outs, axis=-1).astype(o_ref.dtype)


def _ctx_attention(p, oa, n_seq, seq_len):
    nb = WA // LANE
    return pl.pallas_call(
        _ctx_attn_kernel, grid=(n_seq, nb),
        in_specs=[pl.BlockSpec((seq_len, LANE), lambda b, h: (b, OFF_Q // LANE + h)),
                  pl.BlockSpec((seq_len, LANE), lambda b, h: (b, OFF_K // LANE + h)),
                  pl.BlockSpec((seq_len, LANE), lambda b, h: (b, OFF_V // LANE + h)),
                  pl.BlockSpec(memory_space=pl.ANY)],
        out_specs=pl.BlockSpec((seq_len, LANE), lambda b, h: (b, h)),
        out_shape=jax.ShapeDtypeStruct(oa.shape, oa.dtype),
        input_output_aliases={3: 0},
        compiler_params=_cparams(("parallel", "parallel")), name="ctx_attention",
    )(p, p, p, oa)


def _na_bias_table(rpb):
    nh, nr, nc = rpb.shape
    w = GRID_W
    rpb = rpb.astype(F32)
    lo = jnp.broadcast_to(rpb[..., :1], (nh, nr, w - NA_COLS))
    hi = jnp.broadcast_to(rpb[..., -1:], (nh, nr, 2 * w - (w - NA_COLS) - nc))
    ext = jnp.concatenate([lo, rpb, hi], axis=-1)
    toep = jnp.tile(ext, (1, 1, w))[..., :w * (2 * w - 1)].reshape(nh, nr, w, 2 * w - 1)[..., w - 1:]
    col = jnp.arange(w)
    cs = jnp.clip(col - NA_COLS // 2, 0, w - NA_COLS)
    valid = (col[None, :] >= cs[:, None]) & (col[None, :] < cs[:, None] + NA_COLS)
    toep = jnp.where(valid, toep, NEG_INF)
    tab = jnp.stack([toep[:, d0:d0 + NA_ROWS] for d0 in range(NA_ROWS)], axis=1)
    return tab.transpose(0, 1, 3, 2, 4).reshape(nh, NA_ROWS, w, NA_ROWS * w)


NA_ROWS_PER_STEP = 4
NA_CTX_CHUNK = 512


def _na_kernel(q_ref, k_ref, v_ref, ck_ref, cv_ref, tab_ref, o_ref, stat_ref, octx_ref, *, rows, n_fill):
    @pl.when(pl.program_id(0) < n_fill)
    def _():
        o_ref[...] = jnp.zeros(o_ref.shape, o_ref.dtype)

    @pl.when(pl.program_id(0) >= n_fill)
    def _():
        _na_body(q_ref, k_ref, v_ref, ck_ref, cv_ref, tab_ref, o_ref, stat_ref, octx_ref, rows)


def _na_body(q_ref, k_ref, v_ref, ck_ref, cv_ref, tab_ref, o_ref, stat_ref, octx_ref, rows):
    scale = HEAD_DIM_A ** -0.5
    band = NA_ROWS * GRID_W
    nh = LANE // HEAD_DIM_A
    hsl = [slice(h * HEAD_DIM_A, (h + 1) * HEAD_DIM_A) for h in range(nh)]
    ck = [ck_ref[0][:, sl].astype(BF16) for sl in hsl]
    cv = [cv_ref[0][:, sl].astype(BF16) for sl in hsl]

    def ctx_body(c, carry):
        rsl = pl.ds(pl.multiple_of(c * NA_CTX_CHUNK, NA_CTX_CHUNK), NA_CTX_CHUNK)
        q = q_ref[rsl, :]
        lane = lax.broadcasted_iota(jnp.int32, (NA_CTX_CHUNK, LANE), 1)
        stat = jnp.zeros((NA_CTX_CHUNK, LANE), F32)
        outs = []
        for h in range(nh):
            s = _dot_nt(q[:, hsl[h]].astype(BF16), ck[h]) * scale
            m = jnp.max(s, axis=-1, keepdims=True)
            e = jnp.exp(s - m)
            l = jnp.sum(e, axis=-1, keepdims=True)
            stat = jnp.where(lane == 2 * h, m, jnp.where(lane == 2 * h + 1, l, stat))
            outs.append(_dot(e.astype(BF16), cv[h]))
        stat_ref[rsl, :] = stat
        octx_ref[rsl, :] = jnp.concatenate(outs, axis=-1)
        return carry

    lax.fori_loop(0, q_ref.shape[0] // NA_CTX_CHUNK, ctx_body, 0)

    nr = NA_ROWS_PER_STEP

    def body(it, carry):
        chains = []
        for j in range(nr):
            r = it * nr + j
            rs = jnp.clip(r - NA_ROWS // 2, 0, rows - NA_ROWS)
            d0 = rs - r + NA_ROWS - 1
            qsl = pl.ds(pl.multiple_of(r * GRID_W, GRID_W), GRID_W)
            bsl = pl.ds(pl.multiple_of(rs * GRID_W, GRID_W), band)
            q, kb, vb, st, oc = q_ref[qsl, :], k_ref[bsl, :], v_ref[bsl, :], stat_ref[qsl, :], octx_ref[qsl, :]
            for h in range(nh):
                chains.append(dict(q=q[:, hsl[h]].astype(BF16), k=kb[:, hsl[h]].astype(BF16),
                                   v=vb[:, hsl[h]].astype(BF16), tab=tab_ref[h, d0],
                                   m_c=st[:, 2 * h:2 * h + 1], l_c=st[:, 2 * h + 1:2 * h + 2], o_c=oc[:, hsl[h]]))
        s = [_dot_nt(c["q"], c["k"]) * scale + c["tab"] for c in chains]
        m = [jnp.maximum(jnp.max(si, axis=-1, keepdims=True), c["m_c"]) for si, c in zip(s, chains)]
        e = [jnp.exp(si - mi) for si, mi in zip(s, m)]
        a_c = [jnp.exp(c["m_c"] - mi) for c, mi in zip(chains, m)]
        den = [jnp.sum(ei, axis=-1, keepdims=True) + ai * c["l_c"] for ei, ai, c in zip(e, a_c, chains)]
        o = [(_dot(ei.astype(BF16), c["v"]) + ai * c["o_c"]) / di for ei, ai, di, c in zip(e, a_c, den, chains)]
        for j in range(nr):
            r = it * nr + j
            o_ref[pl.ds(pl.multiple_of(r * GRID_W, GRID_W), GRID_W), :] = (
                jnp.concatenate(o[j * nh:(j + 1) * nh], axis=-1).astype(o_ref.dtype))
        return carry

    lax.fori_loop(0, rows // nr, body, 0)


def _nbr_attention(p, ck, cv, tab, n_ctx, n_seq, seq_len):
    nb = WA // LANE
    past = ck.shape[1]
    nf = n_ctx // seq_len
    hp = LANE // HEAD_DIM_A
    tok = lambda off: pl.BlockSpec((seq_len, LANE), lambda b, h: (jnp.maximum(b, nf), off // LANE + h))
    cache = pl.BlockSpec((1, past, LANE), lambda b, h: (jnp.maximum(b - nf, 0), 0, h))
    return pl.pallas_call(
        functools.partial(_na_kernel, rows=seq_len // GRID_W, n_fill=nf), grid=(nf + n_seq, nb),
        in_specs=[tok(OFF_Q), tok(OFF_K), tok(OFF_V), cache, cache,
                  pl.BlockSpec((hp, NA_ROWS, GRID_W, NA_ROWS * GRID_W), lambda b, h: (h, 0, 0, 0))],
        out_specs=pl.BlockSpec((seq_len, LANE), lambda b, h: (b, h)),
        out_shape=jax.ShapeDtypeStruct((p.shape[0], WA), BF16),
        scratch_shapes=[pltpu.VMEM((seq_len, LANE), F32), pltpu.VMEM((seq_len, LANE), F32)],
        compiler_params=_cparams(("parallel", "parallel")), name="nbr_attention",
    )(p, p, p, ck, cv, tab)


def _seq_pos(shape, row0, seq_len):
    return (row0 + lax.broadcasted_iota(jnp.int32, shape, 0)) & (seq_len - 1)


def _shifted(x, t, d, seq_len):
    n = x.shape[0]
    y = pltpu.roll(x, (-d) % n, 0)
    return jnp.where((t + d >= 0) & (t + d < seq_len), y, 0.0)


def _conv_kernel(x_ref, w_ref, o_ref, *, n_ctx, ctx_len, lat_len):
    rb = x_ref.shape[0]
    row0 = pl.program_id(0) * rb
    seq_len = jnp.where(row0 < n_ctx, ctx_len, lat_len)
    x = x_ref[...]
    t = _seq_pos(x.shape, row0, seq_len)
    w = w_ref[...]
    o_ref[...] = (_shifted(x, t, -1, seq_len) * w[0:1] + x * w[1:2] + _shifted(x, t, 1, seq_len) * w[2:3])


def _short_conv(p, conv_w, n_ctx, ctx_len, lat_len):
    n = p.shape[0]
    rb = lat_len
    c = conv_w.shape[1]
    w8 = jnp.zeros((8, c), F32).at[:3].set(conv_w)
    return pl.pallas_call(
        functools.partial(_conv_kernel, n_ctx=n_ctx, ctx_len=ctx_len, lat_len=lat_len),
        grid=(n // rb, c // LANE),
        in_specs=[pl.BlockSpec((rb, LANE), lambda i, j: (i, OFF_RKV // LANE + j)),
                  pl.BlockSpec((8, LANE), lambda i, j: (0, j))],
        out_specs=pl.BlockSpec((rb, LANE), lambda i, j: (i, j)),
        out_shape=jax.ShapeDtypeStruct((n, c), F32),
        compiler_params=_cparams(("parallel", "parallel")), name="short_conv",
    )(p, w8)


def _pool_kernel(u0_ref, u1_ref, u2_ref, u3_ref, w_ref, sc_ref, o_ref, *, n_ctx, ctx_len, lat_len):
    rb = o_ref.shape[0]
    row0 = pl.program_id(0) * rb
    seq_len = jnp.where(row0 < n_ctx, ctx_len, lat_len)
    t = _seq_pos((rb, POOL_GROUP_DIM), row0, seq_len)
    for gi, (win, u_ref) in enumerate(zip(POOL_WINDOWS, (u0_ref, u1_ref, u2_ref, u3_ref))):
        sl = slice(gi * POOL_GROUP_DIM, (gi + 1) * POOL_GROUP_DIM)
        u = u_ref[...]
        acc = u
        for d in range(-(win // 2), win - win // 2):
            if d != 0:
                acc = acc + _shifted(u, t, d, seq_len)
        lo = jnp.maximum(t - win // 2, 0)
        hi = jnp.minimum(t + win - win // 2, seq_len)
        pooled = acc / (hi - lo).astype(F32) - u
        y = _dot(pooled.astype(BF16), w_ref[gi].astype(BF16))
        o_ref[:, sl] = (y * sc_ref[:, sl]).astype(o_ref.dtype)


def _multiscale_pool(p, pool_w, pool_scale, n_ctx, ctx_len, lat_len):
    n = p.shape[0]
    rb = lat_len
    gd = POOL_GROUP_DIM
    group = lambda gi: pl.BlockSpec((rb, gd), lambda i: (i, OFF_POOL // gd + gi))
    return pl.pallas_call(
        functools.partial(_pool_kernel, n_ctx=n_ctx, ctx_len=ctx_len, lat_len=lat_len),
        grid=(n // rb,),
        in_specs=[group(0), group(1), group(2), group(3),
                  pl.BlockSpec(pool_w.shape, lambda i: (0, 0, 0)),
                  pl.BlockSpec((1, WC), lambda i: (0, 0))],
        out_specs=pl.BlockSpec((rb, WC), lambda i: (i, 0)),
        out_shape=jax.ShapeDtypeStruct((n, WC), BF16),
        compiler_params=_cparams(("parallel",)), name="multiscale_pool",
    )(p, p, p, p, pool_w, pool_scale.reshape(1, WC))


def _head_ones():
    r = jnp.arange(LANE) // HEAD_DIM_B
    return (r[:, None] == r[None, :]).astype(F32)


def _rwkv_prep_kernel(r_ref, k_ref, v_ref, xwa_ref, xg0_ref, xg1_ref, wl_ref, g2a_ref, g2b_ref, pt_ref,
                      ones_ref, kk_o, lw_o, b_o, kd_o, g_o, bonus_o):
    r, k, v = r_ref[...], k_ref[...], v_ref[...]
    xwa = xwa_ref[...]
    lane = lax.broadcasted_iota(jnp.int32, xwa.shape, 1)
    act = jnp.where(lane < 2 * DECAY_LORA, jnp.tanh(xwa), xwa)
    lora = _dot(act.astype(BF16), wl_ref[...].astype(BF16))
    pt = pt_ref[...]
    ones = ones_ref[...]
    kkr = k * pt[4:5]
    kk = kkr * lax.rsqrt(_dot(kkr * kkr, ones, precision=HI) + 1e-12)
    kk_o[...] = kk
    for d in range(2):
        z = -(pt[d:d + 1] + lora[:, d * LANE:(d + 1) * LANE])
        softplus = jnp.maximum(z, 0.0) + jnp.log(1.0 + jnp.exp(-jnp.abs(z)))
        lw_o[d] = -jnp.exp(-softplus - 0.5)
        a = _sigmoid(pt[2 + d:3 + d] + lora[:, (2 + d) * LANE:(3 + d) * LANE])
        kd_o[d] = k * (1.0 + (a - 1.0) * pt[5:6])
        b_o[d] = kk * a
    g_o[...] = (_dot(_sigmoid(xg0_ref[...]).astype(BF16), g2a_ref[...].astype(BF16))
                + _dot(_sigmoid(xg1_ref[...]).astype(BF16), g2b_ref[...].astype(BF16)))
    bonus_o[...] = _dot(r * k * pt[6:7], ones, precision=HI) * v


def _rwkv_prep(p, rkv, w0, w2, a0, a2, g2, k_k, k_a, r_k, rb):
    n = p.shape[0]
    nhp = WB // LANE
    wl = jnp.zeros((4, DECAY_LORA, 4, WB), F32)
    wl = wl.at[0, :, 0].set(w2[0]).at[1, :, 1].set(w2[1]).at[2, :, 2].set(a2[0]).at[3, :, 3].set(a2[1])
    wl = wl.reshape(4 * DECAY_LORA, 4, nhp, LANE).transpose(0, 2, 1, 3).reshape(4 * DECAY_LORA, nhp * 4 * LANE)
    pt = jnp.stack([w0[0], w0[1], a0[0], a0[1], k_k, k_a, r_k, jnp.zeros_like(k_k)]).astype(F32)
    nlo = 4 * DECAY_LORA
    col = lambda off: (lambda i, h: (i, off + h))
    tok = jax.ShapeDtypeStruct((n, WB), F32)
    tok2 = jax.ShapeDtypeStruct((2, n, WB), F32)
    spec1 = pl.BlockSpec((rb, LANE), lambda i, h: (i, h))
    spec2 = pl.BlockSpec((2, rb, LANE), lambda i, h: (0, i, h))
    return pl.pallas_call(
        _rwkv_prep_kernel, grid=(n // rb, nhp),
        in_specs=[pl.BlockSpec((rb, LANE), col(0)),
                  pl.BlockSpec((rb, LANE), col(nhp)),
                  pl.BlockSpec((rb, LANE), col(2 * nhp)),
                  pl.BlockSpec((rb, nlo), lambda i, h: (i, OFF_WL // nlo)),
                  pl.BlockSpec((rb, LANE), lambda i, h: (i, OFF_GL // LANE)),
                  pl.BlockSpec((rb, LANE), lambda i, h: (i, OFF_GL // LANE + 1)),
                  pl.BlockSpec((nlo, 4 * LANE), lambda i, h: (0, h)),
                  pl.BlockSpec((LANE, LANE), lambda i, h: (0, h)),
                  pl.BlockSpec((LANE, LANE), lambda i, h: (1, h)),
                  pl.BlockSpec((8, LANE), lambda i, h: (0, h)),
                  pl.BlockSpec((LANE, LANE), lambda i, h: (0, 0))],
        out_specs=[spec1, spec2, spec2, spec2, spec1, spec1],
        out_shape=[tok, tok2, tok2, tok2, tok, tok],
        compiler_params=_cparams(("parallel", "parallel")), name="rwkv_prep",
    )(rkv, rkv, rkv, p, p, p, wl, g2, g2, pt, _head_ones())


def _split_bf16(a):
    hi = pltpu.bitcast(pltpu.bitcast(a, jnp.uint32) & jnp.uint32(0xFFFF0000), F32)
    return hi.astype(BF16), (a - hi).astype(BF16)


_NN = (((1,), (0,)), ((), ()))
_NT = (((1,), (1,)), ((), ()))
_TN = (((0,), (0,)), ((), ()))


def _mm(a, b, passes, dims=_NN):
    if passes == 6:
        return lax.dot_general(a, b, dims, preferred_element_type=F32, precision=HI)
    dg = lambda p, q: lax.dot_general(p, q, dims, preferred_element_type=F32)
    if passes == 1:
        return dg(a.astype(BF16), b.astype(BF16))
    ah, al = _split_bf16(a)
    bh, bl = _split_bf16(b)
    return dg(ah, bh) + (dg(ah, bl) + dg(al, bh))


def _scan_kernel(r_ref, v_ref, kk_ref, lw_ref, b_ref, kd_ref, s0_ref, o_ref, se_ref, s_ref, *,
                 ctx_steps, ctx_chunks, lat_chunks, passes):
    p_gram, p_inv, p_state = passes
    c = SCAN_CHUNK
    hd = HEAD_DIM_B
    d = pl.program_id(0)
    j = pl.program_id(1)
    is_lat = j >= ctx_steps
    ci = jnp.where(is_lat, (j - ctx_steps) & (lat_chunks - 1), j & (ctx_chunks - 1))
    last = jnp.where(is_lat, lat_chunks, ctx_chunks) - 1

    @pl.when((ci == 0) & jnp.logical_not(is_lat))
    def _():
        s_ref[...] = jnp.zeros(s_ref.shape, F32)

    @pl.when((ci == 0) & is_lat)
    def _():
        s_ref[...] = s0_ref[0, 0]

    row = lax.broadcasted_iota(jnp.int32, (c, c), 0)
    col = lax.broadcasted_iota(jnp.int32, (c, c), 1)
    strict = (col - row) * (1 - 2 * d) < 0
    incl = strict | (col == row)
    eye = (col == row).astype(F32)

    lw = lw_ref[0]
    cum = _dot(incl.astype(F32), lw, precision=HI)
    tot = jnp.sum(lw, axis=0, keepdims=True)
    r, v, kk, b, kd = r_ref[...], v_ref[...], kk_ref[...], b_ref[0], kd_ref[0]
    e_neg = jnp.exp(-cum)
    e_end = jnp.exp(tot - cum)
    kk_h = kk * jnp.exp(cum - lw)
    r_h = r * jnp.exp(cum)
    b_t, k_t = b * e_neg, kd * e_neg
    b_e, k_e = b * e_end, kd * e_end
    g_end = jnp.exp(tot)

    heads = range(N_HEADS_B)
    sls = [slice(h * hd, (h + 1) * hd) for h in heads]
    z = [jnp.concatenate([kk_h[:, sl], r_h[:, sl]], axis=0) for sl in sls]
    y = [jnp.concatenate([b_t[:, sl], k_t[:, sl]], axis=0) for sl in sls]
    g = [_mm(z[h], y[h], p_gram, _NT) for h in heads]
    x = [jnp.where(strict, -g[h][:c, :c], 0.0) for h in heads]
    m_k = [jnp.where(strict, g[h][:c, c:], 0.0) for h in heads]
    row2 = lax.broadcasted_iota(jnp.int32, (c, 2 * c), 0)
    col2 = lax.broadcasted_iota(jnp.int32, (c, 2 * c), 1) & (c - 1)
    incl2 = (col2 - row2) * (1 - 2 * d) <= 0
    p_bk = [jnp.where(incl2, g[h][c:], 0.0) for h in heads]
    tinv = [eye + x[h] for h in heads]
    for _ in range(c.bit_length() - 2):
        x = [_mm(x[h], x[h], p_inv) for h in heads]
        tinv = [tinv[h] + _mm(tinv[h], x[h], p_inv) for h in heads]
    s = [s_ref[h] for h in heads]
    vh = [v[:, sl] for sl in sls]
    zs = [_mm(z[h], s[h], p_state, _NT) for h in heads]
    mkv = [_mm(m_k[h], vh[h], p_state) for h in heads]
    u = [-_mm(tinv[h], zs[h][:c] + mkv[h], p_state) for h in heads]
    w = [jnp.concatenate([u[h], vh[h]], axis=0) for h in heads]
    o_ref[0] = jnp.concatenate([zs[h][c:] + _mm(p_bk[h], w[h], p_state) for h in heads], axis=-1)
    for h in heads:
        ye = jnp.concatenate([b_e[:, sls[h]], k_e[:, sls[h]]], axis=0)
        s_ref[h] = s[h] * g_end[:, sls[h]] + _mm(w[h], ye, p_state, _TN)

    @pl.when(ci == last)
    def _():
        se_ref[0, 0] = s_ref[...]


def _rwkv_scan(rkv, kk, lw, b, kd, s0, n_cs, ctx_len, n_ls, lat_len, passes=(3, 1, 3)):
    c = SCAN_CHUNK
    cc, lc = ctx_len // c, lat_len // c
    assert cc & (cc - 1) == 0 and lc & (lc - 1) == 0
    ctx_steps = n_cs * cc
    n_steps = ctx_steps + n_ls * lc
    nb = WB // WB
    n = kk.shape[0]

    def where(j):
        jl = jnp.maximum(j - ctx_steps, 0)
        is_lat = j >= ctx_steps
        seq = jnp.where(is_lat, n_cs + jl // lc, j // cc)
        start = jnp.where(is_lat, ctx_steps + (jl // lc) * lc, (j // cc) * cc)
        ch = jnp.where(is_lat, jl % lc, j % cc)
        return seq, start, ch, jnp.where(is_lat, lc, cc)

    def rb(d, j):
        _, start, ch, nch = where(j)
        return start + ch + d * (nch - 1 - 2 * ch)

    tok = lambda colblk: pl.BlockSpec((c, WB), lambda d, j: (rb(d, j), colblk))
    tok2 = pl.BlockSpec((1, c, WB), lambda d, j: (d, rb(d, j), 0))
    st_shape = (1, 1, N_HEADS_B, HEAD_DIM_B, HEAD_DIM_B)
    return pl.pallas_call(
        functools.partial(_scan_kernel, ctx_steps=ctx_steps, ctx_chunks=cc, lat_chunks=lc, passes=passes),
        grid=(2, n_steps),
        in_specs=[tok(0), tok(2 * nb), tok(0), tok2, tok2, tok2,
                  pl.BlockSpec(st_shape, lambda d, j: (d, jnp.maximum(where(j)[0] - n_cs, 0), 0, 0, 0))],
        out_specs=[tok2, pl.BlockSpec(st_shape, lambda d, j: (d, where(j)[0], 0, 0, 0))],
        out_shape=[jax.ShapeDtypeStruct((2, n, WB), F32),
                   jax.ShapeDtypeStruct((2, n_cs + n_ls, N_HEADS_B, HEAD_DIM_B, HEAD_DIM_B), F32)],
        scratch_shapes=[pltpu.VMEM((N_HEADS_B, HEAD_DIM_B, HEAD_DIM_B), F32)],
        compiler_params=_cparams(("parallel", "arbitrary")), name="rwkv_scan",
    )(rkv, rkv, kk, lw, b, kd, s0)


def _rwkv_post_kernel(o_ref, bonus_ref, g_ref, lng_ref, lnb_ref, ones_ref, y_ref):
    o = o_ref[0] + o_ref[1]
    mean_mat = ones_ref[...] * (1.0 / HEAD_DIM_B)
    mu = _dot(o, mean_mat, precision=HI)
    var = _dot(jnp.square(o - mu), mean_mat, precision=HI)
    y = (o - mu) * lax.rsqrt(var + GN_EPS) * lng_ref[...] + lnb_ref[...]
    y_ref[...] = ((y + bonus_ref[...]) * g_ref[...]).astype(y_ref.dtype)


def _rwkv_post(o2, bonus, g, ln_g, ln_b, rb):
    n = bonus.shape[0]
    nhp = WB // LANE
    spec = pl.BlockSpec((rb, LANE), lambda i, h: (i, h))
    vec = pl.BlockSpec((1, LANE), lambda i, h: (0, h))
    return pl.pallas_call(
        _rwkv_post_kernel, grid=(n // rb, nhp),
        in_specs=[pl.BlockSpec((2, rb, LANE), lambda i, h: (0, i, h)), spec, spec, vec, vec,
                  pl.BlockSpec((LANE, LANE), lambda i, h: (0, 0))],
        out_specs=spec,
        out_shape=jax.ShapeDtypeStruct((n, WB), BF16),
        compiler_params=_cparams(("parallel", "parallel")), name="rwkv_post",
    )(o2, bonus, g, ln_g.reshape(1, WB), ln_b.reshape(1, WB), _head_ones())


def _outproj_kernel(a_ref, b_ref, c_ref, w_ref, x_ref, gt_ref, o_ref):
    y = (_dot(a_ref[...], w_ref[0:WA])
         + _dot(b_ref[...], w_ref[WA:WA + WB])
         + _dot(c_ref[...], w_ref[WA + WB:]))
    o_ref[...] = x_ref[...] + gt_ref[0] * y


def _out_proj(oa, ob, oc, w, x, gt, n_ctx, lat_len, tm, tn):
    n, d = x.shape
    grp = lambda i, j: (_group_of_rows(i * tm, n_ctx, lat_len), 0, j)
    return pl.pallas_call(
        _outproj_kernel, grid=(n // tm, d // tn),
        in_specs=[pl.BlockSpec((tm, WA), lambda i, j: (i, 0)),
                  pl.BlockSpec((tm, WB), lambda i, j: (i, 0)),
                  pl.BlockSpec((tm, WC), lambda i, j: (i, 0)),
                  pl.BlockSpec((d, tn), lambda i, j: (0, j)),
                  pl.BlockSpec((tm, tn), lambda i, j: (i, j)),
                  pl.BlockSpec((1, 1, tn), grp)],
        out_specs=pl.BlockSpec((tm, tn), lambda i, j: (i, j)),
        out_shape=jax.ShapeDtypeStruct((n, d), F32),
        compiler_params=_cparams(("parallel", "parallel")), name="out_proj",
    )(oa, ob, oc, w, x, gt)


def _swiglu_kernel(te_ref, nt_ref, h_ref, w1_ref, w3_ref, w2_ref, o_ref):
    i = pl.program_id(0)
    f = pl.program_id(1)

    @pl.when(f == 0)
    def _():
        o_ref[...] = jnp.zeros(o_ref.shape, F32)

    @pl.when(i < nt_ref[0])
    def _():
        h = h_ref[...]
        a = _dot(h, w1_ref[0].astype(BF16))
        b = _dot(h, w3_ref[0].astype(BF16))
        act = (a * _sigmoid(a) * b).astype(BF16)
        o_ref[...] += _dot(act, w2_ref[0].astype(BF16))


def _swiglu(h, tile_expert, n_tiles_used, w1, w3, w2, tm, tf, out_buffers=2):
    n, d = h.shape
    ff = w1.shape[2]
    return pl.pallas_call(
        _swiglu_kernel,
        grid_spec=pltpu.PrefetchScalarGridSpec(
            num_scalar_prefetch=2, grid=(n // tm, ff // tf),
            in_specs=[pl.BlockSpec((tm, d), lambda i, f, te, nt: (i, 0)),
                      pl.BlockSpec((1, d, tf), lambda i, f, te, nt: (te[i], 0, f)),
                      pl.BlockSpec((1, d, tf), lambda i, f, te, nt: (te[i], 0, f)),
                      pl.BlockSpec((1, tf, d), lambda i, f, te, nt: (te[i], f, 0))],
            out_specs=pl.BlockSpec((tm, d), lambda i, f, te, nt: (i, 0),
                                   pipeline_mode=pl.Buffered(out_buffers))),
        out_shape=jax.ShapeDtypeStruct((n, d), F32),
        compiler_params=_cparams(("parallel", "arbitrary")), name="swiglu",
    )(tile_expert, n_tiles_used, h, w1, w3, w2)


def _residual_kernel(x_ref, f_ref, gt_ref, o_ref):
    o_ref[...] = x_ref[...] + gt_ref[0] * f_ref[...]


def _residual(x, f, gt, n_ctx, lat_len, tm):
    n, d = x.shape
    row = pl.BlockSpec((tm, d), lambda i: (i, 0))
    return pl.pallas_call(
        _residual_kernel, grid=(n // tm,),
        in_specs=[row, row, pl.BlockSpec((1, 1, d), lambda i: (_group_of_rows(i * tm, n_ctx, lat_len), 0, 0))],
        out_specs=row, out_shape=jax.ShapeDtypeStruct((n, d), F32),
        compiler_params=_cparams(("parallel",)), name="residual",
    )(x, f, gt)


GATHER_UNROLL = 8


def _gather_into(idx_ref, base, src_ref, buf_ref, sem):
    rt = buf_ref.shape[0]

    def copy(r):
        return pltpu.make_async_copy(src_ref.at[pl.ds(idx_ref[base + r], 1)], buf_ref.at[pl.ds(r, 1)], sem)

    def start(r, carry):
        copy(r).start()
        return carry

    def wait(r, carry):
        copy(r).wait()
        return carry

    lax.fori_loop(0, rt, start, 0, unroll=GATHER_UNROLL)
    lax.fori_loop(0, rt, wait, 0, unroll=GATHER_UNROLL)


def _gather_rows_kernel(idx_ref, src_ref, o_ref, buf_ref, sem):
    _gather_into(idx_ref, pl.program_id(0) * o_ref.shape[0], src_ref, buf_ref, sem)
    o_ref[...] = buf_ref[...].astype(o_ref.dtype)


def _gather_rows(src, idx, rt, out_dtype):
    n_out = idx.shape[0]
    d = src.shape[1]
    return pl.pallas_call(
        _gather_rows_kernel,
        grid_spec=pltpu.PrefetchScalarGridSpec(
            num_scalar_prefetch=1, grid=(n_out // rt,),
            in_specs=[pl.BlockSpec(memory_space=pl.ANY)],
            out_specs=pl.BlockSpec((rt, d), lambda i, idx: (i, 0)),
            scratch_shapes=[pltpu.VMEM((rt, d), src.dtype), pltpu.SemaphoreType.DMA(())]),
        out_shape=jax.ShapeDtypeStruct((n_out, d), out_dtype),
        compiler_params=_cparams(("arbitrary",)), name="gather_rows",
    )(idx, src)


def _combine_kernel(pos_ref, x_ref, gates_ref, gt_ref, ys_ref, o_ref, y1_ref, y2_ref, sem1, sem2, *, n):
    tm = o_ref.shape[0]
    base = pl.program_id(0) * tm
    _gather_into(pos_ref, base, ys_ref, y1_ref, sem1)
    _gather_into(pos_ref, n + base, ys_ref, y2_ref, sem2)
    g = gates_ref[...]
    o_ref[...] = x_ref[...] + gt_ref[0] * (g[:, 0:1] * y1_ref[...] + g[:, 1:2] * y2_ref[...])


def _combine(x, ys, pos, gates, gt, n_ctx, lat_len, tm):
    n, d = x.shape
    row = pl.BlockSpec((tm, d), lambda i, pos: (i, 0))
    return pl.pallas_call(
        functools.partial(_combine_kernel, n=n),
        grid_spec=pltpu.PrefetchScalarGridSpec(
            num_scalar_prefetch=1, grid=(n // tm,),
            in_specs=[row, pl.BlockSpec((tm, LANE), lambda i, pos: (i, 0)),
                      pl.BlockSpec((1, 1, d), lambda i, pos: (_group_of_rows(i * tm, n_ctx, lat_len), 0, 0)),
                      pl.BlockSpec(memory_space=pl.ANY)],
            out_specs=row,
            scratch_shapes=[pltpu.VMEM((tm, d), F32), pltpu.VMEM((tm, d), F32),
                            pltpu.SemaphoreType.DMA(()), pltpu.SemaphoreType.DMA(())]),
        out_shape=jax.ShapeDtypeStruct((n, d), F32),
        compiler_params=_cparams(("arbitrary",)), name="moe_combine",
    )(pos, x, gates, gt, ys)


def _moe(x, h, gates, idx, w1, w3, w2, gt, n_ctx, lat_len, tm, tf, rt):
    n, d = x.shape
    n_e = w1.shape[0]
    e = jnp.concatenate([idx[:, 0], idx[:, 1]])
    onehot = (e[:, None] == jnp.arange(n_e)[None, :]).astype(jnp.int32)
    rank = jnp.take_along_axis(jnp.cumsum(onehot, axis=0), e[:, None], axis=1)[:, 0] - 1
    counts = jnp.sum(onehot, axis=0)
    tiles = (counts + tm - 1) // tm
    tile_end = jnp.cumsum(tiles)
    start = (tile_end - tiles) * tm
    pos = (start[e] + rank).astype(jnp.int32)
    n_rows = 2 * n + n_e * tm
    src = jnp.zeros((n_rows,), jnp.int32).at[pos].set(jnp.tile(jnp.arange(n, dtype=jnp.int32), 2))
    tile_ids = jnp.arange(n_rows // tm)
    tile_expert = jnp.minimum(jnp.sum(tile_ids[:, None] >= tile_end[None, :], axis=1), n_e - 1).astype(jnp.int32)
    n_used = tile_end[-1:].astype(jnp.int32)

    hs = _gather_rows(h, src, rt, BF16)
    ys = _swiglu(hs, tile_expert, n_used, w1, w3, w2, tm, tf, out_buffers=1)
    return _combine(x, ys, pos, gates, gt, n_ctx, lat_len, rt)


def _forward(x_prompt, x_sample, cache_attn_k, cache_attn_v, state_rwkv_fwd, state_rwkv_bwd, c, c_ctx,
             norm1_g, norm2_g, ada_w, ada_b, w_in, w_out, na_rpb, rw_conv, rw_w0, rw_w2, rw_a0, rw_a2,
             rw_g2, rw_kk, rw_ka, rw_rk, rw_ln_g, rw_ln_b, pool_w, pool_scale, ffn_w1, ffn_w3, ffn_w2,
             moe_router, moe_w1, moe_w3, moe_w2, final_g, *, tm, tm_proj, tn_in, tn_out, tm_dense, tf_dense,
             tm_moe, tf_moe, rt, rb):
    n_cs, ctx_len, d = x_prompt.shape
    n_ls, lat_len, _ = x_sample.shape
    n_ctx = n_cs * ctx_len
    n_lat = n_ls * lat_len
    n = n_ctx + n_lat
    depth = w_in.shape[0]
    past = cache_attn_k.shape[2]
    assert ctx_len & (ctx_len - 1) == 0 and lat_len & (lat_len - 1) == 0
    assert lat_len % ctx_len == 0 and n_ctx % lat_len == 0 and lat_len % GRID_W == 0

    x = jnp.concatenate([x_prompt.reshape(n_ctx, d), x_sample.reshape(n_lat, d)], axis=0)
    cond8 = jnp.zeros((8, d), F32).at[0].set(c_ctx).at[1:1 + n_ls].set(c)
    mods = _ada(cond8, ada_w, ada_b).reshape(depth, 8, 6, 1, d)

    ks_new, vs_new, sf_new, sb_new = [], [], [], []
    dense_tiles = jnp.zeros((n // tm_dense,), jnp.int32)
    dense_used = jnp.full((1,), n // tm_dense, jnp.int32)
    for l in range(depth):
        m = [mods[l, :1 + n_ls, k] for k in range(6)]
        p = _in_proj(x, norm1_g[l], m[1], m[0], w_in[l].astype(BF16), n_ctx, lat_len, tm_proj, tn_in)

        ck = cache_attn_k[:, l].reshape(n_ls, past, WA)
        cv = cache_attn_v[:, l].reshape(n_ls, past, WA)
        oa = _nbr_attention(p, ck, cv, _na_bias_table(na_rpb[l]), n_ctx, n_ls, lat_len)
        oa = _ctx_attention(p, oa, n_cs, ctx_len)

        rkv = _short_conv(p, rw_conv[l], n_ctx, ctx_len, lat_len)
        kk, lw, b, kd, g, bonus = _rwkv_prep(p, rkv, rw_w0[l], rw_w2[l], rw_a0[l], rw_a2[l], rw_g2[l],
                                             rw_kk[l], rw_ka[l], rw_rk[l], rb)
        s0 = jnp.stack([state_rwkv_fwd[:, l], state_rwkv_bwd[:, l]]).astype(F32)
        o2, s_end = _rwkv_scan(rkv, kk, lw, b, kd, s0, n_cs, ctx_len, n_ls, lat_len)
        ob = _rwkv_post(o2, bonus, g, rw_ln_g[l], rw_ln_b[l], rb)

        oc = _multiscale_pool(p, pool_w[l], pool_scale[l], n_ctx, ctx_len, lat_len)
        x = _out_proj(oa, ob, oc, w_out[l].astype(BF16), x, m[2], n_ctx, lat_len, tm_proj, tn_out)

        ks_new.append(p[:n_ctx, OFF_K:OFF_V].reshape(n_cs, ctx_len, N_HEADS_A, HEAD_DIM_A))
        vs_new.append(p[:n_ctx, OFF_V:OFF_RKV].reshape(n_cs, ctx_len, N_HEADS_A, HEAD_DIM_A))
        sf_new.append(s_end[0, :n_cs])
        sb_new.append(s_end[1, :n_cs])

        i = l // 2
        if l % 2 == 0:
            h = _norm_mod(x, norm2_g[l], m[4], m[3], n_ctx, lat_len, tm)
            f = _swiglu(h, dense_tiles, dense_used, ffn_w1[i:i + 1].astype(BF16),
                        ffn_w3[i:i + 1].astype(BF16), ffn_w2[i:i + 1].astype(BF16), tm_dense, tf_dense)
            x = _residual(x, f, m[5], n_ctx, lat_len, tm)
        else:
            h, gates, idx = _norm_mod(x, norm2_g[l], m[4], m[3], n_ctx, lat_len, tm, router=moe_router[i])
            x = _moe(x, h, gates, idx, moe_w1[i], moe_w3[i], moe_w2[i], m[5], n_ctx, lat_len, tm_moe, tf_moe, rt)

    y = _final_norm(x, final_g, tm)
    return (y[:n_ctx].reshape(n_cs, ctx_len, d), y[n_ctx:].reshape(n_ls, lat_len, d),
            jnp.stack(ks_new, axis=1), jnp.stack(vs_new, axis=1),
            jnp.stack(sf_new, axis=1), jnp.stack(sb_new, axis=1))


def kernel(x_prompt, x_sample, cache_attn_k, cache_attn_v, state_rwkv_fwd, state_rwkv_bwd, c, c_ctx, norm1_g, norm2_g, ada_w, ada_b, w_in, w_out, na_rpb, rw_conv, rw_w0, rw_w2, rw_a0, rw_a2, rw_g2, rw_kk, rw_ka, rw_rk, rw_ln_g, rw_ln_b, pool_w, pool_scale, ffn_w1, ffn_w3, ffn_w2, moe_router, moe_w1, moe_w3, moe_w2, final_g):
    return _forward(x_prompt, x_sample, cache_attn_k, cache_attn_v, state_rwkv_fwd, state_rwkv_bwd, c, c_ctx,
                    norm1_g, norm2_g, ada_w, ada_b, w_in, w_out, na_rpb, rw_conv, rw_w0, rw_w2, rw_a0, rw_a2,
                    rw_g2, rw_kk, rw_ka, rw_rk, rw_ln_g, rw_ln_b, pool_w, pool_scale, ffn_w1, ffn_w3, ffn_w2,
                    moe_router, moe_w1, moe_w3, moe_w2, final_g,
                    tm=512, tm_proj=1024, tn_in=1152, tn_out=1024, tm_dense=1024, tf_dense=512,
                    tm_moe=1024, tf_moe=512, rt=256, rb=512)
```

```python
import functools

import jax
import jax.numpy as jnp
from jax import lax
from jax.experimental import pallas as pl
from jax.experimental.pallas import tpu as pltpu

F32 = jnp.float32
BF16 = jnp.bfloat16
HI = lax.Precision.HIGHEST

D_MODEL = 2048
DEPTH = 2
GRID_W = 64
WA = D_MODEL // 2
WB = D_MODEL // 4
WC = D_MODEL - WA - WB
HEAD_DIM_A = 64
N_HEADS_A = WA // HEAD_DIM_A
HEAD_DIM_B = 64
N_HEADS_B = WB // HEAD_DIM_B
POOL_WINDOWS = (2, 4, 8, 16)
POOL_GROUP_DIM = WC // len(POOL_WINDOWS)
NA_ROWS = 8
NA_COLS = 16
DECAY_LORA = 96
AAA_LORA = 96
GATE_LORA = 256
OFF_Q = 0
OFF_K = WA
OFF_V = 2 * WA
OFF_RKV = 3 * WA
OFF_WL = OFF_RKV + 3 * WB
OFF_AL = OFF_WL + 2 * DECAY_LORA
OFF_GL = OFF_AL + 2 * AAA_LORA
OFF_POOL = OFF_GL + GATE_LORA
P_IN = OFF_POOL + WC
N_EXPERTS = 8
RMS_EPS = 1e-6
GN_EPS = 64e-5
NEG_INF = -1e30

LANE = 128
VMEM_LIMIT = 56 * 1024 * 1024
SCAN_CHUNK = 64
NORM_ROWS = 256


def _cparams(sem):
    return pltpu.CompilerParams(dimension_semantics=sem, vmem_limit_bytes=VMEM_LIMIT)


def _dot(a, b, precision=None):
    return jnp.dot(a, b, preferred_element_type=F32, precision=precision)


def _dot_nt(a, b, precision=None):
    return lax.dot_general(a, b, (((1,), (1,)), ((), ())), preferred_element_type=F32, precision=precision)


def _dot_tn(a, b, precision=None):
    return lax.dot_general(a, b, (((0,), (0,)), ((), ())), preferred_element_type=F32, precision=precision)


def _sigmoid(x):
    return 1.0 / (1.0 + jnp.exp(-x))


def _ada_kernel(c_ref, w_ref, b_ref, o_ref):
    c = c_ref[...]
    s = c * _sigmoid(c)
    o_ref[0] = _dot(s.astype(BF16), w_ref[0].astype(BF16)) + b_ref[0]


def _ada(cond8, ada_w, ada_b):
    depth, d, n6 = ada_w.shape
    tn = 768
    return pl.pallas_call(
        _ada_kernel,
        grid=(depth, n6 // tn),
        in_specs=[pl.BlockSpec((8, d), lambda l, j: (0, 0)),
                  pl.BlockSpec((1, d, tn), lambda l, j: (l, 0, j)),
                  pl.BlockSpec((1, 1, tn), lambda l, j: (l, 0, j))],
        out_specs=pl.BlockSpec((1, 8, tn), lambda l, j: (l, 0, j)),
        out_shape=jax.ShapeDtypeStruct((depth, 8, n6), F32),
        compiler_params=_cparams(("parallel", "parallel")),
        name="ada_mod",
    )(cond8, ada_w, ada_b.reshape(depth, 1, n6))


def _modulated(x, g, sc, sh):
    y = x * lax.rsqrt(jnp.mean(x * x, axis=-1, keepdims=True) + RMS_EPS)
    return (y * g) * (1.0 + sc) + sh


def _group_of_rows(row0, n_ctx, lat_len):
    return jnp.where(row0 < n_ctx, 0, 1 + (row0 - n_ctx) // lat_len)


def _inproj_kernel(x_ref, g_ref, sc_ref, sh_ref, w_ref, o_ref, h_ref):
    @pl.when(pl.program_id(1) == 0)
    def _():
        sub = NORM_ROWS

        def body(k, carry):
            rows = pl.ds(pl.multiple_of(k * sub, sub), sub)
            h_ref[rows, :] = _modulated(x_ref[rows, :], g_ref[...], sc_ref[0], sh_ref[0]).astype(BF16)
            return carry

        lax.fori_loop(0, x_ref.shape[0] // sub, body, 0)

    o_ref[...] = _dot(h_ref[...], w_ref[...])


def _in_proj(x, g, sc, sh, w, n_ctx, lat_len, tm, tn):
    n, d = x.shape
    pin = w.shape[1]
    grp = lambda i, j: (_group_of_rows(i * tm, n_ctx, lat_len), 0, 0)
    return pl.pallas_call(
        _inproj_kernel,
        grid=(n // tm, pin // tn),
        in_specs=[pl.BlockSpec((tm, d), lambda i, j: (i, 0)),
                  pl.BlockSpec((1, d), lambda i, j: (0, 0)),
                  pl.BlockSpec((1, 1, d), grp),
                  pl.BlockSpec((1, 1, d), grp),
                  pl.BlockSpec((d, tn), lambda i, j: (0, j))],
        out_specs=pl.BlockSpec((tm, tn), lambda i, j: (i, j)),
        out_shape=jax.ShapeDtypeStruct((n, pin), F32),
        scratch_shapes=[pltpu.VMEM((tm, d), BF16)],
        compiler_params=_cparams(("parallel", "arbitrary")),
        name="in_proj",
    )(x, g.reshape(1, d), sc, sh, w)


def _normmod_kernel(x_ref, g_ref, sc_ref, sh_ref, h_ref):
    h_ref[...] = _modulated(x_ref[...], g_ref[...], sc_ref[0], sh_ref[0]).astype(h_ref.dtype)


def _normmod_router_kernel(x_ref, g_ref, sc_ref, sh_ref, rt_ref, h_ref, gate_ref, idx_ref):
    h = _modulated(x_ref[...], g_ref[...], sc_ref[0], sh_ref[0])
    h_ref[...] = h
    logits = _dot(h, rt_ref[...], precision=HI)
    lane = lax.broadcasted_iota(jnp.int32, logits.shape, 1)
    lanef = lane.astype(F32)
    logits = jnp.where(lane < N_EXPERTS, logits, -jnp.inf)
    m1 = jnp.max(logits, axis=-1, keepdims=True)
    i1 = jnp.min(jnp.where(logits == m1, lanef, float(LANE)), axis=-1, keepdims=True)
    rest = jnp.where(lanef == i1, -jnp.inf, logits)
    m2 = jnp.max(rest, axis=-1, keepdims=True)
    i2 = jnp.min(jnp.where(rest == m2, lanef, float(LANE)), axis=-1, keepdims=True)
    e2 = jnp.exp(m2 - m1)
    p1 = 1.0 / (1.0 + e2)
    p2 = e2 / (1.0 + e2)
    gate_ref[...] = jnp.where(lane == 0, p1, jnp.where(lane == 1, p2, 0.0))
    idx_ref[...] = jnp.where(lane == 0, i1, jnp.where(lane == 1, i2, 0.0)).astype(jnp.int32)


def _norm_mod(x, g, sc, sh, n_ctx, lat_len, tm, router=None):
    n, d = x.shape
    grp = lambda i: (_group_of_rows(i * tm, n_ctx, lat_len), 0, 0)
    in_specs = [pl.BlockSpec((tm, d), lambda i: (i, 0)),
                pl.BlockSpec((1, d), lambda i: (0, 0)),
                pl.BlockSpec((1, 1, d), grp),
                pl.BlockSpec((1, 1, d), grp)]
    row_spec = pl.BlockSpec((tm, d), lambda i: (i, 0))
    if router is None:
        return pl.pallas_call(
            _normmod_kernel, grid=(n // tm,), in_specs=in_specs, out_specs=row_spec,
            out_shape=jax.ShapeDtypeStruct((n, d), BF16),
            compiler_params=_cparams(("parallel",)), name="norm_mod",
        )(x, g.reshape(1, d), sc, sh)
    rt = jnp.zeros((d, LANE), F32).at[:, :N_EXPERTS].set(router)
    lane_spec = pl.BlockSpec((tm, LANE), lambda i: (i, 0))
    return pl.pallas_call(
        _normmod_router_kernel, grid=(n // tm,),
        in_specs=in_specs + [pl.BlockSpec((d, LANE), lambda i: (0, 0))],
        out_specs=[row_spec, lane_spec, lane_spec],
        out_shape=[jax.ShapeDtypeStruct((n, d), F32),
                   jax.ShapeDtypeStruct((n, LANE), F32),
                   jax.ShapeDtypeStruct((n, LANE), jnp.int32)],
        compiler_params=_cparams(("parallel",)), name="norm_mod_router",
    )(x, g.reshape(1, d), sc, sh, rt)


def _final_norm_kernel(x_ref, g_ref, o_ref):
    x = x_ref[...]
    o_ref[...] = (x * lax.rsqrt(jnp.mean(x * x, axis=-1, keepdims=True) + RMS_EPS)) * g_ref[...]


def _final_norm(x, g, tm):
    n, d = x.shape
    return pl.pallas_call(
        _final_norm_kernel, grid=(n // tm,),
        in_specs=[pl.BlockSpec((tm, d), lambda i: (i, 0)), pl.BlockSpec((1, d), lambda i: (0, 0))],
        out_specs=pl.BlockSpec((tm, d), lambda i: (i, 0)),
        out_shape=jax.ShapeDtypeStruct((n, d), F32),
        compiler_params=_cparams(("parallel",)), name="final_norm",
    )(x, g.reshape(1, d))


CTX_ATTN_LANES = 512


def _head_masks(shape):
    lane = lax.broadcasted_iota(jnp.int32, shape, 1)
    return [lane // HEAD_DIM_A == h for h in range(LANE // HEAD_DIM_A)]


def _ctx_attn_kernel(q_ref, k_ref, v_ref, oa_ref, o_ref):
    del oa_ref
    scale = HEAD_DIM_A ** -0.5
    nh = LANE // HEAD_DIM_A
    hm = _head_masks((q_ref.shape[0], LANE))
    chains = []
    for j in range(q_ref.shape[1] // LANE):
        cols = slice(j * LANE, (j + 1) * LANE)
        q, k, v = q_ref[:, cols], k_ref[:, cols].astype(BF16), v_ref[:, cols].astype(BF16)
        chains += [(jnp.where(hm[h], q, 0.0).astype(BF16), k, v) for h in range(nh)]
    s = [_dot_nt(q, k) * scale for q, k, _ in chains]
    e = [jnp.exp(si - jnp.max(si, axis=-1, keepdims=True)) for si in s]
    inv = [1.0 / jnp.sum(ei, axis=-1, keepdims=True) for ei in e]
    pv = [_dot((ei * ii).astype(BF16), v) for ei, ii, (_, _, v) in zip(e, inv, chains)]
    for j in range(q_ref.shape[1] // LANE):
        out = jnp.where(hm[0], pv[j * nh], pv[j * nh + 1])
        o_ref[:, j * LANE:(j + 1) * LANE] = out.astype(o_ref.dtype)


def _ctx_attention(p, oa, n_seq, seq_len):
    w = CTX_ATTN_LANES
    return pl.pallas_call(
        _ctx_attn_kernel, grid=(n_seq, WA // w),
        in_specs=[pl.BlockSpec((seq_len, w), lambda b, h: (b, OFF_Q // w + h)),
                  pl.BlockSpec((seq_len, w), lambda b, h: (b, OFF_K // w + h)),
                  pl.BlockSpec((seq_len, w), lambda b, h: (b, OFF_V // w + h)),
                  pl.BlockSpec(memory_space=pl.ANY)],
        out_specs=pl.BlockSpec((seq_len, w), lambda b, h: (b, h)),
        out_shape=jax.ShapeDtypeStruct(oa.shape, oa.dtype),
        input_output_aliases={3: 0},
        compiler_params=_cparams(("parallel", "parallel")), name="ctx_attention",
    )(p, p, p, oa)


def _na_bias_table(rpb):
    nh, nr, nc = rpb.shape
    w = GRID_W
    rpb = rpb.astype(F32)
    lo = jnp.broadcast_to(rpb[..., :1], (nh, nr, w - NA_COLS))
    hi = jnp.broadcast_to(rpb[..., -1:], (nh, nr, 2 * w - (w - NA_COLS) - nc))
    ext = jnp.concatenate([lo, rpb, hi], axis=-1)
    toep = jnp.tile(ext, (1, 1, w))[..., :w * (2 * w - 1)].reshape(nh, nr, w, 2 * w - 1)[..., w - 1:]
    col = jnp.arange(w)
    cs = jnp.clip(col - NA_COLS // 2, 0, w - NA_COLS)
    valid = (col[None, :] >= cs[:, None]) & (col[None, :] < cs[:, None] + NA_COLS)
    toep = jnp.where(valid, toep, NEG_INF)
    tab = jnp.stack([toep[:, d0:d0 + NA_ROWS] for d0 in range(NA_ROWS)], axis=1)
    return tab.transpose(0, 1, 3, 2, 4).reshape(nh, NA_ROWS, w, NA_ROWS * w)


NA_ROWS_PER_STEP = 4
NA_CTX_CHUNK = 512


def _na_kernel(q_ref, k_ref, v_ref, ck_ref, cv_ref, tab_ref, o_ref, stat_ref, octx_ref, *, rows, n_fill):
    @pl.when(pl.program_id(0) < n_fill)
    def _():
        o_ref[...] = jnp.zeros(o_ref.shape, o_ref.dtype)

    @pl.when(pl.program_id(0) >= n_fill)
    def _():
        _na_body(q_ref, k_ref, v_ref, ck_ref, cv_ref, tab_ref, o_ref, stat_ref, octx_ref, rows)


def _na_body(q_ref, k_ref, v_ref, ck_ref, cv_ref, tab_ref, o_ref, stat_ref, octx_ref, rows):
    scale = HEAD_DIM_A ** -0.5
    band = NA_ROWS * GRID_W
    nh = LANE // HEAD_DIM_A
    ck = ck_ref[0].astype(BF16)
    cv = cv_ref[0].astype(BF16)

    def ctx_body(c, carry):
        rsl = pl.ds(pl.multiple_of(c * NA_CTX_CHUNK, NA_CTX_CHUNK), NA_CTX_CHUNK)
        q = q_ref[rsl, :]
        hm = _head_masks(q.shape)
        lane = lax.broadcasted_iota(jnp.int32, q.shape, 1)
        stat = jnp.zeros(q.shape, F32)
        out = jnp.zeros(q.shape, F32)
        for h in range(nh):
            s = _dot_nt(jnp.where(hm[h], q, 0.0).astype(BF16), ck) * scale
            m = jnp.max(s, axis=-1, keepdims=True)
            e = jnp.exp(s - m)
            l = jnp.sum(e, axis=-1, keepdims=True)
            stat = jnp.where(lane == 2 * h, m, jnp.where(lane == 2 * h + 1, l, stat))
            out = jnp.where(hm[h], _dot(e.astype(BF16), cv), out)
        stat_ref[rsl, :] = stat
        octx_ref[rsl, :] = out
        return carry

    lax.fori_loop(0, q_ref.shape[0] // NA_CTX_CHUNK, ctx_body, 0)

    nr = NA_ROWS_PER_STEP

    def body(it, carry):
        hm = _head_masks((GRID_W, LANE))
        rows_in = []
        chains = []
        for j in range(nr):
            r = it * nr + j
            rs = jnp.clip(r - NA_ROWS // 2, 0, rows - NA_ROWS)
            d0 = rs - r + NA_ROWS - 1
            qsl = pl.ds(pl.multiple_of(r * GRID_W, GRID_W), GRID_W)
            bsl = pl.ds(pl.multiple_of(rs * GRID_W, GRID_W), band)
            q, st = q_ref[qsl, :], stat_ref[qsl, :]
            kb, vb = k_ref[bsl, :].astype(BF16), v_ref[bsl, :].astype(BF16)
            rows_in.append((qsl, octx_ref[qsl, :]))
            for h in range(nh):
                chains.append(dict(q=jnp.where(hm[h], q, 0.0).astype(BF16), k=kb, v=vb, tab=tab_ref[h, d0],
                                   m_c=st[:, 2 * h:2 * h + 1], l_c=st[:, 2 * h + 1:2 * h + 2]))
        s = [_dot_nt(c["q"], c["k"]) * scale + c["tab"] for c in chains]
        m = [jnp.maximum(jnp.max(si, axis=-1, keepdims=True), c["m_c"]) for si, c in zip(s, chains)]
        e = [jnp.exp(si - mi) for si, mi in zip(s, m)]
        a_c = [jnp.exp(c["m_c"] - mi) for c, mi in zip(chains, m)]
        inv = [1.0 / (jnp.sum(ei, axis=-1, keepdims=True) + ai * c["l_c"]) for ei, ai, c in zip(e, a_c, chains)]
        pv = [_dot(ei.astype(BF16), c["v"]) for ei, c in zip(e, chains)]
        for j, (qsl, oc) in enumerate(rows_in):
            out = jnp.zeros((GRID_W, LANE), F32)
            for h in range(nh):
                i = j * nh + h
                out = jnp.where(hm[h], (pv[i] + a_c[i] * oc) * inv[i], out)
            o_ref[qsl, :] = out.astype(o_ref.dtype)
        return carry

    lax.fori_loop(0, rows // nr, body, 0)


def _nbr_attention(p, ck, cv, tab, n_ctx, n_seq, seq_len):
    nb = WA // LANE
    past = ck.shape[1]
    nf = n_ctx // seq_len
    hp = LANE // HEAD_DIM_A
    tok = lambda off: pl.BlockSpec((seq_len, LANE), lambda b, h: (jnp.maximum(b, nf), off // LANE + h))
    cache = pl.BlockSpec((1, past, LANE), lambda b, h: (jnp.maximum(b - nf, 0), 0, h))
    return pl.pallas_call(
        functools.partial(_na_kernel, rows=seq_len // GRID_W, n_fill=nf), grid=(nf + n_seq, nb),
        in_specs=[tok(OFF_Q), tok(OFF_K), tok(OFF_V), cache, cache,
                  pl.BlockSpec((hp, NA_ROWS, GRID_W, NA_ROWS * GRID_W), lambda b, h: (h, 0, 0, 0))],
        out_specs=pl.BlockSpec((seq_len, LANE), lambda b, h: (b, h)),
        out_shape=jax.ShapeDtypeStruct((p.shape[0], WA), BF16),
        scratch_shapes=[pltpu.VMEM((seq_len, LANE), F32), pltpu.VMEM((seq_len, LANE), F32)],
        compiler_params=_cparams(("parallel", "parallel")), name="nbr_attention",
    )(p, p, p, ck, cv, tab)


def _seq_pos(shape, row0, seq_len):
    return (row0 + lax.broadcasted_iota(jnp.int32, shape, 0)) & (seq_len - 1)


def _shifted(x, t, d, seq_len):
    n = x.shape[0]
    y = pltpu.roll(x, (-d) % n, 0)
    return jnp.where((t + d >= 0) & (t + d < seq_len), y, 0.0)


def _conv_kernel(x_ref, w_ref, o_ref, *, n_ctx, ctx_len, lat_len):
    rb = x_ref.shape[0]
    row0 = pl.program_id(0) * rb
    seq_len = jnp.where(row0 < n_ctx, ctx_len, lat_len)
    x = x_ref[...]
    t = _seq_pos(x.shape, row0, seq_len)
    w = w_ref[...]
    o_ref[...] = (_shifted(x, t, -1, seq_len) * w[0:1] + x * w[1:2] + _shifted(x, t, 1, seq_len) * w[2:3])


def _short_conv(p, conv_w, n_ctx, ctx_len, lat_len):
    n = p.shape[0]
    rb = lat_len
    c = conv_w.shape[1]
    w8 = jnp.zeros((8, c), F32).at[:3].set(conv_w)
    return pl.pallas_call(
        functools.partial(_conv_kernel, n_ctx=n_ctx, ctx_len=ctx_len, lat_len=lat_len),
        grid=(n // rb, c // LANE),
        in_specs=[pl.BlockSpec((rb, LANE), lambda i, j: (i, OFF_RKV // LANE + j)),
                  pl.BlockSpec((8, LANE), lambda i, j: (0, j))],
        out_specs=pl.BlockSpec((rb, LANE), lambda i, j: (i, j)),
        out_shape=jax.ShapeDtypeStruct((n, c), F32),
        compiler_params=_cparams(("parallel", "parallel")), name="short_conv",
    )(p, w8)


def _pool_kernel(u0_ref, u1_ref, u2_ref, u3_ref, w_ref, sc_ref, o_ref, *, n_ctx, ctx_len, lat_len):
    rb = o_ref.shape[0]
    row0 = pl.program_id(0) * rb
    seq_len = jnp.where(row0 < n_ctx, ctx_len, lat_len)
    t = _seq_pos((rb, POOL_GROUP_DIM), row0, seq_len)
    for gi, (win, u_ref) in enumerate(zip(POOL_WINDOWS, (u0_ref, u1_ref, u2_ref, u3_ref))):
        sl = slice(gi * POOL_GROUP_DIM, (gi + 1) * POOL_GROUP_DIM)
        u = u_ref[...]
        acc = u
        for d in range(-(win // 2), win - win // 2):
            if d != 0:
                acc = acc + _shifted(u, t, d, seq_len)
        lo = jnp.maximum(t - win // 2, 0)
        hi = jnp.minimum(t + win - win // 2, seq_len)
        pooled = acc / (hi - lo).astype(F32) - u
        y = _dot(pooled.astype(BF16), w_ref[gi].astype(BF16))
        o_ref[:, sl] = (y * sc_ref[:, sl]).astype(o_ref.dtype)


def _multiscale_pool(p, pool_w, pool_scale, n_ctx, ctx_len, lat_len):
    n = p.shape[0]
    rb = lat_len
    gd = POOL_GROUP_DIM
    group = lambda gi: pl.BlockSpec((rb, gd), lambda i: (i, OFF_POOL // gd + gi))
    return pl.pallas_call(
        functools.partial(_pool_kernel, n_ctx=n_ctx, ctx_len=ctx_len, lat_len=lat_len),
        grid=(n // rb,),
        in_specs=[group(0), group(1), group(2), group(3),
                  pl.BlockSpec(pool_w.shape, lambda i: (0, 0, 0)),
                  pl.BlockSpec((1, WC), lambda i: (0, 0))],
        out_specs=pl.BlockSpec((rb, WC), lambda i: (i, 0)),
        out_shape=jax.ShapeDtypeStruct((n, WC), BF16),
        compiler_params=_cparams(("parallel",)), name="multiscale_pool",
    )(p, p, p, p, pool_w, pool_scale.reshape(1, WC))


def _head_ones():
    r = jnp.arange(LANE) // HEAD_DIM_B
    return (r[:, None] == r[None, :]).astype(F32)


def _rwkv_prep_kernel(r_ref, k_ref, v_ref, xwa_ref, xg0_ref, xg1_ref, wl_ref, g2a_ref, g2b_ref, pt_ref,
                      ones_ref, kk_o, lw_o, b_o, kd_o, g_o, bonus_o):
    r, k, v = r_ref[...], k_ref[...], v_ref[...]
    xwa = xwa_ref[...]
    lane = lax.broadcasted_iota(jnp.int32, xwa.shape, 1)
    act = jnp.where(lane < 2 * DECAY_LORA, jnp.tanh(xwa), xwa)
    lora = _dot(act.astype(BF16), wl_ref[...].astype(BF16))
    pt = pt_ref[...]
    ones = ones_ref[...]
    kkr = k * pt[4:5]
    kk = kkr * lax.rsqrt(_dot(kkr * kkr, ones, precision=HI) + 1e-12)
    kk_o[...] = kk
    for d in range(2):
        z = -(pt[d:d + 1] + lora[:, d * LANE:(d + 1) * LANE])
        softplus = jnp.maximum(z, 0.0) + jnp.log(1.0 + jnp.exp(-jnp.abs(z)))
        lw_o[d] = -jnp.exp(-softplus - 0.5)
        a = _sigmoid(pt[2 + d:3 + d] + lora[:, (2 + d) * LANE:(3 + d) * LANE])
        kd_o[d] = k * (1.0 + (a - 1.0) * pt[5:6])
        b_o[d] = kk * a
    g_o[...] = (_dot(_sigmoid(xg0_ref[...]).astype(BF16), g2a_ref[...].astype(BF16))
                + _dot(_sigmoid(xg1_ref[...]).astype(BF16), g2b_ref[...].astype(BF16)))
    bonus_o[...] = _dot(r * k * pt[6:7], ones, precision=HI) * v


def _rwkv_prep(p, rkv, w0, w2, a0, a2, g2, k_k, k_a, r_k, rb):
    n = p.shape[0]
    nhp = WB // LANE
    wl = jnp.zeros((4, DECAY_LORA, 4, WB), F32)
    wl = wl.at[0, :, 0].set(w2[0]).at[1, :, 1].set(w2[1]).at[2, :, 2].set(a2[0]).at[3, :, 3].set(a2[1])
    wl = wl.reshape(4 * DECAY_LORA, 4, nhp, LANE).transpose(0, 2, 1, 3).reshape(4 * DECAY_LORA, nhp * 4 * LANE)
    pt = jnp.stack([w0[0], w0[1], a0[0], a0[1], k_k, k_a, r_k, jnp.zeros_like(k_k)]).astype(F32)
    nlo = 4 * DECAY_LORA
    col = lambda off: (lambda i, h: (i, off + h))
    tok = jax.ShapeDtypeStruct((n, WB), F32)
    tok2 = jax.ShapeDtypeStruct((2, n, WB), F32)
    spec1 = pl.BlockSpec((rb, LANE), lambda i, h: (i, h))
    spec2 = pl.BlockSpec((2, rb, LANE), lambda i, h: (0, i, h))
    return pl.pallas_call(
        _rwkv_prep_kernel, grid=(n // rb, nhp),
        in_specs=[pl.BlockSpec((rb, LANE), col(0)),
                  pl.BlockSpec((rb, LANE), col(nhp)),
                  pl.BlockSpec((rb, LANE), col(2 * nhp)),
                  pl.BlockSpec((rb, nlo), lambda i, h: (i, OFF_WL // nlo)),
                  pl.BlockSpec((rb, LANE), lambda i, h: (i, OFF_GL // LANE)),
                  pl.BlockSpec((rb, LANE), lambda i, h: (i, OFF_GL // LANE + 1)),
                  pl.BlockSpec((nlo, 4 * LANE), lambda i, h: (0, h)),
                  pl.BlockSpec((LANE, LANE), lambda i, h: (0, h)),
                  pl.BlockSpec((LANE, LANE), lambda i, h: (1, h)),
                  pl.BlockSpec((8, LANE), lambda i, h: (0, h)),
                  pl.BlockSpec((LANE, LANE), lambda i, h: (0, 0))],
        out_specs=[spec1, spec2, spec2, spec2, spec1, spec1],
        out_shape=[tok, tok2, tok2, tok2, tok, tok],
        compiler_params=_cparams(("parallel", "parallel")), name="rwkv_prep",
    )(rkv, rkv, rkv, p, p, p, wl, g2, g2, pt, _head_ones())


def _split_bf16(a):
    hi = pltpu.bitcast(pltpu.bitcast(a, jnp.uint32) & jnp.uint32(0xFFFF0000), F32)
    return hi.astype(BF16), (a - hi).astype(BF16)


_NN = (((1,), (0,)), ((), ()))
_NT = (((1,), (1,)), ((), ()))
_TN = (((0,), (0,)), ((), ()))


def _mm(a, b, passes, dims=_NN):
    if passes == 6:
        return lax.dot_general(a, b, dims, preferred_element_type=F32, precision=HI)
    dg = lambda p, q: lax.dot_general(p, q, dims, preferred_element_type=F32)
    if passes == 1:
        return dg(a.astype(BF16), b.astype(BF16))
    ah, al = _split_bf16(a)
    bh, bl = _split_bf16(b)
    return dg(ah, bh) + (dg(ah, bl) + dg(al, bh))


def _scan_kernel(rf_ref, vf_ref, kkf_ref, rb_ref, vb_ref, kkb_ref, lwf_ref, bf_ref, kdf_ref, lwb_ref, bb_ref,
                 kdb_ref, s0_ref, of_ref, ob_ref, se_ref, s_ref, *, ctx_steps, ctx_chunks, lat_chunks, passes):
    p_gram, p_inv, p_state = passes
    c = SCAN_CHUNK
    hd = HEAD_DIM_B
    j = pl.program_id(0)
    is_lat = j >= ctx_steps
    ci = jnp.where(is_lat, (j - ctx_steps) & (lat_chunks - 1), j & (ctx_chunks - 1))
    last = jnp.where(is_lat, lat_chunks, ctx_chunks) - 1

    @pl.when((ci == 0) & jnp.logical_not(is_lat))
    def _():
        s_ref[...] = jnp.zeros(s_ref.shape, F32)

    @pl.when((ci == 0) & is_lat)
    def _():
        s_ref[...] = s0_ref[:, 0]

    row = lax.broadcasted_iota(jnp.int32, (c, c), 0)
    col = lax.broadcasted_iota(jnp.int32, (c, c), 1)
    row2 = lax.broadcasted_iota(jnp.int32, (c, 2 * c), 0)
    col2 = lax.broadcasted_iota(jnp.int32, (c, 2 * c), 1) & (c - 1)
    eye = (col == row).astype(F32)
    sls = [slice(h * hd, (h + 1) * hd) for h in range(N_HEADS_B)]

    z, y, ye, vh, g_end, strict, incl2, sidx = [], [], [], [], [], [], [], []
    for d, (r_ref, v_ref, kk_ref, lw_ref, b_ref, kd_ref) in enumerate(
            ((rf_ref, vf_ref, kkf_ref, lwf_ref, bf_ref, kdf_ref), (rb_ref, vb_ref, kkb_ref, lwb_ref, bb_ref, kdb_ref))):
        before = (col < row) if d == 0 else (col > row)
        before2 = (col2 <= row2) if d == 0 else (col2 >= row2)
        lw = lw_ref[0]
        cum = _dot((before | (col == row)).astype(F32), lw, precision=HI)
        tot = jnp.sum(lw, axis=0, keepdims=True)
        r, v, kk, b, kd = r_ref[...], v_ref[...], kk_ref[...], b_ref[0], kd_ref[0]
        e_neg = jnp.exp(-cum)
        e_end = jnp.exp(tot - cum)
        kk_h = kk * jnp.exp(cum - lw)
        r_h = r * jnp.exp(cum)
        b_t, k_t = b * e_neg, kd * e_neg
        b_e, k_e = b * e_end, kd * e_end
        ge = jnp.exp(tot)
        for h, sl in enumerate(sls):
            z.append(jnp.concatenate([kk_h[:, sl], r_h[:, sl]], axis=0))
            y.append(jnp.concatenate([b_t[:, sl], k_t[:, sl]], axis=0))
            ye.append(jnp.concatenate([b_e[:, sl], k_e[:, sl]], axis=0))
            vh.append(v[:, sl])
            g_end.append(ge[:, sl])
            strict.append(before)
            incl2.append(before2)
            sidx.append((d, h))
    n = range(len(z))
    g = [_mm(z[i], y[i], p_gram, _NT) for i in n]
    x = [jnp.where(strict[i], -g[i][:c, :c], 0.0) for i in n]
    m_k = [jnp.where(strict[i], g[i][:c, c:], 0.0) for i in n]
    p_bk = [jnp.where(incl2[i], g[i][c:], 0.0) for i in n]
    tinv = [eye + x[i] for i in n]
    for _ in range(c.bit_length() - 2):
        x = [_mm(x[i], x[i], p_inv) for i in n]
        tinv = [tinv[i] + _mm(tinv[i], x[i], p_inv) for i in n]
    s = [s_ref[sidx[i]] for i in n]
    zs = [_mm(z[i], s[i], p_state, _NT) for i in n]
    mkv = [_mm(m_k[i], vh[i], p_state) for i in n]
    u = [-_mm(tinv[i], zs[i][:c] + mkv[i], p_state) for i in n]
    w = [jnp.concatenate([u[i], vh[i]], axis=0) for i in n]
    o = [zs[i][c:] + _mm(p_bk[i], w[i], p_state) for i in n]
    of_ref[...] = jnp.concatenate(o[:N_HEADS_B], axis=-1)
    ob_ref[...] = jnp.concatenate(o[N_HEADS_B:], axis=-1)
    for i in n:
        s_ref[sidx[i]] = s[i] * g_end[i] + _mm(w[i], ye[i], p_state, _TN)

    @pl.when(ci == last)
    def _():
        se_ref[:, 0] = s_ref[...]


def _rwkv_scan(rkv, kk, lw, b, kd, s0, n_cs, ctx_len, n_ls, lat_len, passes=(1, 1, 1)):
    c = SCAN_CHUNK
    cc, lc = ctx_len // c, lat_len // c
    assert cc & (cc - 1) == 0 and lc & (lc - 1) == 0
    ctx_steps = n_cs * cc
    n_steps = ctx_steps + n_ls * lc
    nb = WB // WB
    n = kk.shape[0]

    def where(j):
        jl = jnp.maximum(j - ctx_steps, 0)
        is_lat = j >= ctx_steps
        seq = jnp.where(is_lat, n_cs + jl // lc, j // cc)
        start = jnp.where(is_lat, ctx_steps + (jl // lc) * lc, (j // cc) * cc)
        ch = jnp.where(is_lat, jl % lc, j % cc)
        return seq, start, ch, jnp.where(is_lat, lc, cc)

    def rb(d, j):
        _, start, ch, nch = where(j)
        return start + ch + d * (nch - 1 - 2 * ch)

    tok = lambda d, colblk: pl.BlockSpec((c, WB), lambda j: (rb(d, j), colblk))
    tok2 = lambda d: pl.BlockSpec((1, c, WB), lambda j: (d, rb(d, j), 0))
    st_shape = (2, 1, N_HEADS_B, HEAD_DIM_B, HEAD_DIM_B)
    return pl.pallas_call(
        functools.partial(_scan_kernel, ctx_steps=ctx_steps, ctx_chunks=cc, lat_chunks=lc, passes=passes),
        grid=(n_steps,),
        in_specs=[tok(0, 0), tok(0, 2 * nb), tok(0, 0), tok(1, 0), tok(1, 2 * nb), tok(1, 0),
                  tok2(0), tok2(0), tok2(0), tok2(1), tok2(1), tok2(1),
                  pl.BlockSpec(st_shape, lambda j: (0, jnp.maximum(where(j)[0] - n_cs, 0), 0, 0, 0))],
        out_specs=[tok(0, 0), tok(1, 0), pl.BlockSpec(st_shape, lambda j: (0, where(j)[0], 0, 0, 0))],
        out_shape=[jax.ShapeDtypeStruct((n, WB), F32), jax.ShapeDtypeStruct((n, WB), F32),
                   jax.ShapeDtypeStruct((2, n_cs + n_ls, N_HEADS_B, HEAD_DIM_B, HEAD_DIM_B), F32)],
        scratch_shapes=[pltpu.VMEM((2, N_HEADS_B, HEAD_DIM_B, HEAD_DIM_B), F32)],
        compiler_params=_cparams(("arbitrary",)), name="rwkv_scan",
    )(rkv, rkv, kk, rkv, rkv, kk, lw, b, kd, lw, b, kd, s0)


def _rwkv_post_kernel(of_ref, ob_ref, bonus_ref, g_ref, lng_ref, lnb_ref, ones_ref, y_ref):
    o = of_ref[...] + ob_ref[...]
    mean_mat = ones_ref[...] * (1.0 / HEAD_DIM_B)
    mu = _dot(o, mean_mat, precision=HI)
    var = _dot(jnp.square(o - mu), mean_mat, precision=HI)
    y = (o - mu) * lax.rsqrt(var + GN_EPS) * lng_ref[...] + lnb_ref[...]
    y_ref[...] = ((y + bonus_ref[...]) * g_ref[...]).astype(y_ref.dtype)


def _rwkv_post(o_f, o_b, bonus, g, ln_g, ln_b, rb):
    n = bonus.shape[0]
    nhp = WB // LANE
    spec = pl.BlockSpec((rb, LANE), lambda i, h: (i, h))
    vec = pl.BlockSpec((1, LANE), lambda i, h: (0, h))
    return pl.pallas_call(
        _rwkv_post_kernel, grid=(n // rb, nhp),
        in_specs=[spec, spec, spec, spec, vec, vec, pl.BlockSpec((LANE, LANE), lambda i, h: (0, 0))],
        out_specs=spec,
        out_shape=jax.ShapeDtypeStruct((n, WB), BF16),
        compiler_params=_cparams(("parallel", "parallel")), name="rwkv_post",
    )(o_f, o_b, bonus, g, ln_g.reshape(1, WB), ln_b.reshape(1, WB), _head_ones())


def _outproj_kernel(a_ref, b_ref, c_ref, w_ref, x_ref, gt_ref, o_ref):
    y = (_dot(a_ref[...], w_ref[0:WA])
         + _dot(b_ref[...], w_ref[WA:WA + WB])
         + _dot(c_ref[...], w_ref[WA + WB:]))
    o_ref[...] = x_ref[...] + gt_ref[0] * y


def _out_proj(oa, ob, oc, w, x, gt, n_ctx, lat_len, tm, tn):
    n, d = x.shape
    grp = lambda i, j: (_group_of_rows(i * tm, n_ctx, lat_len), 0, j)
    return pl.pallas_call(
        _outproj_kernel, grid=(n // tm, d // tn),
        in_specs=[pl.BlockSpec((tm, WA), lambda i, j: (i, 0)),
                  pl.BlockSpec((tm, WB), lambda i, j: (i, 0)),
                  pl.BlockSpec((tm, WC), lambda i, j: (i, 0)),
                  pl.BlockSpec((d, tn), lambda i, j: (0, j)),
                  pl.BlockSpec((tm, tn), lambda i, j: (i, j)),
                  pl.BlockSpec((1, 1, tn), grp)],
        out_specs=pl.BlockSpec((tm, tn), lambda i, j: (i, j)),
        out_shape=jax.ShapeDtypeStruct((n, d), F32),
        compiler_params=_cparams(("parallel", "parallel")), name="out_proj",
    )(oa, ob, oc, w, x, gt)


def _swiglu_kernel(te_ref, nt_ref, h_ref, w1_ref, w3_ref, w2_ref, o_ref):
    i = pl.program_id(0)
    f = pl.program_id(1)

    @pl.when(f == 0)
    def _():
        o_ref[...] = jnp.zeros(o_ref.shape, F32)

    @pl.when(i < nt_ref[0])
    def _():
        h = h_ref[...]
        a = _dot(h, w1_ref[0].astype(BF16))
        b = _dot(h, w3_ref[0].astype(BF16))
        act = (a * _sigmoid(a) * b).astype(BF16)
        o_ref[...] += _dot(act, w2_ref[0].astype(BF16))


def _swiglu(h, tile_expert, n_tiles_used, w1, w3, w2, tm, tf, out_buffers=2):
    n, d = h.shape
    ff = w1.shape[2]
    return pl.pallas_call(
        _swiglu_kernel,
        grid_spec=pltpu.PrefetchScalarGridSpec(
            num_scalar_prefetch=2, grid=(n // tm, ff // tf),
            in_specs=[pl.BlockSpec((tm, d), lambda i, f, te, nt: (i, 0)),
                      pl.BlockSpec((1, d, tf), lambda i, f, te, nt: (te[i], 0, f)),
                      pl.BlockSpec((1, d, tf), lambda i, f, te, nt: (te[i], 0, f)),
                      pl.BlockSpec((1, tf, d), lambda i, f, te, nt: (te[i], f, 0))],
            out_specs=pl.BlockSpec((tm, d), lambda i, f, te, nt: (i, 0),
                                   pipeline_mode=pl.Buffered(out_buffers))),
        out_shape=jax.ShapeDtypeStruct((n, d), F32),
        compiler_params=_cparams(("parallel", "arbitrary")), name="swiglu",
    )(tile_expert, n_tiles_used, h, w1, w3, w2)


def _residual_kernel(x_ref, f_ref, gt_ref, o_ref):
    o_ref[...] = x_ref[...] + gt_ref[0] * f_ref[...]


def _residual(x, f, gt, n_ctx, lat_len, tm):
    n, d = x.shape
    row = pl.BlockSpec((tm, d), lambda i: (i, 0))
    return pl.pallas_call(
        _residual_kernel, grid=(n // tm,),
        in_specs=[row, row, pl.BlockSpec((1, 1, d), lambda i: (_group_of_rows(i * tm, n_ctx, lat_len), 0, 0))],
        out_specs=row, out_shape=jax.ShapeDtypeStruct((n, d), F32),
        compiler_params=_cparams(("parallel",)), name="residual",
    )(x, f, gt)


GATHER_UNROLL = 8


def _gather_into(idx_ref, base, src_ref, buf_ref, sem):
    rt = buf_ref.shape[0]

    def copy(r):
        return pltpu.make_async_copy(src_ref.at[pl.ds(idx_ref[base + r], 1)], buf_ref.at[pl.ds(r, 1)], sem)

    def start(r, carry):
        copy(r).start()
        return carry

    def wait(r, carry):
        copy(r).wait()
        return carry

    lax.fori_loop(0, rt, start, 0, unroll=GATHER_UNROLL)
    lax.fori_loop(0, rt, wait, 0, unroll=GATHER_UNROLL)


def _gather_rows_kernel(idx_ref, src_ref, o_ref, buf_ref, sem):
    _gather_into(idx_ref, pl.program_id(0) * o_ref.shape[0], src_ref, buf_ref, sem)
    o_ref[...] = buf_ref[...].astype(o_ref.dtype)


def _gather_rows(src, idx, rt, out_dtype):
    n_out = idx.shape[0]
    d = src.shape[1]
    return pl.pallas_call(
        _gather_rows_kernel,
        grid_spec=pltpu.PrefetchScalarGridSpec(
            num_scalar_prefetch=1, grid=(n_out // rt,),
            in_specs=[pl.BlockSpec(memory_space=pl.ANY)],
            out_specs=pl.BlockSpec((rt, d), lambda i, idx: (i, 0)),
            scratch_shapes=[pltpu.VMEM((rt, d), src.dtype), pltpu.SemaphoreType.DMA(())]),
        out_shape=jax.ShapeDtypeStruct((n_out, d), out_dtype),
        compiler_params=_cparams(("arbitrary",)), name="gather_rows",
    )(idx, src)


def _combine_kernel(pos_ref, x_ref, gates_ref, gt_ref, ys_ref, o_ref, y1_ref, y2_ref, sem1, sem2, *, n):
    tm = o_ref.shape[0]
    base = pl.program_id(0) * tm
    _gather_into(pos_ref, base, ys_ref, y1_ref, sem1)
    _gather_into(pos_ref, n + base, ys_ref, y2_ref, sem2)
    g = gates_ref[...]
    o_ref[...] = x_ref[...] + gt_ref[0] * (g[:, 0:1] * y1_ref[...] + g[:, 1:2] * y2_ref[...])


def _combine(x, ys, pos, gates, gt, n_ctx, lat_len, tm):
    n, d = x.shape
    row = pl.BlockSpec((tm, d), lambda i, pos: (i, 0))
    return pl.pallas_call(
        functools.partial(_combine_kernel, n=n),
        grid_spec=pltpu.PrefetchScalarGridSpec(
            num_scalar_prefetch=1, grid=(n // tm,),
            in_specs=[row, pl.BlockSpec((tm, LANE), lambda i, pos: (i, 0)),
                      pl.BlockSpec((1, 1, d), lambda i, pos: (_group_of_rows(i * tm, n_ctx, lat_len), 0, 0)),
                      pl.BlockSpec(memory_space=pl.ANY)],
            out_specs=row,
            scratch_shapes=[pltpu.VMEM((tm, d), F32), pltpu.VMEM((tm, d), F32),
                            pltpu.SemaphoreType.DMA(()), pltpu.SemaphoreType.DMA(())]),
        out_shape=jax.ShapeDtypeStruct((n, d), F32),
        compiler_params=_cparams(("arbitrary",)), name="moe_combine",
    )(pos, x, gates, gt, ys)


def _moe(x, h, gates, idx, w1, w3, w2, gt, n_ctx, lat_len, tm, tf, rt):
    n, d = x.shape
    n_e = w1.shape[0]
    e = jnp.concatenate([idx[:, 0], idx[:, 1]])
    onehot = (e[:, None] == jnp.arange(n_e)[None, :]).astype(jnp.int32)
    rank = jnp.take_along_axis(jnp.cumsum(onehot, axis=0), e[:, None], axis=1)[:, 0] - 1
    counts = jnp.sum(onehot, axis=0)
    tiles = (counts + tm - 1) // tm
    tile_end = jnp.cumsum(tiles)
    start = (tile_end - tiles) * tm
    pos = (start[e] + rank).astype(jnp.int32)
    n_rows = 2 * n + n_e * tm
    src = jnp.zeros((n_rows,), jnp.int32).at[pos].set(jnp.tile(jnp.arange(n, dtype=jnp.int32), 2))
    tile_ids = jnp.arange(n_rows // tm)
    tile_expert = jnp.minimum(jnp.sum(tile_ids[:, None] >= tile_end[None, :], axis=1), n_e - 1).astype(jnp.int32)
    n_used = tile_end[-1:].astype(jnp.int32)

    hs = _gather_rows(h, src, rt, BF16)
    ys = _swiglu(hs, tile_expert, n_used, w1, w3, w2, tm, tf, out_buffers=1)
    return _combine(x, ys, pos, gates, gt, n_ctx, lat_len, rt)


def _forward(x_prompt, x_sample, cache_attn_k, cache_attn_v, state_rwkv_fwd, state_rwkv_bwd, c, c_ctx,
             norm1_g, norm2_g, ada_w, ada_b, w_in, w_out, na_rpb, rw_conv, rw_w0, rw_w2, rw_a0, rw_a2,
             rw_g2, rw_kk, rw_ka, rw_rk, rw_ln_g, rw_ln_b, pool_w, pool_scale, ffn_w1, ffn_w3, ffn_w2,
             moe_router, moe_w1, moe_w3, moe_w2, final_g, *, tm, tm_proj, tn_in, tn_out, tm_dense, tf_dense,
             tm_moe, tf_moe, rt, rb):
    n_cs, ctx_len, d = x_prompt.shape
    n_ls, lat_len, _ = x_sample.shape
    n_ctx = n_cs * ctx_len
    n_lat = n_ls * lat_len
    n = n_ctx + n_lat
    depth = w_in.shape[0]
    past = cache_attn_k.shape[2]
    assert ctx_len & (ctx_len - 1) == 0 and lat_len & (lat_len - 1) == 0
    assert lat_len % ctx_len == 0 and n_ctx % lat_len == 0 and lat_len % GRID_W == 0

    x = jnp.concatenate([x_prompt.reshape(n_ctx, d), x_sample.reshape(n_lat, d)], axis=0)
    cond8 = jnp.zeros((8, d), F32).at[0].set(c_ctx).at[1:1 + n_ls].set(c)
    mods = _ada(cond8, ada_w, ada_b).reshape(depth, 8, 6, 1, d)

    ks_new, vs_new, sf_new, sb_new = [], [], [], []
    dense_tiles = jnp.zeros((n // tm_dense,), jnp.int32)
    dense_used = jnp.full((1,), n // tm_dense, jnp.int32)
    for l in range(depth):
        m = [mods[l, :1 + n_ls, k] for k in range(6)]
        p = _in_proj(x, norm1_g[l], m[1], m[0], w_in[l].astype(BF16), n_ctx, lat_len, tm_proj, tn_in)

        ck = cache_attn_k[:, l].reshape(n_ls, past, WA)
        cv = cache_attn_v[:, l].reshape(n_ls, past, WA)
        oa = _nbr_attention(p, ck, cv, _na_bias_table(na_rpb[l]), n_ctx, n_ls, lat_len)
        oa = _ctx_attention(p, oa, n_cs, ctx_len)

        rkv = _short_conv(p, rw_conv[l], n_ctx, ctx_len, lat_len)
        kk, lw, b, kd, g, bonus = _rwkv_prep(p, rkv, rw_w0[l], rw_w2[l], rw_a0[l], rw_a2[l], rw_g2[l],
                                             rw_kk[l], rw_ka[l], rw_rk[l], rb)
        s0 = jnp.stack([state_rwkv_fwd[:, l], state_rwkv_bwd[:, l]]).astype(F32)
        o_f, o_b, s_end = _rwkv_scan(rkv, kk, lw, b, kd, s0, n_cs, ctx_len, n_ls, lat_len)
        ob = _rwkv_post(o_f, o_b, bonus, g, rw_ln_g[l], rw_ln_b[l], rb)

        oc = _multiscale_pool(p, pool_w[l], pool_scale[l], n_ctx, ctx_len, lat_len)
        x = _out_proj(oa, ob, oc, w_out[l].astype(BF16), x, m[2], n_ctx, lat_len, tm_proj, tn_out)

        ks_new.append(p[:n_ctx, OFF_K:OFF_V].reshape(n_cs, ctx_len, N_HEADS_A, HEAD_DIM_A))
        vs_new.append(p[:n_ctx, OFF_V:OFF_RKV].reshape(n_cs, ctx_len, N_HEADS_A, HEAD_DIM_A))
        sf_new.append(s_end[0, :n_cs])
        sb_new.append(s_end[1, :n_cs])

        i = l // 2
        if l % 2 == 0:
            h = _norm_mod(x, norm2_g[l], m[4], m[3], n_ctx, lat_len, tm)
            f = _swiglu(h, dense_tiles, dense_used, ffn_w1[i:i + 1].astype(BF16),
                        ffn_w3[i:i + 1].astype(BF16), ffn_w2[i:i + 1].astype(BF16), tm_dense, tf_dense)
            x = _residual(x, f, m[5], n_ctx, lat_len, tm)
        else:
            h, gates, idx = _norm_mod(x, norm2_g[l], m[4], m[3], n_ctx, lat_len, tm, router=moe_router[i])
            x = _moe(x, h, gates, idx, moe_w1[i], moe_w3[i], moe_w2[i], m[5], n_ctx, lat_len, tm_moe, tf_moe, rt)

    y = _final_norm(x, final_g, tm)
    return (y[:n_ctx].reshape(n_cs, ctx_len, d), y[n_ctx:].reshape(n_ls, lat_len, d),
            jnp.stack(ks_new, axis=1), jnp.stack(vs_new, axis=1),
            jnp.stack(sf_new, axis=1), jnp.stack(sb_new, axis=1))


def kernel(x_prompt, x_sample, cache_attn_k, cache_attn_v, state_rwkv_fwd, state_rwkv_bwd, c, c_ctx, norm1_g, norm2_g, ada_w, ada_b, w_in, w_out, na_rpb, rw_conv, rw_w0, rw_w2, rw_a0, rw_a2, rw_g2, rw_kk, rw_ka, rw_rk, rw_ln_g, rw_ln_b, pool_w, pool_scale, ffn_w1, ffn_w3, ffn_w2, moe_router, moe_w1, moe_w3, moe_w2, final_g):
    return _forward(x_prompt, x_sample, cache_attn_k, cache_attn_v, state_rwkv_fwd, state_rwkv_bwd, c, c_ctx,
                    norm1_g, norm2_g, ada_w, ada_b, w_in, w_out, na_rpb, rw_conv, rw_w0, rw_w2, rw_a0, rw_a2,
                    rw_g2, rw_kk, rw_ka, rw_rk, rw_ln_g, rw_ln_b, pool_w, pool_scale, ffn_w1, ffn_w3, ffn_w2,
                    moe_router, moe_w1, moe_w3, moe_w2, final_g,
                    tm=512, tm_proj=1024, tn_in=1152, tn_out=1024, tm_dense=1024, tf_dense=512,
                    tm_moe=1024, tf_moe=512, rt=256, rb=512)
```

```python
import functools

import jax
import jax.numpy as jnp
from jax import lax
from jax.experimental import pallas as pl
from jax.experimental.pallas import tpu as pltpu

F32 = jnp.float32
BF16 = jnp.bfloat16
HI = lax.Precision.HIGHEST

D_MODEL = 2048
DEPTH = 2
GRID_W = 64
WA = D_MODEL // 2
WB = D_MODEL // 4
WC = D_MODEL - WA - WB
HEAD_DIM_A = 64
N_HEADS_A = WA // HEAD_DIM_A
HEAD_DIM_B = 64
N_HEADS_B = WB // HEAD_DIM_B
POOL_WINDOWS = (2, 4, 8, 16)
POOL_GROUP_DIM = WC // len(POOL_WINDOWS)
NA_ROWS = 8
NA_COLS = 16
DECAY_LORA = 96
AAA_LORA = 96
GATE_LORA = 256
OFF_Q = 0
OFF_K = WA
OFF_V = 2 * WA
OFF_RKV = 3 * WA
OFF_WL = OFF_RKV + 3 * WB
OFF_AL = OFF_WL + 2 * DECAY_LORA
OFF_GL = OFF_AL + 2 * AAA_LORA
OFF_POOL = OFF_GL + GATE_LORA
P_IN = OFF_POOL + WC
N_EXPERTS = 8
RMS_EPS = 1e-6
GN_EPS = 64e-5
NEG_INF = -1e30

LANE = 128
VMEM_LIMIT = 56 * 1024 * 1024
SCAN_CHUNK = 64
NORM_ROWS = 256


def _cparams(sem):
    return pltpu.CompilerParams(dimension_semantics=sem, vmem_limit_bytes=VMEM_LIMIT)


def _dot(a, b, precision=None):
    return jnp.dot(a, b, preferred_element_type=F32, precision=precision)


def _dot_nt(a, b, precision=None):
    return lax.dot_general(a, b, (((1,), (1,)), ((), ())), preferred_element_type=F32, precision=precision)


def _dot_tn(a, b, precision=None):
    return lax.dot_general(a, b, (((0,), (0,)), ((), ())), preferred_element_type=F32, precision=precision)


def _sigmoid(x):
    return 1.0 / (1.0 + jnp.exp(-x))


def _ada_kernel(c_ref, w_ref, b_ref, o_ref):
    c = c_ref[...]
    s = c * _sigmoid(c)
    o_ref[0] = _dot(s.astype(BF16), w_ref[0].astype(BF16)) + b_ref[0]


def _ada(cond8, ada_w, ada_b):
    depth, d, n6 = ada_w.shape
    tn = 768
    return pl.pallas_call(
        _ada_kernel,
        grid=(depth, n6 // tn),
        in_specs=[pl.BlockSpec((8, d), lambda l, j: (0, 0)),
                  pl.BlockSpec((1, d, tn), lambda l, j: (l, 0, j)),
                  pl.BlockSpec((1, 1, tn), lambda l, j: (l, 0, j))],
        out_specs=pl.BlockSpec((1, 8, tn), lambda l, j: (l, 0, j)),
        out_shape=jax.ShapeDtypeStruct((depth, 8, n6), F32),
        compiler_params=_cparams(("parallel", "parallel")),
        name="ada_mod",
    )(cond8, ada_w, ada_b.reshape(depth, 1, n6))


def _modulated(x, g, sc, sh):
    y = x * lax.rsqrt(jnp.mean(x * x, axis=-1, keepdims=True) + RMS_EPS)
    return (y * g) * (1.0 + sc) + sh


def _group_of_rows(row0, n_ctx, lat_len):
    return jnp.where(row0 < n_ctx, 0, 1 + (row0 - n_ctx) // lat_len)


def _inproj_kernel(x_ref, g_ref, sc_ref, sh_ref, w_ref, o_ref, h_ref):
    @pl.when(pl.program_id(1) == 0)
    def _():
        sub = NORM_ROWS

        def body(k, carry):
            rows = pl.ds(pl.multiple_of(k * sub, sub), sub)
            h_ref[rows, :] = _modulated(x_ref[rows, :], g_ref[...], sc_ref[0], sh_ref[0]).astype(BF16)
            return carry

        lax.fori_loop(0, x_ref.shape[0] // sub, body, 0)

    o_ref[...] = _dot(h_ref[...], w_ref[...])


def _in_proj(x, g, sc, sh, w, n_ctx, lat_len, tm, tn):
    n, d = x.shape
    pin = w.shape[1]
    grp = lambda i, j: (_group_of_rows(i * tm, n_ctx, lat_len), 0, 0)
    return pl.pallas_call(
        _inproj_kernel,
        grid=(n // tm, pin // tn),
        in_specs=[pl.BlockSpec((tm, d), lambda i, j: (i, 0)),
                  pl.BlockSpec((1, d), lambda i, j: (0, 0)),
                  pl.BlockSpec((1, 1, d), grp),
                  pl.BlockSpec((1, 1, d), grp),
                  pl.BlockSpec((d, tn), lambda i, j: (0, j))],
        out_specs=pl.BlockSpec((tm, tn), lambda i, j: (i, j)),
        out_shape=jax.ShapeDtypeStruct((n, pin), F32),
        scratch_shapes=[pltpu.VMEM((tm, d), BF16)],
        compiler_params=_cparams(("parallel", "arbitrary")),
        name="in_proj",
    )(x, g.reshape(1, d), sc, sh, w)


def _normmod_kernel(x_ref, g_ref, sc_ref, sh_ref, h_ref):
    h_ref[...] = _modulated(x_ref[...], g_ref[...], sc_ref[0], sh_ref[0]).astype(h_ref.dtype)


def _normmod_router_kernel(x_ref, g_ref, sc_ref, sh_ref, rt_ref, h_ref, gate_ref, idx_ref):
    h = _modulated(x_ref[...], g_ref[...], sc_ref[0], sh_ref[0])
    h_ref[...] = h
    logits = _dot(h, rt_ref[...], precision=HI)
    lane = lax.broadcasted_iota(jnp.int32, logits.shape, 1)
    lanef = lane.astype(F32)
    logits = jnp.where(lane < N_EXPERTS, logits, -jnp.inf)
    m1 = jnp.max(logits, axis=-1, keepdims=True)
    i1 = jnp.min(jnp.where(logits == m1, lanef, float(LANE)), axis=-1, keepdims=True)
    rest = jnp.where(lanef == i1, -jnp.inf, logits)
    m2 = jnp.max(rest, axis=-1, keepdims=True)
    i2 = jnp.min(jnp.where(rest == m2, lanef, float(LANE)), axis=-1, keepdims=True)
    e2 = jnp.exp(m2 - m1)
    p1 = 1.0 / (1.0 + e2)
    p2 = e2 / (1.0 + e2)
    gate_ref[...] = jnp.where(lane == 0, p1, jnp.where(lane == 1, p2, 0.0))
    idx_ref[...] = jnp.where(lane == 0, i1, jnp.where(lane == 1, i2, 0.0)).astype(jnp.int32)


def _norm_mod(x, g, sc, sh, n_ctx, lat_len, tm, router=None):
    n, d = x.shape
    grp = lambda i: (_group_of_rows(i * tm, n_ctx, lat_len), 0, 0)
    in_specs = [pl.BlockSpec((tm, d), lambda i: (i, 0)),
                pl.BlockSpec((1, d), lambda i: (0, 0)),
                pl.BlockSpec((1, 1, d), grp),
                pl.BlockSpec((1, 1, d), grp)]
    row_spec = pl.BlockSpec((tm, d), lambda i: (i, 0))
    if router is None:
        return pl.pallas_call(
            _normmod_kernel, grid=(n // tm,), in_specs=in_specs, out_specs=row_spec,
            out_shape=jax.ShapeDtypeStruct((n, d), BF16),
            compiler_params=_cparams(("parallel",)), name="norm_mod",
        )(x, g.reshape(1, d), sc, sh)
    rt = jnp.zeros((d, LANE), F32).at[:, :N_EXPERTS].set(router)
    lane_spec = pl.BlockSpec((tm, LANE), lambda i: (i, 0))
    return pl.pallas_call(
        _normmod_router_kernel, grid=(n // tm,),
        in_specs=in_specs + [pl.BlockSpec((d, LANE), lambda i: (0, 0))],
        out_specs=[row_spec, lane_spec, lane_spec],
        out_shape=[jax.ShapeDtypeStruct((n, d), F32),
                   jax.ShapeDtypeStruct((n, LANE), F32),
                   jax.ShapeDtypeStruct((n, LANE), jnp.int32)],
        compiler_params=_cparams(("parallel",)), name="norm_mod_router",
    )(x, g.reshape(1, d), sc, sh, rt)


def _final_norm_kernel(x_ref, g_ref, o_ref):
    x = x_ref[...]
    o_ref[...] = (x * lax.rsqrt(jnp.mean(x * x, axis=-1, keepdims=True) + RMS_EPS)) * g_ref[...]


def _final_norm(x, g, tm):
    n, d = x.shape
    return pl.pallas_call(
        _final_norm_kernel, grid=(n // tm,),
        in_specs=[pl.BlockSpec((tm, d), lambda i: (i, 0)), pl.BlockSpec((1, d), lambda i: (0, 0))],
        out_specs=pl.BlockSpec((tm, d), lambda i: (i, 0)),
        out_shape=jax.ShapeDtypeStruct((n, d), F32),
        compiler_params=_cparams(("parallel",)), name="final_norm",
    )(x, g.reshape(1, d))


CTX_ATTN_LANES = 512


def _head_masks(shape):
    lane = lax.broadcasted_iota(jnp.int32, shape, 1)
    return [lane // HEAD_DIM_A == h for h in range(LANE // HEAD_DIM_A)]


def _ctx_attn_kernel(q_ref, k_ref, v_ref, oa_ref, o_ref):
    del oa_ref
    scale = HEAD_DIM_A ** -0.5
    nh = LANE // HEAD_DIM_A
    hm = _head_masks((q_ref.shape[0], LANE))
    chains = []
    for j in range(q_ref.shape[1] // LANE):
        cols = slice(j * LANE, (j + 1) * LANE)
        q, k, v = q_ref[:, cols], k_ref[:, cols].astype(BF16), v_ref[:, cols].astype(BF16)
        chains += [(jnp.where(hm[h], q, 0.0).astype(BF16), k, v) for h in range(nh)]
    s = [_dot_nt(q, k) * scale for q, k, _ in chains]
    e = [jnp.exp(si - jnp.max(si, axis=-1, keepdims=True)) for si in s]
    inv = [1.0 / jnp.sum(ei, axis=-1, keepdims=True) for ei in e]
    pv = [_dot((ei * ii).astype(BF16), v) for ei, ii, (_, _, v) in zip(e, inv, chains)]
    for j in range(q_ref.shape[1] // LANE):
        out = jnp.where(hm[0], pv[j * nh], pv[j * nh + 1])
        o_ref[:, j * LANE:(j + 1) * LANE] = out.astype(o_ref.dtype)


def _ctx_attention(p, oa, n_seq, seq_len):
    w = CTX_ATTN_LANES
    return pl.pallas_call(
        _ctx_attn_kernel, grid=(n_seq, WA // w),
        in_specs=[pl.BlockSpec((seq_len, w), lambda b, h: (b, OFF_Q // w + h)),
                  pl.BlockSpec((seq_len, w), lambda b, h: (b, OFF_K // w + h)),
                  pl.BlockSpec((seq_len, w), lambda b, h: (b, OFF_V // w + h)),
                  pl.BlockSpec(memory_space=pl.ANY)],
        out_specs=pl.BlockSpec((seq_len, w), lambda b, h: (b, h)),
        out_shape=jax.ShapeDtypeStruct(oa.shape, oa.dtype),
        input_output_aliases={3: 0},
        compiler_params=_cparams(("parallel", "parallel")), name="ctx_attention",
    )(p, p, p, oa)


def _na_bias_table(rpb):
    nh, nr, nc = rpb.shape
    w = GRID_W
    rpb = rpb.astype(F32)
    lo = jnp.broadcast_to(rpb[..., :1], (nh, nr, w - NA_COLS))
    hi = jnp.broadcast_to(rpb[..., -1:], (nh, nr, 2 * w - (w - NA_COLS) - nc))
    ext = jnp.concatenate([lo, rpb, hi], axis=-1)
    toep = jnp.tile(ext, (1, 1, w))[..., :w * (2 * w - 1)].reshape(nh, nr, w, 2 * w - 1)[..., w - 1:]
    col = jnp.arange(w)
    cs = jnp.clip(col - NA_COLS // 2, 0, w - NA_COLS)
    valid = (col[None, :] >= cs[:, None]) & (col[None, :] < cs[:, None] + NA_COLS)
    toep = jnp.where(valid, toep, NEG_INF)
    tab = jnp.stack([toep[:, d0:d0 + NA_ROWS] for d0 in range(NA_ROWS)], axis=1)
    return tab.transpose(0, 1, 3, 2, 4).reshape(nh, NA_ROWS, w, NA_ROWS * w)


NA_ROWS_PER_STEP = 4
NA_CTX_CHUNK = 512


def _na_kernel(q_ref, k_ref, v_ref, ck_ref, cv_ref, tab_ref, o_ref, stat_ref, octx_ref, kbf_ref, vbf_ref, qm_ref, *,
               rows, n_fill):
    @pl.when(pl.program_id(0) < n_fill)
    def _():
        o_ref[...] = jnp.zeros(o_ref.shape, o_ref.dtype)

    @pl.when(pl.program_id(0) >= n_fill)
    def _():
        _na_body(q_ref, k_ref, v_ref, ck_ref, cv_ref, tab_ref, o_ref, stat_ref, octx_ref, kbf_ref, vbf_ref, qm_ref,
                 rows)


def _na_body(q_ref, k_ref, v_ref, ck_ref, cv_ref, tab_ref, o_ref, stat_ref, octx_ref, kbf_ref, vbf_ref, qm_ref, rows):
    assert HEAD_DIM_A & (HEAD_DIM_A - 1) == 0 and HEAD_DIM_A.bit_length() % 2 == 1
    scale = HEAD_DIM_A ** -0.5
    band = NA_ROWS * GRID_W
    nh = LANE // HEAD_DIM_A
    ck = ck_ref[0].astype(BF16)
    cv = cv_ref[0].astype(BF16)
    kbf_ref[...] = k_ref[...].astype(BF16)
    vbf_ref[...] = v_ref[...].astype(BF16)

    def ctx_body(c, carry):
        rsl = pl.ds(pl.multiple_of(c * NA_CTX_CHUNK, NA_CTX_CHUNK), NA_CTX_CHUNK)
        q = q_ref[rsl, :] * scale
        hm = _head_masks(q.shape)
        lane = lax.broadcasted_iota(jnp.int32, q.shape, 1)
        stat = jnp.zeros(q.shape, F32)
        out = jnp.zeros(q.shape, F32)
        for h in range(nh):
            qm = jnp.where(hm[h], q, 0.0).astype(BF16)
            qm_ref[h, rsl, :] = qm
            s = _dot_nt(qm, ck)
            m = jnp.max(s, axis=-1, keepdims=True)
            e = jnp.exp(s - m)
            l = jnp.sum(e, axis=-1, keepdims=True)
            stat = jnp.where(lane == 2 * h, m, jnp.where(lane == 2 * h + 1, l, stat))
            out = jnp.where(hm[h], _dot(e.astype(BF16), cv), out)
        stat_ref[rsl, :] = stat
        octx_ref[rsl, :] = out
        return carry

    lax.fori_loop(0, q_ref.shape[0] // NA_CTX_CHUNK, ctx_body, 0)

    nr = NA_ROWS_PER_STEP

    def body(it, carry):
        hm = _head_masks((GRID_W, LANE))
        rows_in = []
        chains = []
        for j in range(nr):
            r = it * nr + j
            rs = jnp.clip(r - NA_ROWS // 2, 0, rows - NA_ROWS)
            d0 = rs - r + NA_ROWS - 1
            qsl = pl.ds(pl.multiple_of(r * GRID_W, GRID_W), GRID_W)
            bsl = pl.ds(pl.multiple_of(rs * GRID_W, GRID_W), band)
            st = stat_ref[qsl, :]
            kb, vb = kbf_ref[bsl, :], vbf_ref[bsl, :]
            rows_in.append((qsl, octx_ref[qsl, :]))
            for h in range(nh):
                chains.append(dict(q=qm_ref[h, qsl, :], k=kb, v=vb, tab=tab_ref[h, d0],
                                   m_c=st[:, 2 * h:2 * h + 1], l_c=st[:, 2 * h + 1:2 * h + 2]))
        s = [_dot_nt(c["q"], c["k"]) + c["tab"] for c in chains]
        m = [jnp.maximum(jnp.max(si, axis=-1, keepdims=True), c["m_c"]) for si, c in zip(s, chains)]
        e = [jnp.exp(si - mi) for si, mi in zip(s, m)]
        a_c = [jnp.exp(c["m_c"] - mi) for c, mi in zip(chains, m)]
        inv = [1.0 / (jnp.sum(ei, axis=-1, keepdims=True) + ai * c["l_c"]) for ei, ai, c in zip(e, a_c, chains)]
        pv = [_dot(ei.astype(BF16), c["v"]) for ei, c in zip(e, chains)]
        for j, (qsl, oc) in enumerate(rows_in):
            out = jnp.zeros((GRID_W, LANE), F32)
            for h in range(nh):
                i = j * nh + h
                out = jnp.where(hm[h], (pv[i] + a_c[i] * oc) * inv[i], out)
            o_ref[qsl, :] = out.astype(o_ref.dtype)
        return carry

    lax.fori_loop(0, rows // nr, body, 0)


def _nbr_attention(p, ck, cv, tab, n_ctx, n_seq, seq_len, layer, depth):
    nb = WA // LANE
    past = ck.shape[1]
    nf = n_ctx // seq_len
    hp = LANE // HEAD_DIM_A
    tok = lambda off: pl.BlockSpec((seq_len, LANE), lambda b, h: (jnp.maximum(b, nf), off // LANE + h))
    cache = pl.BlockSpec((1, past, LANE), lambda b, h: (jnp.maximum(b - nf, 0) * depth + layer, 0, h))
    return pl.pallas_call(
        functools.partial(_na_kernel, rows=seq_len // GRID_W, n_fill=nf), grid=(nf + n_seq, nb),
        in_specs=[tok(OFF_Q), tok(OFF_K), tok(OFF_V), cache, cache,
                  pl.BlockSpec((hp, NA_ROWS, GRID_W, NA_ROWS * GRID_W), lambda b, h: (h, 0, 0, 0))],
        out_specs=pl.BlockSpec((seq_len, LANE), lambda b, h: (b, h)),
        out_shape=jax.ShapeDtypeStruct((p.shape[0], WA), BF16),
        scratch_shapes=[pltpu.VMEM((seq_len, LANE), F32), pltpu.VMEM((seq_len, LANE), F32),
                        pltpu.VMEM((seq_len, LANE), BF16), pltpu.VMEM((seq_len, LANE), BF16),
                        pltpu.VMEM((hp, seq_len, LANE), BF16)],
        compiler_params=_cparams(("parallel", "parallel")), name="nbr_attention",
    )(p, p, p, ck, cv, tab)


def _seq_pos(shape, row0, seq_len):
    return (row0 + lax.broadcasted_iota(jnp.int32, shape, 0)) & (seq_len - 1)


def _shifted(x, t, d, seq_len):
    n = x.shape[0]
    y = pltpu.roll(x, (-d) % n, 0)
    return jnp.where((t + d >= 0) & (t + d < seq_len), y, 0.0)


def _conv_kernel(x_ref, w_ref, o_ref, *, n_ctx, ctx_len, lat_len):
    rb = x_ref.shape[0]
    row0 = pl.program_id(0) * rb
    seq_len = jnp.where(row0 < n_ctx, ctx_len, lat_len)
    x = x_ref[...]
    t = _seq_pos(x.shape, row0, seq_len)
    w = w_ref[...]
    o_ref[...] = (_shifted(x, t, -1, seq_len) * w[0:1] + x * w[1:2] + _shifted(x, t, 1, seq_len) * w[2:3])


def _short_conv(p, conv_w, n_ctx, ctx_len, lat_len):
    n = p.shape[0]
    rb = lat_len
    c = conv_w.shape[1]
    w8 = jnp.zeros((8, c), F32).at[:3].set(conv_w)
    return pl.pallas_call(
        functools.partial(_conv_kernel, n_ctx=n_ctx, ctx_len=ctx_len, lat_len=lat_len),
        grid=(n // rb, c // LANE),
        in_specs=[pl.BlockSpec((rb, LANE), lambda i, j: (i, OFF_RKV // LANE + j)),
                  pl.BlockSpec((8, LANE), lambda i, j: (0, j))],
        out_specs=pl.BlockSpec((rb, LANE), lambda i, j: (i, j)),
        out_shape=jax.ShapeDtypeStruct((n, c), F32),
        compiler_params=_cparams(("parallel", "parallel")), name="short_conv",
    )(p, w8)


def _pool_kernel(u0_ref, u1_ref, u2_ref, u3_ref, w_ref, sc_ref, o_ref, *, n_ctx, ctx_len, lat_len):
    rb = o_ref.shape[0]
    row0 = pl.program_id(0) * rb
    seq_len = jnp.where(row0 < n_ctx, ctx_len, lat_len)
    t = _seq_pos((rb, POOL_GROUP_DIM), row0, seq_len)
    for gi, (win, u_ref) in enumerate(zip(POOL_WINDOWS, (u0_ref, u1_ref, u2_ref, u3_ref))):
        sl = slice(gi * POOL_GROUP_DIM, (gi + 1) * POOL_GROUP_DIM)
        u = u_ref[...]
        acc = u
        for d in range(-(win // 2), win - win // 2):
            if d != 0:
                acc = acc + _shifted(u, t, d, seq_len)
        lo = jnp.maximum(t - win // 2, 0)
        hi = jnp.minimum(t + win - win // 2, seq_len)
        pooled = acc / (hi - lo).astype(F32) - u
        y = _dot(pooled.astype(BF16), w_ref[gi].astype(BF16))
        o_ref[:, sl] = (y * sc_ref[:, sl]).astype(o_ref.dtype)


def _multiscale_pool(p, pool_w, pool_scale, n_ctx, ctx_len, lat_len):
    n = p.shape[0]
    rb = lat_len
    gd = POOL_GROUP_DIM
    group = lambda gi: pl.BlockSpec((rb, gd), lambda i: (i, OFF_POOL // gd + gi))
    return pl.pallas_call(
        functools.partial(_pool_kernel, n_ctx=n_ctx, ctx_len=ctx_len, lat_len=lat_len),
        grid=(n // rb,),
        in_specs=[group(0), group(1), group(2), group(3),
                  pl.BlockSpec(pool_w.shape, lambda i: (0, 0, 0)),
                  pl.BlockSpec((1, WC), lambda i: (0, 0))],
        out_specs=pl.BlockSpec((rb, WC), lambda i: (i, 0)),
        out_shape=jax.ShapeDtypeStruct((n, WC), BF16),
        compiler_params=_cparams(("parallel",)), name="multiscale_pool",
    )(p, p, p, p, pool_w, pool_scale.reshape(1, WC))


def _head_ones():
    r = jnp.arange(LANE) // HEAD_DIM_B
    return (r[:, None] == r[None, :]).astype(F32)


def _rwkv_prep_kernel(r_ref, k_ref, v_ref, xwa_ref, xg0_ref, xg1_ref, wl_ref, g2a_ref, g2b_ref, pt_ref,
                      ones_ref, kk_o, lw_o, b_o, kd_o, g_o, bonus_o):
    r, k, v = r_ref[...], k_ref[...], v_ref[...]
    xwa = xwa_ref[...]
    lane = lax.broadcasted_iota(jnp.int32, xwa.shape, 1)
    act = jnp.where(lane < 2 * DECAY_LORA, jnp.tanh(xwa), xwa)
    lora = _dot(act.astype(BF16), wl_ref[...].astype(BF16))
    pt = pt_ref[...]
    ones = ones_ref[...]
    kkr = k * pt[4:5]
    kk = kkr * lax.rsqrt(_dot(kkr * kkr, ones, precision=HI) + 1e-12)
    kk_o[...] = kk
    for d in range(2):
        z = -(pt[d:d + 1] + lora[:, d * LANE:(d + 1) * LANE])
        softplus = jnp.maximum(z, 0.0) + jnp.log(1.0 + jnp.exp(-jnp.abs(z)))
        lw_o[d] = -jnp.exp(-softplus - 0.5)
        a = _sigmoid(pt[2 + d:3 + d] + lora[:, (2 + d) * LANE:(3 + d) * LANE])
        kd_o[d] = k * (1.0 + (a - 1.0) * pt[5:6])
        b_o[d] = kk * a
    g_o[...] = (_dot(_sigmoid(xg0_ref[...]).astype(BF16), g2a_ref[...].astype(BF16))
                + _dot(_sigmoid(xg1_ref[...]).astype(BF16), g2b_ref[...].astype(BF16)))
    bonus_o[...] = _dot(r * k * pt[6:7], ones, precision=HI) * v


def _rwkv_prep(p, rkv, w0, w2, a0, a2, g2, k_k, k_a, r_k, rb):
    n = p.shape[0]
    nhp = WB // LANE
    wl = jnp.zeros((4, DECAY_LORA, 4, WB), F32)
    wl = wl.at[0, :, 0].set(w2[0]).at[1, :, 1].set(w2[1]).at[2, :, 2].set(a2[0]).at[3, :, 3].set(a2[1])
    wl = wl.reshape(4 * DECAY_LORA, 4, nhp, LANE).transpose(0, 2, 1, 3).reshape(4 * DECAY_LORA, nhp * 4 * LANE)
    pt = jnp.stack([w0[0], w0[1], a0[0], a0[1], k_k, k_a, r_k, jnp.zeros_like(k_k)]).astype(F32)
    nlo = 4 * DECAY_LORA
    col = lambda off: (lambda i, h: (i, off + h))
    tok = jax.ShapeDtypeStruct((n, WB), F32)
    tok2 = jax.ShapeDtypeStruct((2, n, WB), F32)
    spec1 = pl.BlockSpec((rb, LANE), lambda i, h: (i, h))
    spec2 = pl.BlockSpec((2, rb, LANE), lambda i, h: (0, i, h))
    return pl.pallas_call(
        _rwkv_prep_kernel, grid=(n // rb, nhp),
        in_specs=[pl.BlockSpec((rb, LANE), col(0)),
                  pl.BlockSpec((rb, LANE), col(nhp)),
                  pl.BlockSpec((rb, LANE), col(2 * nhp)),
                  pl.BlockSpec((rb, nlo), lambda i, h: (i, OFF_WL // nlo)),
                  pl.BlockSpec((rb, LANE), lambda i, h: (i, OFF_GL // LANE)),
                  pl.BlockSpec((rb, LANE), lambda i, h: (i, OFF_GL // LANE + 1)),
                  pl.BlockSpec((nlo, 4 * LANE), lambda i, h: (0, h)),
                  pl.BlockSpec((LANE, LANE), lambda i, h: (0, h)),
                  pl.BlockSpec((LANE, LANE), lambda i, h: (1, h)),
                  pl.BlockSpec((8, LANE), lambda i, h: (0, h)),
                  pl.BlockSpec((LANE, LANE), lambda i, h: (0, 0))],
        out_specs=[spec1, spec2, spec2, spec2, spec1, spec1],
        out_shape=[tok, tok2, tok2, tok2, tok, tok],
        compiler_params=_cparams(("parallel", "parallel")), name="rwkv_prep",
    )(rkv, rkv, rkv, p, p, p, wl, g2, g2, pt, _head_ones())


def _split_bf16(a):
    hi = pltpu.bitcast(pltpu.bitcast(a, jnp.uint32) & jnp.uint32(0xFFFF0000), F32)
    return hi.astype(BF16), (a - hi).astype(BF16)


_NN = (((1,), (0,)), ((), ()))
_NT = (((1,), (1,)), ((), ()))
_TN = (((0,), (0,)), ((), ()))


def _mm(a, b, passes, dims=_NN):
    if passes == 6:
        return lax.dot_general(a, b, dims, preferred_element_type=F32, precision=HI)
    dg = lambda p, q: lax.dot_general(p, q, dims, preferred_element_type=F32)
    if passes == 1:
        return dg(a.astype(BF16), b.astype(BF16))
    ah, al = _split_bf16(a)
    bh, bl = _split_bf16(b)
    return dg(ah, bh) + (dg(ah, bl) + dg(al, bh))


def _scan_kernel(rf_ref, vf_ref, kkf_ref, rb_ref, vb_ref, kkb_ref, lwf_ref, bf_ref, kdf_ref, lwb_ref, bb_ref,
                 kdb_ref, s0_ref, of_ref, ob_ref, se_ref, s_ref, *, ctx_steps, ctx_chunks, lat_chunks, passes):
    p_gram, p_inv, p_state = passes
    c = SCAN_CHUNK
    hd = HEAD_DIM_B
    j = pl.program_id(0)
    is_lat = j >= ctx_steps
    ci = jnp.where(is_lat, (j - ctx_steps) & (lat_chunks - 1), j & (ctx_chunks - 1))
    last = jnp.where(is_lat, lat_chunks, ctx_chunks) - 1

    @pl.when((ci == 0) & jnp.logical_not(is_lat))
    def _():
        s_ref[...] = jnp.zeros(s_ref.shape, F32)

    @pl.when((ci == 0) & is_lat)
    def _():
        s_ref[...] = s0_ref[:, 0]

    row = lax.broadcasted_iota(jnp.int32, (c, c), 0)
    col = lax.broadcasted_iota(jnp.int32, (c, c), 1)
    row2 = lax.broadcasted_iota(jnp.int32, (c, 2 * c), 0)
    col2 = lax.broadcasted_iota(jnp.int32, (c, 2 * c), 1) & (c - 1)
    eye = (col == row).astype(F32)
    sls = [slice(h * hd, (h + 1) * hd) for h in range(N_HEADS_B)]

    z, y, ye, vh, g_end, strict, incl2, sidx = [], [], [], [], [], [], [], []
    for d, (r_ref, v_ref, kk_ref, lw_ref, b_ref, kd_ref) in enumerate(
            ((rf_ref, vf_ref, kkf_ref, lwf_ref, bf_ref, kdf_ref), (rb_ref, vb_ref, kkb_ref, lwb_ref, bb_ref, kdb_ref))):
        before = (col < row) if d == 0 else (col > row)
        before2 = (col2 <= row2) if d == 0 else (col2 >= row2)
        lw = lw_ref[0]
        cum = _dot((before | (col == row)).astype(F32), lw, precision=HI)
        tot = jnp.sum(lw, axis=0, keepdims=True)
        r, v, kk, b, kd = r_ref[...], v_ref[...], kk_ref[...], b_ref[0], kd_ref[0]
        e_neg = jnp.exp(-cum)
        e_end = jnp.exp(tot - cum)
        kk_h = kk * jnp.exp(cum - lw)
        r_h = r * jnp.exp(cum)
        b_t, k_t = b * e_neg, kd * e_neg
        b_e, k_e = b * e_end, kd * e_end
        ge = jnp.exp(tot)
        for h, sl in enumerate(sls):
            z.append(jnp.concatenate([kk_h[:, sl], r_h[:, sl]], axis=0))
            y.append(jnp.concatenate([b_t[:, sl], k_t[:, sl]], axis=0))
            ye.append(jnp.concatenate([b_e[:, sl], k_e[:, sl]], axis=0))
            vh.append(v[:, sl])
            g_end.append(ge[:, sl])
            strict.append(before)
            incl2.append(before2)
            sidx.append((d, h))
    n = range(len(z))
    g = [_mm(z[i], y[i], p_gram, _NT) for i in n]
    x = [jnp.where(strict[i], -g[i][:c, :c], 0.0) for i in n]
    m_k = [jnp.where(strict[i], g[i][:c, c:], 0.0) for i in n]
    p_bk = [jnp.where(incl2[i], g[i][c:], 0.0) for i in n]
    tinv = [eye + x[i] for i in n]
    for _ in range(c.bit_length() - 2):
        x = [_mm(x[i], x[i], p_inv) for i in n]
        tinv = [tinv[i] + _mm(tinv[i], x[i], p_inv) for i in n]
    s = [s_ref[sidx[i]] for i in n]
    zs = [_mm(z[i], s[i], p_state, _NT) for i in n]
    mkv = [_mm(m_k[i], vh[i], p_state) for i in n]
    u = [-_mm(tinv[i], zs[i][:c] + mkv[i], p_state) for i in n]
    w = [jnp.concatenate([u[i], vh[i]], axis=0) for i in n]
    o = [zs[i][c:] + _mm(p_bk[i], w[i], p_state) for i in n]
    of_ref[...] = jnp.concatenate(o[:N_HEADS_B], axis=-1)
    ob_ref[...] = jnp.concatenate(o[N_HEADS_B:], axis=-1)
    for i in n:
        s_ref[sidx[i]] = s[i] * g_end[i] + _mm(w[i], ye[i], p_state, _TN)

    @pl.when(ci == last)
    def _():
        se_ref[:, 0] = s_ref[...]


def _rwkv_scan(rkv, kk, lw, b, kd, s0, n_cs, ctx_len, n_ls, lat_len, passes=(1, 1, 1)):
    c = SCAN_CHUNK
    cc, lc = ctx_len // c, lat_len // c
    assert cc & (cc - 1) == 0 and lc & (lc - 1) == 0
    ctx_steps = n_cs * cc
    n_steps = ctx_steps + n_ls * lc
    nb = WB // WB
    n = kk.shape[0]

    def where(j):
        jl = jnp.maximum(j - ctx_steps, 0)
        is_lat = j >= ctx_steps
        seq = jnp.where(is_lat, n_cs + jl // lc, j // cc)
        start = jnp.where(is_lat, ctx_steps + (jl // lc) * lc, (j // cc) * cc)
        ch = jnp.where(is_lat, jl % lc, j % cc)
        return seq, start, ch, jnp.where(is_lat, lc, cc)

    def rb(d, j):
        _, start, ch, nch = where(j)
        return start + ch + d * (nch - 1 - 2 * ch)

    tok = lambda d, colblk: pl.BlockSpec((c, WB), lambda j: (rb(d, j), colblk))
    tok2 = lambda d: pl.BlockSpec((1, c, WB), lambda j: (d, rb(d, j), 0))
    st_shape = (2, 1, N_HEADS_B, HEAD_DIM_B, HEAD_DIM_B)
    return pl.pallas_call(
        functools.partial(_scan_kernel, ctx_steps=ctx_steps, ctx_chunks=cc, lat_chunks=lc, passes=passes),
        grid=(n_steps,),
        in_specs=[tok(0, 0), tok(0, 2 * nb), tok(0, 0), tok(1, 0), tok(1, 2 * nb), tok(1, 0),
                  tok2(0), tok2(0), tok2(0), tok2(1), tok2(1), tok2(1),
                  pl.BlockSpec(st_shape, lambda j: (0, jnp.maximum(where(j)[0] - n_cs, 0), 0, 0, 0))],
        out_specs=[tok(0, 0), tok(1, 0), pl.BlockSpec(st_shape, lambda j: (0, where(j)[0], 0, 0, 0))],
        out_shape=[jax.ShapeDtypeStruct((n, WB), F32), jax.ShapeDtypeStruct((n, WB), F32),
                   jax.ShapeDtypeStruct((2, n_cs + n_ls, N_HEADS_B, HEAD_DIM_B, HEAD_DIM_B), F32)],
        scratch_shapes=[pltpu.VMEM((2, N_HEADS_B, HEAD_DIM_B, HEAD_DIM_B), F32)],
        compiler_params=_cparams(("arbitrary",)), name="rwkv_scan",
    )(rkv, rkv, kk, rkv, rkv, kk, lw, b, kd, lw, b, kd, s0)


def _rwkv_post_kernel(of_ref, ob_ref, bonus_ref, g_ref, lng_ref, lnb_ref, ones_ref, y_ref):
    o = of_ref[...] + ob_ref[...]
    mean_mat = ones_ref[...] * (1.0 / HEAD_DIM_B)
    mu = _dot(o, mean_mat, precision=HI)
    var = _dot(jnp.square(o - mu), mean_mat, precision=HI)
    y = (o - mu) * lax.rsqrt(var + GN_EPS) * lng_ref[...] + lnb_ref[...]
    y_ref[...] = ((y + bonus_ref[...]) * g_ref[...]).astype(y_ref.dtype)


def _rwkv_post(o_f, o_b, bonus, g, ln_g, ln_b, rb):
    n = bonus.shape[0]
    nhp = WB // LANE
    spec = pl.BlockSpec((rb, LANE), lambda i, h: (i, h))
    vec = pl.BlockSpec((1, LANE), lambda i, h: (0, h))
    return pl.pallas_call(
        _rwkv_post_kernel, grid=(n // rb, nhp),
        in_specs=[spec, spec, spec, spec, vec, vec, pl.BlockSpec((LANE, LANE), lambda i, h: (0, 0))],
        out_specs=spec,
        out_shape=jax.ShapeDtypeStruct((n, WB), BF16),
        compiler_params=_cparams(("parallel", "parallel")), name="rwkv_post",
    )(o_f, o_b, bonus, g, ln_g.reshape(1, WB), ln_b.reshape(1, WB), _head_ones())


def _outproj_kernel(a_ref, b_ref, c_ref, w_ref, x_ref, gt_ref, o_ref):
    y = (_dot(a_ref[...], w_ref[0:WA])
         + _dot(b_ref[...], w_ref[WA:WA + WB])
         + _dot(c_ref[...], w_ref[WA + WB:]))
    o_ref[...] = x_ref[...] + gt_ref[0] * y


def _out_proj(oa, ob, oc, w, x, gt, n_ctx, lat_len, tm, tn):
    n, d = x.shape
    grp = lambda i, j: (_group_of_rows(i * tm, n_ctx, lat_len), 0, j)
    return pl.pallas_call(
        _outproj_kernel, grid=(n // tm, d // tn),
        in_specs=[pl.BlockSpec((tm, WA), lambda i, j: (i, 0)),
                  pl.BlockSpec((tm, WB), lambda i, j: (i, 0)),
                  pl.BlockSpec((tm, WC), lambda i, j: (i, 0)),
                  pl.BlockSpec((d, tn), lambda i, j: (0, j)),
                  pl.BlockSpec((tm, tn), lambda i, j: (i, j)),
                  pl.BlockSpec((1, 1, tn), grp)],
        out_specs=pl.BlockSpec((tm, tn), lambda i, j: (i, j)),
        out_shape=jax.ShapeDtypeStruct((n, d), F32),
        compiler_params=_cparams(("parallel", "parallel")), name="out_proj",
    )(oa, ob, oc, w, x, gt)


def _swiglu_kernel(te_ref, nv_ref, h_ref, w1_ref, w3_ref, w2_ref, o_ref):
    tm = o_ref.shape[0]
    nv = nv_ref[pl.program_id(0)]

    @pl.when(pl.program_id(1) == 0)
    def _():
        o_ref[...] = jnp.zeros(o_ref.shape, F32)

    def accumulate(rows):
        h = h_ref[0:rows, :]
        a = _dot(h, w1_ref[0].astype(BF16))
        b = _dot(h, w3_ref[0].astype(BF16))
        act = (a * _sigmoid(a) * b).astype(BF16)
        o_ref[0:rows, :] += _dot(act, w2_ref[0].astype(BF16))

    @pl.when(nv > tm // 2)
    def _():
        accumulate(tm)

    @pl.when((nv > 0) & (nv <= tm // 2))
    def _():
        accumulate(tm // 2)


def _swiglu(h, tile_expert, tile_valid, w1, w3, w2, tm, tf, out_buffers=2):
    n, d = h.shape
    ff = w1.shape[2]
    return pl.pallas_call(
        _swiglu_kernel,
        grid_spec=pltpu.PrefetchScalarGridSpec(
            num_scalar_prefetch=2, grid=(n // tm, ff // tf),
            in_specs=[pl.BlockSpec((tm, d), lambda i, f, te, nt: (i, 0)),
                      pl.BlockSpec((1, d, tf), lambda i, f, te, nt: (te[i], 0, f)),
                      pl.BlockSpec((1, d, tf), lambda i, f, te, nt: (te[i], 0, f)),
                      pl.BlockSpec((1, tf, d), lambda i, f, te, nt: (te[i], f, 0))],
            out_specs=pl.BlockSpec((tm, d), lambda i, f, te, nt: (i, 0),
                                   pipeline_mode=pl.Buffered(out_buffers))),
        out_shape=jax.ShapeDtypeStruct((n, d), F32),
        compiler_params=_cparams(("parallel", "arbitrary")), name="swiglu",
    )(tile_expert, tile_valid, h, w1, w3, w2)


def _residual_kernel(x_ref, f_ref, gt_ref, o_ref):
    o_ref[...] = x_ref[...] + gt_ref[0] * f_ref[...]


def _residual(x, f, gt, n_ctx, lat_len, tm):
    n, d = x.shape
    row = pl.BlockSpec((tm, d), lambda i: (i, 0))
    return pl.pallas_call(
        _residual_kernel, grid=(n // tm,),
        in_specs=[row, row, pl.BlockSpec((1, 1, d), lambda i: (_group_of_rows(i * tm, n_ctx, lat_len), 0, 0))],
        out_specs=row, out_shape=jax.ShapeDtypeStruct((n, d), F32),
        compiler_params=_cparams(("parallel",)), name="residual",
    )(x, f, gt)


GATHER_UNROLL = 8


def _gather_into(idx_ref, base, src_ref, buf_ref, sem):
    rt = buf_ref.shape[0]

    def copy(r):
        return pltpu.make_async_copy(src_ref.at[pl.ds(idx_ref[base + r], 1)], buf_ref.at[pl.ds(r, 1)], sem)

    def start(r, carry):
        copy(r).start()
        return carry

    def wait(r, carry):
        copy(r).wait()
        return carry

    lax.fori_loop(0, rt, start, 0, unroll=GATHER_UNROLL)
    lax.fori_loop(0, rt, wait, 0, unroll=GATHER_UNROLL)


def _gather_rows_kernel(idx_ref, src_ref, o_ref, buf_ref, sem):
    _gather_into(idx_ref, pl.program_id(0) * o_ref.shape[0], src_ref, buf_ref, sem)
    o_ref[...] = buf_ref[...].astype(o_ref.dtype)


def _gather_rows(src, idx, rt, out_dtype):
    n_out = idx.shape[0]
    d = src.shape[1]
    return pl.pallas_call(
        _gather_rows_kernel,
        grid_spec=pltpu.PrefetchScalarGridSpec(
            num_scalar_prefetch=1, grid=(n_out // rt,),
            in_specs=[pl.BlockSpec(memory_space=pl.ANY)],
            out_specs=pl.BlockSpec((rt, d), lambda i, idx: (i, 0)),
            scratch_shapes=[pltpu.VMEM((rt, d), src.dtype), pltpu.SemaphoreType.DMA(())]),
        out_shape=jax.ShapeDtypeStruct((n_out, d), out_dtype),
        compiler_params=_cparams(("arbitrary",)), name="gather_rows",
    )(idx, src)


def _combine_kernel(pos_ref, x_ref, gates_ref, gt_ref, fg_ref, ys_ref, o_ref, y1_ref, y2_ref, sem1, sem2, *, n,
                    final_norm):
    tm = o_ref.shape[0]
    base = pl.program_id(0) * tm
    _gather_into(pos_ref, base, ys_ref, y1_ref, sem1)
    _gather_into(pos_ref, n + base, ys_ref, y2_ref, sem2)
    g = gates_ref[...]
    x = x_ref[...] + gt_ref[0] * (g[:, 0:1] * y1_ref[...] + g[:, 1:2] * y2_ref[...])
    if final_norm:
        x = (x * lax.rsqrt(jnp.mean(x * x, axis=-1, keepdims=True) + RMS_EPS)) * fg_ref[...]
    o_ref[...] = x


def _combine(x, ys, pos, gates, gt, final_g, n_ctx, lat_len, tm):
    n, d = x.shape
    row = pl.BlockSpec((tm, d), lambda i, pos: (i, 0))
    return pl.pallas_call(
        functools.partial(_combine_kernel, n=n, final_norm=final_g is not None),
        grid_spec=pltpu.PrefetchScalarGridSpec(
            num_scalar_prefetch=1, grid=(n // tm,),
            in_specs=[row, pl.BlockSpec((tm, LANE), lambda i, pos: (i, 0)),
                      pl.BlockSpec((1, 1, d), lambda i, pos: (_group_of_rows(i * tm, n_ctx, lat_len), 0, 0)),
                      pl.BlockSpec((1, d), lambda i, pos: (0, 0)),
                      pl.BlockSpec(memory_space=pl.ANY)],
            out_specs=row,
            scratch_shapes=[pltpu.VMEM((tm, d), F32), pltpu.VMEM((tm, d), F32),
                            pltpu.SemaphoreType.DMA(()), pltpu.SemaphoreType.DMA(())]),
        out_shape=jax.ShapeDtypeStruct((n, d), F32),
        compiler_params=_cparams(("arbitrary",)), name="moe_combine",
    )(pos, x, gates, gt, (jnp.ones((d,), F32) if final_g is None else final_g).reshape(1, d), ys)


def _moe(x, h, gates, idx, w1, w3, w2, gt, final_g, n_ctx, lat_len, tm, tf, rt):
    n, d = x.shape
    n_e = w1.shape[0]
    e = jnp.concatenate([idx[:, 0], idx[:, 1]])
    onehot = (e[:, None] == jnp.arange(n_e)[None, :]).astype(jnp.int32)
    rank = jnp.take_along_axis(jnp.cumsum(onehot, axis=0), e[:, None], axis=1)[:, 0] - 1
    counts = jnp.sum(onehot, axis=0)
    tiles = (counts + tm - 1) // tm
    tile_end = jnp.cumsum(tiles)
    start = (tile_end - tiles) * tm
    pos = (start[e] + rank).astype(jnp.int32)
    n_rows = 2 * n + n_e * tm
    src = jnp.zeros((n_rows,), jnp.int32).at[pos].set(jnp.tile(jnp.arange(n, dtype=jnp.int32), 2))
    tile_ids = jnp.arange(n_rows // tm)
    tile_expert = jnp.minimum(jnp.sum(tile_ids[:, None] >= tile_end[None, :], axis=1), n_e - 1).astype(jnp.int32)
    first_tile = (tile_end - tiles)[tile_expert]
    tile_valid = jnp.clip(counts[tile_expert] - (tile_ids - first_tile) * tm, 0, tm)
    tile_valid = jnp.where(tile_ids < tile_end[-1], tile_valid, 0).astype(jnp.int32)

    hs = _gather_rows(h, src, rt, BF16)
    ys = _swiglu(hs, tile_expert, tile_valid, w1, w3, w2, tm, tf, out_buffers=1)
    return _combine(x, ys, pos, gates, gt, final_g, n_ctx, lat_len, rt)


def _kv_kernel(*refs, depth):
    k_refs, v_refs, (ok_ref, ov_ref) = refs[:depth], refs[depth:2 * depth], refs[2 * depth:]
    for l in range(depth):
        @pl.when(pl.program_id(0) == l)
        def _():
            ok_ref[0, 0] = k_refs[l][...]
            ov_ref[0, 0] = v_refs[l][...]


def _kv_outputs(p_layers, n_seq, seq_len):
    depth = len(p_layers)

    def col(l, off):
        return pl.BlockSpec((seq_len, WA), lambda ll, b: (jnp.where(ll == l, b, 0), off // WA))

    out = pl.BlockSpec((1, 1, seq_len, WA), lambda ll, b: (b, ll, 0, 0))
    shape = jax.ShapeDtypeStruct((n_seq, depth, seq_len, WA), F32)
    return pl.pallas_call(
        functools.partial(_kv_kernel, depth=depth), grid=(depth, n_seq),
        in_specs=[col(l, OFF_K) for l in range(depth)] + [col(l, OFF_V) for l in range(depth)],
        out_specs=[out, out], out_shape=[shape, shape],
        compiler_params=_cparams(("parallel", "parallel")), name="kv_outputs",
    )(*p_layers, *p_layers)


def _forward(x_prompt, x_sample, cache_attn_k, cache_attn_v, state_rwkv_fwd, state_rwkv_bwd, c, c_ctx,
             norm1_g, norm2_g, ada_w, ada_b, w_in, w_out, na_rpb, rw_conv, rw_w0, rw_w2, rw_a0, rw_a2,
             rw_g2, rw_kk, rw_ka, rw_rk, rw_ln_g, rw_ln_b, pool_w, pool_scale, ffn_w1, ffn_w3, ffn_w2,
             moe_router, moe_w1, moe_w3, moe_w2, final_g, *, tm, tm_proj, tn_in, tn_out, tm_dense, tf_dense,
             tm_moe, tf_moe, rt, rb):
    n_cs, ctx_len, d = x_prompt.shape
    n_ls, lat_len, _ = x_sample.shape
    n_ctx = n_cs * ctx_len
    n_lat = n_ls * lat_len
    n = n_ctx + n_lat
    depth = w_in.shape[0]
    past = cache_attn_k.shape[2]
    assert ctx_len & (ctx_len - 1) == 0 and lat_len & (lat_len - 1) == 0
    assert lat_len % ctx_len == 0 and n_ctx % lat_len == 0 and lat_len % GRID_W == 0

    x = jnp.concatenate([x_prompt.reshape(n_ctx, d), x_sample.reshape(n_lat, d)], axis=0)
    cond8 = jnp.zeros((8, d), F32).at[0].set(c_ctx).at[1:1 + n_ls].set(c)
    mods = _ada(cond8, ada_w, ada_b).reshape(depth, 8, 6, 1, d)

    ck = cache_attn_k.reshape(n_ls * depth, past, WA)
    cv = cache_attn_v.reshape(n_ls * depth, past, WA)
    p_layers, sf_new, sb_new = [], [], []
    dense_tiles = jnp.zeros((n // tm_dense,), jnp.int32)
    dense_valid = jnp.full((n // tm_dense,), tm_dense, jnp.int32)
    for l in range(depth):
        m = [mods[l, :1 + n_ls, k] for k in range(6)]
        p = _in_proj(x, norm1_g[l], m[1], m[0], w_in[l].astype(BF16), n_ctx, lat_len, tm_proj, tn_in)

        oa = _nbr_attention(p, ck, cv, _na_bias_table(na_rpb[l]), n_ctx, n_ls, lat_len, l, depth)
        oa = _ctx_attention(p, oa, n_cs, ctx_len)

        rkv = _short_conv(p, rw_conv[l], n_ctx, ctx_len, lat_len)
        kk, lw, b, kd, g, bonus = _rwkv_prep(p, rkv, rw_w0[l], rw_w2[l], rw_a0[l], rw_a2[l], rw_g2[l],
                                             rw_kk[l], rw_ka[l], rw_rk[l], rb)
        s0 = jnp.stack([state_rwkv_fwd[:, l], state_rwkv_bwd[:, l]]).astype(F32)
        o_f, o_b, s_end = _rwkv_scan(rkv, kk, lw, b, kd, s0, n_cs, ctx_len, n_ls, lat_len)
        ob = _rwkv_post(o_f, o_b, bonus, g, rw_ln_g[l], rw_ln_b[l], rb)

        oc = _multiscale_pool(p, pool_w[l], pool_scale[l], n_ctx, ctx_len, lat_len)
        x = _out_proj(oa, ob, oc, w_out[l].astype(BF16), x, m[2], n_ctx, lat_len, tm_proj, tn_out)

        p_layers.append(p)
        sf_new.append(s_end[0, :n_cs])
        sb_new.append(s_end[1, :n_cs])

        i = l // 2
        if l % 2 == 0:
            h = _norm_mod(x, norm2_g[l], m[4], m[3], n_ctx, lat_len, tm)
            f = _swiglu(h, dense_tiles, dense_valid, ffn_w1[i:i + 1].astype(BF16),
                        ffn_w3[i:i + 1].astype(BF16), ffn_w2[i:i + 1].astype(BF16), tm_dense, tf_dense)
            x = _residual(x, f, m[5], n_ctx, lat_len, tm)
        else:
            h, gates, idx = _norm_mod(x, norm2_g[l], m[4], m[3], n_ctx, lat_len, tm, router=moe_router[i])
            fg = final_g if l == depth - 1 else None
            x = _moe(x, h, gates, idx, moe_w1[i], moe_w3[i], moe_w2[i], m[5], fg, n_ctx, lat_len, tm_moe, tf_moe, rt)

    y = x if depth % 2 == 0 else _final_norm(x, final_g, tm)
    k_new, v_new = _kv_outputs(p_layers, n_cs, ctx_len)
    return (y[:n_ctx].reshape(n_cs, ctx_len, d), y[n_ctx:].reshape(n_ls, lat_len, d),
            k_new.reshape(n_cs, depth, ctx_len, N_HEADS_A, HEAD_DIM_A),
            v_new.reshape(n_cs, depth, ctx_len, N_HEADS_A, HEAD_DIM_A),
            jnp.stack(sf_new, axis=1), jnp.stack(sb_new, axis=1))


def kernel(x_prompt, x_sample, cache_attn_k, cache_attn_v, state_rwkv_fwd, state_rwkv_bwd, c, c_ctx, norm1_g, norm2_g, ada_w, ada_b, w_in, w_out, na_rpb, rw_conv, rw_w0, rw_w2, rw_a0, rw_a2, rw_g2, rw_kk, rw_ka, rw_rk, rw_ln_g, rw_ln_b, pool_w, pool_scale, ffn_w1, ffn_w3, ffn_w2, moe_router, moe_w1, moe_w3, moe_w2, final_g):
    return _forward(x_prompt, x_sample, cache_attn_k, cache_attn_v, state_rwkv_fwd, state_rwkv_bwd, c, c_ctx,
                    norm1_g, norm2_g, ada_w, ada_b, w_in, w_out, na_rpb, rw_conv, rw_w0, rw_w2, rw_a0, rw_a2,
                    rw_g2, rw_kk, rw_ka, rw_rk, rw_ln_g, rw_ln_b, pool_w, pool_scale, ffn_w1, ffn_w3, ffn_w2,
                    moe_router, moe_w1, moe_w3, moe_w2, final_g,
                    tm=512, tm_proj=1024, tn_in=1152, tn_out=1024, tm_dense=1024, tf_dense=512,
                    tm_moe=1024, tf_moe=512, rt=256, rb=512)
```

```python
import functools

import jax
import jax.numpy as jnp
from jax import lax
from jax.experimental import pallas as pl
from jax.experimental.pallas import tpu as pltpu

F32 = jnp.float32
BF16 = jnp.bfloat16
HI = lax.Precision.HIGHEST

D_MODEL = 2048
DEPTH = 2
GRID_W = 64
WA = D_MODEL // 2
WB = D_MODEL // 4
WC = D_MODEL - WA - WB
HEAD_DIM_A = 64
N_HEADS_A = WA // HEAD_DIM_A
HEAD_DIM_B = 64
N_HEADS_B = WB // HEAD_DIM_B
POOL_WINDOWS = (2, 4, 8, 16)
POOL_GROUP_DIM = WC // len(POOL_WINDOWS)
NA_ROWS = 8
NA_COLS = 16
DECAY_LORA = 96
AAA_LORA = 96
GATE_LORA = 256
OFF_Q = 0
OFF_K = WA
OFF_V = 2 * WA
OFF_RKV = 3 * WA
OFF_WL = OFF_RKV + 3 * WB
OFF_AL = OFF_WL + 2 * DECAY_LORA
OFF_GL = OFF_AL + 2 * AAA_LORA
OFF_POOL = OFF_GL + GATE_LORA
P_IN = OFF_POOL + WC
N_EXPERTS = 8
RMS_EPS = 1e-6
GN_EPS = 64e-5
NEG_INF = -1e30

LANE = 128
VMEM_LIMIT = 56 * 1024 * 1024
SCAN_CHUNK = 64
NORM_ROWS = 256


def _cparams(sem):
    return pltpu.CompilerParams(dimension_semantics=sem, vmem_limit_bytes=VMEM_LIMIT)


def _dot(a, b, precision=None):
    return jnp.dot(a, b, preferred_element_type=F32, precision=precision)


def _dot_nt(a, b, precision=None):
    return lax.dot_general(a, b, (((1,), (1,)), ((), ())), preferred_element_type=F32, precision=precision)


def _dot_tn(a, b, precision=None):
    return lax.dot_general(a, b, (((0,), (0,)), ((), ())), preferred_element_type=F32, precision=precision)


def _sigmoid(x):
    return 1.0 / (1.0 + jnp.exp(-x))


def _ada_kernel(c_ref, w_ref, b_ref, o_ref):
    c = c_ref[...]
    s = c * _sigmoid(c)
    o_ref[0] = _dot(s.astype(BF16), w_ref[0].astype(BF16)) + b_ref[0]


def _ada(cond8, ada_w, ada_b):
    depth, d, n6 = ada_w.shape
    tn = 768
    return pl.pallas_call(
        _ada_kernel,
        grid=(depth, n6 // tn),
        in_specs=[pl.BlockSpec((8, d), lambda l, j: (0, 0)),
                  pl.BlockSpec((1, d, tn), lambda l, j: (l, 0, j)),
                  pl.BlockSpec((1, 1, tn), lambda l, j: (l, 0, j))],
        out_specs=pl.BlockSpec((1, 8, tn), lambda l, j: (l, 0, j)),
        out_shape=jax.ShapeDtypeStruct((depth, 8, n6), F32),
        compiler_params=_cparams(("parallel", "parallel")),
        name="ada_mod",
    )(cond8, ada_w, ada_b.reshape(depth, 1, n6))


def _modulated(x, g, sc, sh):
    y = x * lax.rsqrt(jnp.mean(x * x, axis=-1, keepdims=True) + RMS_EPS)
    return (y * g) * (1.0 + sc) + sh


def _group_of_rows(row0, n_ctx, lat_len):
    return jnp.where(row0 < n_ctx, 0, 1 + (row0 - n_ctx) // lat_len)


def _inproj_kernel(x_ref, g_ref, sc_ref, sh_ref, w_ref, o_ref, h_ref):
    @pl.when(pl.program_id(1) == 0)
    def _():
        sub = NORM_ROWS

        def body(k, carry):
            rows = pl.ds(pl.multiple_of(k * sub, sub), sub)
            h_ref[rows, :] = _modulated(x_ref[rows, :], g_ref[...], sc_ref[0], sh_ref[0]).astype(BF16)
            return carry

        lax.fori_loop(0, x_ref.shape[0] // sub, body, 0)

    o_ref[...] = _dot(h_ref[...], w_ref[...])


def _in_proj(x, g, sc, sh, w, n_ctx, lat_len, tm, tn):
    n, d = x.shape
    pin = w.shape[1]
    grp = lambda i, j: (_group_of_rows(i * tm, n_ctx, lat_len), 0, 0)
    return pl.pallas_call(
        _inproj_kernel,
        grid=(n // tm, pin // tn),
        in_specs=[pl.BlockSpec((tm, d), lambda i, j: (i, 0)),
                  pl.BlockSpec((1, d), lambda i, j: (0, 0)),
                  pl.BlockSpec((1, 1, d), grp),
                  pl.BlockSpec((1, 1, d), grp),
                  pl.BlockSpec((d, tn), lambda i, j: (0, j))],
        out_specs=pl.BlockSpec((tm, tn), lambda i, j: (i, j)),
        out_shape=jax.ShapeDtypeStruct((n, pin), F32),
        scratch_shapes=[pltpu.VMEM((tm, d), BF16)],
        compiler_params=_cparams(("parallel", "arbitrary")),
        name="in_proj",
    )(x, g.reshape(1, d), sc, sh, w)


def _normmod_router_kernel(x_ref, g_ref, sc_ref, sh_ref, rt_ref, h_ref, gate_ref, idx_ref):
    h = _modulated(x_ref[...], g_ref[...], sc_ref[0], sh_ref[0])
    h_ref[...] = h
    logits = _dot(h, rt_ref[...], precision=HI)
    lane = lax.broadcasted_iota(jnp.int32, logits.shape, 1)
    lanef = lane.astype(F32)
    logits = jnp.where(lane < N_EXPERTS, logits, -jnp.inf)
    m1 = jnp.max(logits, axis=-1, keepdims=True)
    i1 = jnp.min(jnp.where(logits == m1, lanef, float(LANE)), axis=-1, keepdims=True)
    rest = jnp.where(lanef == i1, -jnp.inf, logits)
    m2 = jnp.max(rest, axis=-1, keepdims=True)
    i2 = jnp.min(jnp.where(rest == m2, lanef, float(LANE)), axis=-1, keepdims=True)
    e2 = jnp.exp(m2 - m1)
    p1 = 1.0 / (1.0 + e2)
    p2 = e2 / (1.0 + e2)
    gate_ref[...] = jnp.where(lane == 0, p1, jnp.where(lane == 1, p2, 0.0))
    idx_ref[...] = jnp.where(lane == 0, i1, jnp.where(lane == 1, i2, 0.0)).astype(jnp.int32)


def _norm_mod_router(x, g, sc, sh, router, n_ctx, lat_len, tm):
    n, d = x.shape
    grp = lambda i: (_group_of_rows(i * tm, n_ctx, lat_len), 0, 0)
    in_specs = [pl.BlockSpec((tm, d), lambda i: (i, 0)),
                pl.BlockSpec((1, d), lambda i: (0, 0)),
                pl.BlockSpec((1, 1, d), grp),
                pl.BlockSpec((1, 1, d), grp)]
    row_spec = pl.BlockSpec((tm, d), lambda i: (i, 0))
    rt = jnp.zeros((d, LANE), F32).at[:, :N_EXPERTS].set(router)
    lane_spec = pl.BlockSpec((tm, LANE), lambda i: (i, 0))
    return pl.pallas_call(
        _normmod_router_kernel, grid=(n // tm,),
        in_specs=in_specs + [pl.BlockSpec((d, LANE), lambda i: (0, 0))],
        out_specs=[row_spec, lane_spec, lane_spec],
        out_shape=[jax.ShapeDtypeStruct((n, d), F32),
                   jax.ShapeDtypeStruct((n, LANE), F32),
                   jax.ShapeDtypeStruct((n, LANE), jnp.int32)],
        compiler_params=_cparams(("parallel",)), name="norm_mod_router",
    )(x, g.reshape(1, d), sc, sh, rt)


def _final_norm_kernel(x_ref, g_ref, o_ref):
    x = x_ref[...]
    o_ref[...] = (x * lax.rsqrt(jnp.mean(x * x, axis=-1, keepdims=True) + RMS_EPS)) * g_ref[...]


def _final_norm(x, g, tm):
    n, d = x.shape
    return pl.pallas_call(
        _final_norm_kernel, grid=(n // tm,),
        in_specs=[pl.BlockSpec((tm, d), lambda i: (i, 0)), pl.BlockSpec((1, d), lambda i: (0, 0))],
        out_specs=pl.BlockSpec((tm, d), lambda i: (i, 0)),
        out_shape=jax.ShapeDtypeStruct((n, d), F32),
        compiler_params=_cparams(("parallel",)), name="final_norm",
    )(x, g.reshape(1, d))


CTX_ATTN_LANES = 512


def _head_masks(shape):
    lane = lax.broadcasted_iota(jnp.int32, shape, 1)
    return [lane // HEAD_DIM_A == h for h in range(LANE // HEAD_DIM_A)]


def _ctx_attn_kernel(q_ref, k_ref, v_ref, oa_ref, o_ref):
    del oa_ref
    scale = HEAD_DIM_A ** -0.5
    nh = LANE // HEAD_DIM_A
    hm = _head_masks((q_ref.shape[0], LANE))
    chains = []
    for j in range(q_ref.shape[1] // LANE):
        cols = slice(j * LANE, (j + 1) * LANE)
        q, k, v = q_ref[:, cols], k_ref[:, cols].astype(BF16), v_ref[:, cols].astype(BF16)
        chains += [(jnp.where(hm[h], q, 0.0).astype(BF16), k, v) for h in range(nh)]
    s = [_dot_nt(q, k) * scale for q, k, _ in chains]
    e = [jnp.exp(si - jnp.max(si, axis=-1, keepdims=True)) for si in s]
    inv = [1.0 / jnp.sum(ei, axis=-1, keepdims=True) for ei in e]
    pv = [_dot((ei * ii).astype(BF16), v) for ei, ii, (_, _, v) in zip(e, inv, chains)]
    for j in range(q_ref.shape[1] // LANE):
        out = jnp.where(hm[0], pv[j * nh], pv[j * nh + 1])
        o_ref[:, j * LANE:(j + 1) * LANE] = out.astype(o_ref.dtype)


def _ctx_attention(p, oa, n_seq, seq_len):
    w = CTX_ATTN_LANES
    return pl.pallas_call(
        _ctx_attn_kernel, grid=(n_seq, WA // w),
        in_specs=[pl.BlockSpec((seq_len, w), lambda b, h: (b, OFF_Q // w + h)),
                  pl.BlockSpec((seq_len, w), lambda b, h: (b, OFF_K // w + h)),
                  pl.BlockSpec((seq_len, w), lambda b, h: (b, OFF_V // w + h)),
                  pl.BlockSpec(memory_space=pl.ANY)],
        out_specs=pl.BlockSpec((seq_len, w), lambda b, h: (b, h)),
        out_shape=jax.ShapeDtypeStruct(oa.shape, oa.dtype),
        input_output_aliases={3: 0},
        compiler_params=_cparams(("parallel", "parallel")), name="ctx_attention",
    )(p, p, p, oa)


def _na_bias_table(rpb):
    nh, nr, nc = rpb.shape
    w = GRID_W
    rpb = rpb.astype(F32)
    lo = jnp.broadcast_to(rpb[..., :1], (nh, nr, w - NA_COLS))
    hi = jnp.broadcast_to(rpb[..., -1:], (nh, nr, 2 * w - (w - NA_COLS) - nc))
    ext = jnp.concatenate([lo, rpb, hi], axis=-1)
    toep = jnp.tile(ext, (1, 1, w))[..., :w * (2 * w - 1)].reshape(nh, nr, w, 2 * w - 1)[..., w - 1:]
    col = jnp.arange(w)
    cs = jnp.clip(col - NA_COLS // 2, 0, w - NA_COLS)
    valid = (col[None, :] >= cs[:, None]) & (col[None, :] < cs[:, None] + NA_COLS)
    toep = jnp.where(valid, toep, NEG_INF)
    tab = jnp.stack([toep[:, d0:d0 + NA_ROWS] for d0 in range(NA_ROWS)], axis=1)
    return tab.transpose(0, 1, 3, 2, 4).reshape(nh, NA_ROWS, w, NA_ROWS * w)


NA_ROWS_PER_STEP = 4
NA_CTX_CHUNK = 512


def _na_kernel(q_ref, k_ref, v_ref, ck_ref, cv_ref, tab_ref, o_ref, stat_ref, octx_ref, kbf_ref, vbf_ref, qm_ref, *,
               rows, n_fill):
    @pl.when(pl.program_id(0) < n_fill)
    def _():
        o_ref[...] = jnp.zeros(o_ref.shape, o_ref.dtype)

    @pl.when(pl.program_id(0) >= n_fill)
    def _():
        _na_body(q_ref, k_ref, v_ref, ck_ref, cv_ref, tab_ref, o_ref, stat_ref, octx_ref, kbf_ref, vbf_ref, qm_ref,
                 rows)


def _na_body(q_ref, k_ref, v_ref, ck_ref, cv_ref, tab_ref, o_ref, stat_ref, octx_ref, kbf_ref, vbf_ref, qm_ref, rows):
    assert HEAD_DIM_A & (HEAD_DIM_A - 1) == 0 and HEAD_DIM_A.bit_length() % 2 == 1
    scale = HEAD_DIM_A ** -0.5
    band = NA_ROWS * GRID_W
    nh = LANE // HEAD_DIM_A
    ck = ck_ref[0].astype(BF16)
    cv = cv_ref[0].astype(BF16)
    kbf_ref[...] = k_ref[...].astype(BF16)
    vbf_ref[...] = v_ref[...].astype(BF16)

    def ctx_body(c, carry):
        rsl = pl.ds(pl.multiple_of(c * NA_CTX_CHUNK, NA_CTX_CHUNK), NA_CTX_CHUNK)
        q = q_ref[rsl, :] * scale
        hm = _head_masks(q.shape)
        lane = lax.broadcasted_iota(jnp.int32, q.shape, 1)
        stat = jnp.zeros(q.shape, F32)
        out = jnp.zeros(q.shape, F32)
        for h in range(nh):
            qm = jnp.where(hm[h], q, 0.0).astype(BF16)
            qm_ref[h, rsl, :] = qm
            s = _dot_nt(qm, ck)
            m = jnp.max(s, axis=-1, keepdims=True)
            e = jnp.exp(s - m)
            l = jnp.sum(e, axis=-1, keepdims=True)
            stat = jnp.where(lane == 2 * h, m, jnp.where(lane == 2 * h + 1, l, stat))
            out = jnp.where(hm[h], _dot(e.astype(BF16), cv), out)
        stat_ref[rsl, :] = stat
        octx_ref[rsl, :] = out
        return carry

    lax.fori_loop(0, q_ref.shape[0] // NA_CTX_CHUNK, ctx_body, 0)

    nr = NA_ROWS_PER_STEP

    def body(it, carry):
        hm = _head_masks((GRID_W, LANE))
        rows_in = []
        chains = []
        for j in range(nr):
            r = it * nr + j
            rs = jnp.clip(r - NA_ROWS // 2, 0, rows - NA_ROWS)
            d0 = rs - r + NA_ROWS - 1
            qsl = pl.ds(pl.multiple_of(r * GRID_W, GRID_W), GRID_W)
            bsl = pl.ds(pl.multiple_of(rs * GRID_W, GRID_W), band)
            st = stat_ref[qsl, :]
            kb, vb = kbf_ref[bsl, :], vbf_ref[bsl, :]
            rows_in.append((qsl, octx_ref[qsl, :]))
            for h in range(nh):
                chains.append(dict(q=qm_ref[h, qsl, :], k=kb, v=vb, tab=tab_ref[h, d0],
                                   m_c=st[:, 2 * h:2 * h + 1], l_c=st[:, 2 * h + 1:2 * h + 2]))
        s = [_dot_nt(c["q"], c["k"]) + c["tab"] for c in chains]
        m = [jnp.maximum(jnp.max(si, axis=-1, keepdims=True), c["m_c"]) for si, c in zip(s, chains)]
        e = [jnp.exp(si - mi) for si, mi in zip(s, m)]
        a_c = [jnp.exp(c["m_c"] - mi) for c, mi in zip(chains, m)]
        inv = [1.0 / (jnp.sum(ei, axis=-1, keepdims=True) + ai * c["l_c"]) for ei, ai, c in zip(e, a_c, chains)]
        pv = [_dot(ei.astype(BF16), c["v"]) for ei, c in zip(e, chains)]
        for j, (qsl, oc) in enumerate(rows_in):
            out = jnp.zeros((GRID_W, LANE), F32)
            for h in range(nh):
                i = j * nh + h
                out = jnp.where(hm[h], (pv[i] + a_c[i] * oc) * inv[i], out)
            o_ref[qsl, :] = out.astype(o_ref.dtype)
        return carry

    lax.fori_loop(0, rows // nr, body, 0)


def _nbr_attention(p, ck, cv, tab, n_ctx, n_seq, seq_len, layer, depth):
    nb = WA // LANE
    past = ck.shape[1]
    nf = n_ctx // seq_len
    hp = LANE // HEAD_DIM_A
    tok = lambda off: pl.BlockSpec((seq_len, LANE), lambda b, h: (jnp.maximum(b, nf), off // LANE + h))
    cache = pl.BlockSpec((1, past, LANE), lambda b, h: (jnp.maximum(b - nf, 0) * depth + layer, 0, h))
    return pl.pallas_call(
        functools.partial(_na_kernel, rows=seq_len // GRID_W, n_fill=nf), grid=(nf + n_seq, nb),
        in_specs=[tok(OFF_Q), tok(OFF_K), tok(OFF_V), cache, cache,
                  pl.BlockSpec((hp, NA_ROWS, GRID_W, NA_ROWS * GRID_W), lambda b, h: (h, 0, 0, 0))],
        out_specs=pl.BlockSpec((seq_len, LANE), lambda b, h: (b, h)),
        out_shape=jax.ShapeDtypeStruct((p.shape[0], WA), BF16),
        scratch_shapes=[pltpu.VMEM((seq_len, LANE), F32), pltpu.VMEM((seq_len, LANE), F32),
                        pltpu.VMEM((seq_len, LANE), BF16), pltpu.VMEM((seq_len, LANE), BF16),
                        pltpu.VMEM((hp, seq_len, LANE), BF16)],
        compiler_params=_cparams(("parallel", "parallel")), name="nbr_attention",
    )(p, p, p, ck, cv, tab)


def _seq_pos(shape, row0, seq_len):
    return (row0 + lax.broadcasted_iota(jnp.int32, shape, 0)) & (seq_len - 1)


def _shifted(x, t, d, seq_len):
    n = x.shape[0]
    y = pltpu.roll(x, (-d) % n, 0)
    return jnp.where((t + d >= 0) & (t + d < seq_len), y, 0.0)


def _conv_kernel(x_ref, w_ref, o_ref, *, n_ctx, ctx_len, lat_len):
    rb = x_ref.shape[0]
    row0 = pl.program_id(0) * rb
    seq_len = jnp.where(row0 < n_ctx, ctx_len, lat_len)
    x = x_ref[...]
    t = _seq_pos(x.shape, row0, seq_len)
    w = w_ref[...]
    o_ref[...] = (_shifted(x, t, -1, seq_len) * w[0:1] + x * w[1:2] + _shifted(x, t, 1, seq_len) * w[2:3])


def _short_conv(p, conv_w, n_ctx, ctx_len, lat_len):
    n = p.shape[0]
    rb = lat_len
    c = conv_w.shape[1]
    w8 = jnp.zeros((8, c), F32).at[:3].set(conv_w)
    return pl.pallas_call(
        functools.partial(_conv_kernel, n_ctx=n_ctx, ctx_len=ctx_len, lat_len=lat_len),
        grid=(n // rb, c // LANE),
        in_specs=[pl.BlockSpec((rb, LANE), lambda i, j: (i, OFF_RKV // LANE + j)),
                  pl.BlockSpec((8, LANE), lambda i, j: (0, j))],
        out_specs=pl.BlockSpec((rb, LANE), lambda i, j: (i, j)),
        out_shape=jax.ShapeDtypeStruct((n, c), F32),
        compiler_params=_cparams(("parallel", "parallel")), name="short_conv",
    )(p, w8)


def _pool_kernel(u0_ref, u1_ref, u2_ref, u3_ref, w_ref, sc_ref, o_ref, *, n_ctx, ctx_len, lat_len):
    rb = o_ref.shape[0]
    row0 = pl.program_id(0) * rb
    seq_len = jnp.where(row0 < n_ctx, ctx_len, lat_len)
    t = _seq_pos((rb, POOL_GROUP_DIM), row0, seq_len)
    for gi, (win, u_ref) in enumerate(zip(POOL_WINDOWS, (u0_ref, u1_ref, u2_ref, u3_ref))):
        sl = slice(gi * POOL_GROUP_DIM, (gi + 1) * POOL_GROUP_DIM)
        u = u_ref[...]
        acc = u
        for d in range(-(win // 2), win - win // 2):
            if d != 0:
                acc = acc + _shifted(u, t, d, seq_len)
        lo = jnp.maximum(t - win // 2, 0)
        hi = jnp.minimum(t + win - win // 2, seq_len)
        pooled = acc / (hi - lo).astype(F32) - u
        y = _dot(pooled.astype(BF16), w_ref[gi].astype(BF16))
        o_ref[:, sl] = (y * sc_ref[:, sl]).astype(o_ref.dtype)


def _multiscale_pool(p, pool_w, pool_scale, n_ctx, ctx_len, lat_len):
    n = p.shape[0]
    rb = lat_len
    gd = POOL_GROUP_DIM
    group = lambda gi: pl.BlockSpec((rb, gd), lambda i: (i, OFF_POOL // gd + gi))
    return pl.pallas_call(
        functools.partial(_pool_kernel, n_ctx=n_ctx, ctx_len=ctx_len, lat_len=lat_len),
        grid=(n // rb,),
        in_specs=[group(0), group(1), group(2), group(3),
                  pl.BlockSpec(pool_w.shape, lambda i: (0, 0, 0)),
                  pl.BlockSpec((1, WC), lambda i: (0, 0))],
        out_specs=pl.BlockSpec((rb, WC), lambda i: (i, 0)),
        out_shape=jax.ShapeDtypeStruct((n, WC), BF16),
        compiler_params=_cparams(("parallel",)), name="multiscale_pool",
    )(p, p, p, p, pool_w, pool_scale.reshape(1, WC))


def _head_ones():
    r = jnp.arange(LANE) // HEAD_DIM_B
    return (r[:, None] == r[None, :]).astype(F32)


def _rwkv_prep_kernel(r_ref, k_ref, v_ref, xwa_ref, xg0_ref, xg1_ref, wl_ref, g2a_ref, g2b_ref, pt_ref,
                      ones_ref, kk_o, lw_o, b_o, kd_o, g_o, bonus_o):
    r, k, v = r_ref[...], k_ref[...], v_ref[...]
    xwa = xwa_ref[...]
    lane = lax.broadcasted_iota(jnp.int32, xwa.shape, 1)
    act = jnp.where(lane < 2 * DECAY_LORA, jnp.tanh(xwa), xwa)
    lora = _dot(act.astype(BF16), wl_ref[...].astype(BF16))
    pt = pt_ref[...]
    ones = ones_ref[...]
    kkr = k * pt[4:5]
    kk = kkr * lax.rsqrt(_dot(kkr * kkr, ones, precision=HI) + 1e-12)
    kk_o[...] = kk
    for d in range(2):
        z = -(pt[d:d + 1] + lora[:, d * LANE:(d + 1) * LANE])
        softplus = jnp.maximum(z, 0.0) + jnp.log(1.0 + jnp.exp(-jnp.abs(z)))
        lw_o[d] = -jnp.exp(-softplus - 0.5)
        a = _sigmoid(pt[2 + d:3 + d] + lora[:, (2 + d) * LANE:(3 + d) * LANE])
        kd_o[d] = k * (1.0 + (a - 1.0) * pt[5:6])
        b_o[d] = kk * a
    g_o[...] = (_dot(_sigmoid(xg0_ref[...]).astype(BF16), g2a_ref[...].astype(BF16))
                + _dot(_sigmoid(xg1_ref[...]).astype(BF16), g2b_ref[...].astype(BF16)))
    bonus_o[...] = _dot(r * k * pt[6:7], ones, precision=HI) * v


def _rwkv_prep(p, rkv, w0, w2, a0, a2, g2, k_k, k_a, r_k, rb):
    n = p.shape[0]
    nhp = WB // LANE
    wl = jnp.zeros((4, DECAY_LORA, 4, WB), F32)
    wl = wl.at[0, :, 0].set(w2[0]).at[1, :, 1].set(w2[1]).at[2, :, 2].set(a2[0]).at[3, :, 3].set(a2[1])
    wl = wl.reshape(4 * DECAY_LORA, 4, nhp, LANE).transpose(0, 2, 1, 3).reshape(4 * DECAY_LORA, nhp * 4 * LANE)
    pt = jnp.stack([w0[0], w0[1], a0[0], a0[1], k_k, k_a, r_k, jnp.zeros_like(k_k)]).astype(F32)
    nlo = 4 * DECAY_LORA
    col = lambda off: (lambda i, h: (i, off + h))
    tok = jax.ShapeDtypeStruct((n, WB), F32)
    tok2 = jax.ShapeDtypeStruct((2, n, WB), F32)
    spec1 = pl.BlockSpec((rb, LANE), lambda i, h: (i, h))
    spec2 = pl.BlockSpec((2, rb, LANE), lambda i, h: (0, i, h))
    return pl.pallas_call(
        _rwkv_prep_kernel, grid=(n // rb, nhp),
        in_specs=[pl.BlockSpec((rb, LANE), col(0)),
                  pl.BlockSpec((rb, LANE), col(nhp)),
                  pl.BlockSpec((rb, LANE), col(2 * nhp)),
                  pl.BlockSpec((rb, nlo), lambda i, h: (i, OFF_WL // nlo)),
                  pl.BlockSpec((rb, LANE), lambda i, h: (i, OFF_GL // LANE)),
                  pl.BlockSpec((rb, LANE), lambda i, h: (i, OFF_GL // LANE + 1)),
                  pl.BlockSpec((nlo, 4 * LANE), lambda i, h: (0, h)),
                  pl.BlockSpec((LANE, LANE), lambda i, h: (0, h)),
                  pl.BlockSpec((LANE, LANE), lambda i, h: (1, h)),
                  pl.BlockSpec((8, LANE), lambda i, h: (0, h)),
                  pl.BlockSpec((LANE, LANE), lambda i, h: (0, 0))],
        out_specs=[spec1, spec2, spec2, spec2, spec1, spec1],
        out_shape=[tok, tok2, tok2, tok2, tok, tok],
        compiler_params=_cparams(("parallel", "parallel")), name="rwkv_prep",
    )(rkv, rkv, rkv, p, p, p, wl, g2, g2, pt, _head_ones())


def _split_bf16(a):
    hi = pltpu.bitcast(pltpu.bitcast(a, jnp.uint32) & jnp.uint32(0xFFFF0000), F32)
    return hi.astype(BF16), (a - hi).astype(BF16)


_NN = (((1,), (0,)), ((), ()))
_NT = (((1,), (1,)), ((), ()))
_TN = (((0,), (0,)), ((), ()))


def _mm(a, b, passes, dims=_NN):
    if passes == 6:
        return lax.dot_general(a, b, dims, preferred_element_type=F32, precision=HI)
    dg = lambda p, q: lax.dot_general(p, q, dims, preferred_element_type=F32)
    if passes == 1:
        return dg(a.astype(BF16), b.astype(BF16))
    ah, al = _split_bf16(a)
    bh, bl = _split_bf16(b)
    return dg(ah, bh) + (dg(ah, bl) + dg(al, bh))


def _scan_kernel(rf_ref, vf_ref, kkf_ref, rb_ref, vb_ref, kkb_ref, lwf_ref, bf_ref, kdf_ref, lwb_ref, bb_ref,
                 kdb_ref, s0_ref, of_ref, ob_ref, se_ref, s_ref, *, ctx_steps, ctx_chunks, lat_chunks, passes):
    p_gram, p_inv, p_state = passes
    c = SCAN_CHUNK
    hd = HEAD_DIM_B
    j = pl.program_id(0)
    is_lat = j >= ctx_steps
    ci = jnp.where(is_lat, (j - ctx_steps) & (lat_chunks - 1), j & (ctx_chunks - 1))
    last = jnp.where(is_lat, lat_chunks, ctx_chunks) - 1

    @pl.when((ci == 0) & jnp.logical_not(is_lat))
    def _():
        s_ref[...] = jnp.zeros(s_ref.shape, F32)

    @pl.when((ci == 0) & is_lat)
    def _():
        s_ref[...] = s0_ref[:, 0]

    row = lax.broadcasted_iota(jnp.int32, (c, c), 0)
    col = lax.broadcasted_iota(jnp.int32, (c, c), 1)
    row2 = lax.broadcasted_iota(jnp.int32, (c, 2 * c), 0)
    col2 = lax.broadcasted_iota(jnp.int32, (c, 2 * c), 1) & (c - 1)
    eye = (col == row).astype(F32)
    sls = [slice(h * hd, (h + 1) * hd) for h in range(N_HEADS_B)]

    z, y, ye, vh, g_end, strict, incl2, sidx = [], [], [], [], [], [], [], []
    for d, (r_ref, v_ref, kk_ref, lw_ref, b_ref, kd_ref) in enumerate(
            ((rf_ref, vf_ref, kkf_ref, lwf_ref, bf_ref, kdf_ref), (rb_ref, vb_ref, kkb_ref, lwb_ref, bb_ref, kdb_ref))):
        before = (col < row) if d == 0 else (col > row)
        before2 = (col2 <= row2) if d == 0 else (col2 >= row2)
        lw = lw_ref[0]
        cum = _dot((before | (col == row)).astype(F32), lw, precision=HI)
        tot = jnp.sum(lw, axis=0, keepdims=True)
        r, v, kk, b, kd = r_ref[...], v_ref[...], kk_ref[...], b_ref[0], kd_ref[0]
        e_neg = jnp.exp(-cum)
        e_end = jnp.exp(tot - cum)
        kk_h = kk * jnp.exp(cum - lw)
        r_h = r * jnp.exp(cum)
        b_t, k_t = b * e_neg, kd * e_neg
        b_e, k_e = b * e_end, kd * e_end
        ge = jnp.exp(tot)
        for h, sl in enumerate(sls):
            z.append(jnp.concatenate([kk_h[:, sl], r_h[:, sl]], axis=0))
            y.append(jnp.concatenate([b_t[:, sl], k_t[:, sl]], axis=0))
            ye.append(jnp.concatenate([b_e[:, sl], k_e[:, sl]], axis=0))
            vh.append(v[:, sl])
            g_end.append(ge[:, sl])
            strict.append(before)
            incl2.append(before2)
            sidx.append((d, h))
    n = range(len(z))
    g = [_mm(z[i], y[i], p_gram, _NT) for i in n]
    x = [jnp.where(strict[i], -g[i][:c, :c], 0.0) for i in n]
    m_k = [jnp.where(strict[i], g[i][:c, c:], 0.0) for i in n]
    p_bk = [jnp.where(incl2[i], g[i][c:], 0.0) for i in n]
    tinv = [eye + x[i] for i in n]
    for _ in range(c.bit_length() - 2):
        x = [_mm(x[i], x[i], p_inv) for i in n]
        tinv = [tinv[i] + _mm(tinv[i], x[i], p_inv) for i in n]
    s = [s_ref[sidx[i]] for i in n]
    zs = [_mm(z[i], s[i], p_state, _NT) for i in n]
    mkv = [_mm(m_k[i], vh[i], p_state) for i in n]
    u = [-_mm(tinv[i], zs[i][:c] + mkv[i], p_state) for i in n]
    w = [jnp.concatenate([u[i], vh[i]], axis=0) for i in n]
    o = [zs[i][c:] + _mm(p_bk[i], w[i], p_state) for i in n]
    of_ref[...] = jnp.concatenate(o[:N_HEADS_B], axis=-1)
    ob_ref[...] = jnp.concatenate(o[N_HEADS_B:], axis=-1)
    for i in n:
        s_ref[sidx[i]] = s[i] * g_end[i] + _mm(w[i], ye[i], p_state, _TN)

    @pl.when(ci == last)
    def _():
        se_ref[:, 0] = s_ref[...]


def _rwkv_scan(rkv, kk, lw, b, kd, s0, n_cs, ctx_len, n_ls, lat_len, passes=(1, 1, 1)):
    c = SCAN_CHUNK
    cc, lc = ctx_len // c, lat_len // c
    assert cc & (cc - 1) == 0 and lc & (lc - 1) == 0
    ctx_steps = n_cs * cc
    n_steps = ctx_steps + n_ls * lc
    nb = WB // WB
    n = kk.shape[0]

    def where(j):
        jl = jnp.maximum(j - ctx_steps, 0)
        is_lat = j >= ctx_steps
        seq = jnp.where(is_lat, n_cs + jl // lc, j // cc)
        start = jnp.where(is_lat, ctx_steps + (jl // lc) * lc, (j // cc) * cc)
        ch = jnp.where(is_lat, jl % lc, j % cc)
        return seq, start, ch, jnp.where(is_lat, lc, cc)

    def rb(d, j):
        _, start, ch, nch = where(j)
        return start + ch + d * (nch - 1 - 2 * ch)

    tok = lambda d, colblk: pl.BlockSpec((c, WB), lambda j: (rb(d, j), colblk))
    tok2 = lambda d: pl.BlockSpec((1, c, WB), lambda j: (d, rb(d, j), 0))
    st_shape = (2, 1, N_HEADS_B, HEAD_DIM_B, HEAD_DIM_B)
    return pl.pallas_call(
        functools.partial(_scan_kernel, ctx_steps=ctx_steps, ctx_chunks=cc, lat_chunks=lc, passes=passes),
        grid=(n_steps,),
        in_specs=[tok(0, 0), tok(0, 2 * nb), tok(0, 0), tok(1, 0), tok(1, 2 * nb), tok(1, 0),
                  tok2(0), tok2(0), tok2(0), tok2(1), tok2(1), tok2(1),
                  pl.BlockSpec(st_shape, lambda j: (0, jnp.maximum(where(j)[0] - n_cs, 0), 0, 0, 0))],
        out_specs=[tok(0, 0), tok(1, 0), pl.BlockSpec(st_shape, lambda j: (0, where(j)[0], 0, 0, 0))],
        out_shape=[jax.ShapeDtypeStruct((n, WB), F32), jax.ShapeDtypeStruct((n, WB), F32),
                   jax.ShapeDtypeStruct((2, n_cs + n_ls, N_HEADS_B, HEAD_DIM_B, HEAD_DIM_B), F32)],
        scratch_shapes=[pltpu.VMEM((2, N_HEADS_B, HEAD_DIM_B, HEAD_DIM_B), F32)],
        compiler_params=_cparams(("arbitrary",)), name="rwkv_scan",
    )(rkv, rkv, kk, rkv, rkv, kk, lw, b, kd, lw, b, kd, s0)


def _rwkv_post_kernel(of_ref, ob_ref, bonus_ref, g_ref, lng_ref, lnb_ref, ones_ref, y_ref):
    o = of_ref[...] + ob_ref[...]
    mean_mat = ones_ref[...] * (1.0 / HEAD_DIM_B)
    mu = _dot(o, mean_mat, precision=HI)
    var = _dot(jnp.square(o - mu), mean_mat, precision=HI)
    y = (o - mu) * lax.rsqrt(var + GN_EPS) * lng_ref[...] + lnb_ref[...]
    y_ref[...] = ((y + bonus_ref[...]) * g_ref[...]).astype(y_ref.dtype)


def _rwkv_post(o_f, o_b, bonus, g, ln_g, ln_b, rb):
    n = bonus.shape[0]
    nhp = WB // LANE
    spec = pl.BlockSpec((rb, LANE), lambda i, h: (i, h))
    vec = pl.BlockSpec((1, LANE), lambda i, h: (0, h))
    return pl.pallas_call(
        _rwkv_post_kernel, grid=(n // rb, nhp),
        in_specs=[spec, spec, spec, spec, vec, vec, pl.BlockSpec((LANE, LANE), lambda i, h: (0, 0))],
        out_specs=spec,
        out_shape=jax.ShapeDtypeStruct((n, WB), BF16),
        compiler_params=_cparams(("parallel", "parallel")), name="rwkv_post",
    )(o_f, o_b, bonus, g, ln_g.reshape(1, WB), ln_b.reshape(1, WB), _head_ones())


def _outproj_kernel(a_ref, b_ref, c_ref, w_ref, x_ref, gt_ref, o_ref):
    y = (_dot(a_ref[...], w_ref[0:WA])
         + _dot(b_ref[...], w_ref[WA:WA + WB])
         + _dot(c_ref[...], w_ref[WA + WB:]))
    o_ref[...] = x_ref[...] + gt_ref[0] * y


def _out_proj(oa, ob, oc, w, x, gt, n_ctx, lat_len, tm, tn):
    n, d = x.shape
    grp = lambda i, j: (_group_of_rows(i * tm, n_ctx, lat_len), 0, j)
    return pl.pallas_call(
        _outproj_kernel, grid=(n // tm, d // tn),
        in_specs=[pl.BlockSpec((tm, WA), lambda i, j: (i, 0)),
                  pl.BlockSpec((tm, WB), lambda i, j: (i, 0)),
                  pl.BlockSpec((tm, WC), lambda i, j: (i, 0)),
                  pl.BlockSpec((d, tn), lambda i, j: (0, j)),
                  pl.BlockSpec((tm, tn), lambda i, j: (i, j)),
                  pl.BlockSpec((1, 1, tn), grp)],
        out_specs=pl.BlockSpec((tm, tn), lambda i, j: (i, j)),
        out_shape=jax.ShapeDtypeStruct((n, d), F32),
        compiler_params=_cparams(("parallel", "parallel")), name="out_proj",
    )(oa, ob, oc, w, x, gt)


def _swiglu_kernel(te_ref, nv_ref, h_ref, w1_ref, w3_ref, w2_ref, o_ref):
    tm = o_ref.shape[0]
    nv = nv_ref[pl.program_id(0)]

    @pl.when(pl.program_id(1) == 0)
    def _():
        o_ref[...] = jnp.zeros(o_ref.shape, F32)

    def accumulate(rows):
        h = h_ref[0:rows, :]
        a = _dot(h, w1_ref[0].astype(BF16))
        b = _dot(h, w3_ref[0].astype(BF16))
        act = (a * _sigmoid(a) * b).astype(BF16)
        o_ref[0:rows, :] += _dot(act, w2_ref[0].astype(BF16))

    @pl.when(nv > tm // 2)
    def _():
        accumulate(tm)

    @pl.when((nv > 0) & (nv <= tm // 2))
    def _():
        accumulate(tm // 2)


def _swiglu(h, tile_expert, tile_valid, w1, w3, w2, tm, tf, out_buffers=2):
    n, d = h.shape
    ff = w1.shape[2]
    return pl.pallas_call(
        _swiglu_kernel,
        grid_spec=pltpu.PrefetchScalarGridSpec(
            num_scalar_prefetch=2, grid=(n // tm, ff // tf),
            in_specs=[pl.BlockSpec((tm, d), lambda i, f, te, nt: (i, 0)),
                      pl.BlockSpec((1, d, tf), lambda i, f, te, nt: (te[i], 0, f)),
                      pl.BlockSpec((1, d, tf), lambda i, f, te, nt: (te[i], 0, f)),
                      pl.BlockSpec((1, tf, d), lambda i, f, te, nt: (te[i], f, 0))],
            out_specs=pl.BlockSpec((tm, d), lambda i, f, te, nt: (i, 0),
                                   pipeline_mode=pl.Buffered(out_buffers))),
        out_shape=jax.ShapeDtypeStruct((n, d), F32),
        compiler_params=_cparams(("parallel", "arbitrary")), name="swiglu",
    )(tile_expert, tile_valid, h, w1, w3, w2)


def _dense_ffn_kernel(x_ref, g_ref, sc_ref, sh_ref, gt_ref, w1_ref, w3_ref, w2_ref, o_ref, h_ref):
    f = pl.program_id(1)

    @pl.when(f == 0)
    def _():
        sub = NORM_ROWS

        def body(k, carry):
            rows = pl.ds(pl.multiple_of(k * sub, sub), sub)
            h_ref[rows, :] = _modulated(x_ref[rows, :], g_ref[...], sc_ref[0], sh_ref[0]).astype(BF16)
            return carry

        lax.fori_loop(0, x_ref.shape[0] // sub, body, 0)
        o_ref[...] = jnp.zeros(o_ref.shape, F32)

    h = h_ref[...]
    a = _dot(h, w1_ref[...])
    b = _dot(h, w3_ref[...])
    act = (a * _sigmoid(a) * b).astype(BF16)
    o_ref[...] += _dot(act, w2_ref[...])

    @pl.when(f == pl.num_programs(1) - 1)
    def _():
        o_ref[...] = x_ref[...] + gt_ref[0] * o_ref[...]


def _dense_ffn(x, g, sc, sh, gt, w1, w3, w2, n_ctx, lat_len, tm, tf):
    n, d = x.shape
    ff = w1.shape[1]
    grp = lambda i, f: (_group_of_rows(i * tm, n_ctx, lat_len), 0, 0)
    vec = pl.BlockSpec((1, 1, d), grp)
    row = pl.BlockSpec((tm, d), lambda i, f: (i, 0))
    return pl.pallas_call(
        _dense_ffn_kernel, grid=(n // tm, ff // tf),
        in_specs=[row, pl.BlockSpec((1, d), lambda i, f: (0, 0)), vec, vec, vec,
                  pl.BlockSpec((d, tf), lambda i, f: (0, f)),
                  pl.BlockSpec((d, tf), lambda i, f: (0, f)),
                  pl.BlockSpec((tf, d), lambda i, f: (f, 0))],
        out_specs=row,
        out_shape=jax.ShapeDtypeStruct((n, d), F32),
        scratch_shapes=[pltpu.VMEM((tm, d), BF16)],
        compiler_params=_cparams(("parallel", "arbitrary")), name="dense_ffn",
    )(x, g.reshape(1, d), sc, sh, gt, w1, w3, w2)


GATHER_UNROLL = 8


def _gather_into(idx_ref, base, src_ref, buf_ref, sem, skip_negative=False):
    rt = buf_ref.shape[0]

    def copy(r):
        return pltpu.make_async_copy(src_ref.at[pl.ds(idx_ref[base + r], 1)], buf_ref.at[pl.ds(r, 1)], sem)

    def guarded(r, action):
        if skip_negative:
            pl.when(idx_ref[base + r] >= 0)(action)
        else:
            action()

    def start(r, carry):
        guarded(r, lambda: copy(r).start())
        return carry

    def wait(r, carry):
        guarded(r, lambda: copy(r).wait())
        return carry

    lax.fori_loop(0, rt, start, 0, unroll=GATHER_UNROLL)
    lax.fori_loop(0, rt, wait, 0, unroll=GATHER_UNROLL)


def _gather_rows_kernel(idx_ref, src_ref, o_ref, buf_ref, sem):
    buf_ref[...] = jnp.zeros(buf_ref.shape, buf_ref.dtype)
    _gather_into(idx_ref, pl.program_id(0) * o_ref.shape[0], src_ref, buf_ref, sem, skip_negative=True)
    o_ref[...] = buf_ref[...].astype(o_ref.dtype)


def _gather_rows(src, idx, rt, out_dtype):
    n_out = idx.shape[0]
    d = src.shape[1]
    return pl.pallas_call(
        _gather_rows_kernel,
        grid_spec=pltpu.PrefetchScalarGridSpec(
            num_scalar_prefetch=1, grid=(n_out // rt,),
            in_specs=[pl.BlockSpec(memory_space=pl.ANY)],
            out_specs=pl.BlockSpec((rt, d), lambda i, idx: (i, 0)),
            scratch_shapes=[pltpu.VMEM((rt, d), src.dtype), pltpu.SemaphoreType.DMA(())]),
        out_shape=jax.ShapeDtypeStruct((n_out, d), out_dtype),
        compiler_params=_cparams(("arbitrary",)), name="gather_rows",
    )(idx, src)


def _combine_kernel(pos_ref, x_ref, gates_ref, gt_ref, fg_ref, ys_ref, *refs, n, final_norm, split_tiles):
    out_refs, (y1_ref, y2_ref, sem1, sem2) = refs[:-4], refs[-4:]
    tm = y1_ref.shape[0]
    base = pl.program_id(0) * tm
    _gather_into(pos_ref, base, ys_ref, y1_ref, sem1)
    _gather_into(pos_ref, n + base, ys_ref, y2_ref, sem2)
    g = gates_ref[...]
    x = x_ref[...] + gt_ref[0] * (g[:, 0:1] * y1_ref[...] + g[:, 1:2] * y2_ref[...])
    if final_norm:
        x = (x * lax.rsqrt(jnp.mean(x * x, axis=-1, keepdims=True) + RMS_EPS)) * fg_ref[...]
    if split_tiles is None:
        out_refs[0][...] = x
    else:
        @pl.when(pl.program_id(0) < split_tiles)
        def _():
            out_refs[0][...] = x

        @pl.when(pl.program_id(0) >= split_tiles)
        def _():
            out_refs[1][...] = x


def _combine(x, ys, pos, gates, gt, final_g, n_ctx, lat_len, tm, split=False):
    n, d = x.shape
    row = pl.BlockSpec((tm, d), lambda i, pos: (i, 0))
    st = n_ctx // tm
    if split:
        out_specs = [pl.BlockSpec((tm, d), lambda i, pos: (jnp.minimum(i, st - 1), 0)),
                     pl.BlockSpec((tm, d), lambda i, pos: (jnp.maximum(i - st, 0), 0))]
        out_shape = [jax.ShapeDtypeStruct((n_ctx, d), F32), jax.ShapeDtypeStruct((n - n_ctx, d), F32)]
    else:
        out_specs, out_shape = row, jax.ShapeDtypeStruct((n, d), F32)
    return pl.pallas_call(
        functools.partial(_combine_kernel, n=n, final_norm=final_g is not None, split_tiles=st if split else None),
        grid_spec=pltpu.PrefetchScalarGridSpec(
            num_scalar_prefetch=1, grid=(n // tm,),
            in_specs=[row, pl.BlockSpec((tm, LANE), lambda i, pos: (i, 0)),
                      pl.BlockSpec((1, 1, d), lambda i, pos: (_group_of_rows(i * tm, n_ctx, lat_len), 0, 0)),
                      pl.BlockSpec((1, d), lambda i, pos: (0, 0)),
                      pl.BlockSpec(memory_space=pl.ANY)],
            out_specs=out_specs,
            scratch_shapes=[pltpu.VMEM((tm, d), F32), pltpu.VMEM((tm, d), F32),
                            pltpu.SemaphoreType.DMA(()), pltpu.SemaphoreType.DMA(())]),
        out_shape=out_shape,
        compiler_params=_cparams(("arbitrary",)), name="moe_combine",
    )(pos, x, gates, gt, (jnp.ones((d,), F32) if final_g is None else final_g).reshape(1, d), ys)


def _moe(x, h, gates, idx, w1, w3, w2, gt, final_g, n_ctx, lat_len, tm, tf, rt, split=False):
    n, d = x.shape
    n_e = w1.shape[0]
    e = jnp.concatenate([idx[:, 0], idx[:, 1]])
    onehot = (e[:, None] == jnp.arange(n_e)[None, :]).astype(jnp.int32)
    rank = jnp.take_along_axis(jnp.cumsum(onehot, axis=0), e[:, None], axis=1)[:, 0] - 1
    counts = jnp.sum(onehot, axis=0)
    tiles = (counts + tm - 1) // tm
    tile_end = jnp.cumsum(tiles)
    start = (tile_end - tiles) * tm
    pos = (start[e] + rank).astype(jnp.int32)
    n_rows = 2 * n + n_e * tm
    src = jnp.full((n_rows,), -1, jnp.int32).at[pos].set(jnp.tile(jnp.arange(n, dtype=jnp.int32), 2))
    tile_ids = jnp.arange(n_rows // tm)
    tile_expert = jnp.minimum(jnp.sum(tile_ids[:, None] >= tile_end[None, :], axis=1), n_e - 1).astype(jnp.int32)
    first_tile = (tile_end - tiles)[tile_expert]
    tile_valid = jnp.clip(counts[tile_expert] - (tile_ids - first_tile) * tm, 0, tm)
    tile_valid = jnp.where(tile_ids < tile_end[-1], tile_valid, 0).astype(jnp.int32)

    hs = _gather_rows(h, src, rt, BF16)
    ys = _swiglu(hs, tile_expert, tile_valid, w1, w3, w2, tm, tf, out_buffers=1)
    return _combine(x, ys, pos, gates, gt, final_g, n_ctx, lat_len, rt, split)


def _kv_kernel(*refs, depth):
    k_refs, v_refs, (ok_ref, ov_ref) = refs[:depth], refs[depth:2 * depth], refs[2 * depth:]
    for l in range(depth):
        @pl.when(pl.program_id(0) == l)
        def _():
            ok_ref[0, 0] = k_refs[l][...]
            ov_ref[0, 0] = v_refs[l][...]


def _kv_outputs(p_layers, n_seq, seq_len):
    depth = len(p_layers)

    def col(l, off):
        return pl.BlockSpec((seq_len, WA), lambda ll, b: (jnp.where(ll == l, b, 0), off // WA))

    out = pl.BlockSpec((1, 1, seq_len, WA), lambda ll, b: (b, ll, 0, 0))
    shape = jax.ShapeDtypeStruct((n_seq, depth, seq_len, WA), F32)
    return pl.pallas_call(
        functools.partial(_kv_kernel, depth=depth), grid=(depth, n_seq),
        in_specs=[col(l, OFF_K) for l in range(depth)] + [col(l, OFF_V) for l in range(depth)],
        out_specs=[out, out], out_shape=[shape, shape],
        compiler_params=_cparams(("parallel", "parallel")), name="kv_outputs",
    )(*p_layers, *p_layers)


def _forward(x_prompt, x_sample, cache_attn_k, cache_attn_v, state_rwkv_fwd, state_rwkv_bwd, c, c_ctx,
             norm1_g, norm2_g, ada_w, ada_b, w_in, w_out, na_rpb, rw_conv, rw_w0, rw_w2, rw_a0, rw_a2,
             rw_g2, rw_kk, rw_ka, rw_rk, rw_ln_g, rw_ln_b, pool_w, pool_scale, ffn_w1, ffn_w3, ffn_w2,
             moe_router, moe_w1, moe_w3, moe_w2, final_g, *, tm, tm_proj, tn_in, tn_out, tm_dense, tf_dense,
             tm_moe, tf_moe, rt, rb):
    n_cs, ctx_len, d = x_prompt.shape
    n_ls, lat_len, _ = x_sample.shape
    n_ctx = n_cs * ctx_len
    n_lat = n_ls * lat_len
    n = n_ctx + n_lat
    depth = w_in.shape[0]
    past = cache_attn_k.shape[2]
    assert ctx_len & (ctx_len - 1) == 0 and lat_len & (lat_len - 1) == 0
    assert lat_len % ctx_len == 0 and n_ctx % lat_len == 0 and lat_len % GRID_W == 0

    x = jnp.concatenate([x_prompt.reshape(n_ctx, d), x_sample.reshape(n_lat, d)], axis=0)
    cond8 = jnp.zeros((8, d), F32).at[0].set(c_ctx).at[1:1 + n_ls].set(c)
    mods = _ada(cond8, ada_w, ada_b).reshape(depth, 8, 6, 1, d)

    ck = cache_attn_k.reshape(n_ls * depth, past, WA)
    cv = cache_attn_v.reshape(n_ls * depth, past, WA)
    p_layers, sf_new, sb_new = [], [], []
    for l in range(depth):
        m = [mods[l, :1 + n_ls, k] for k in range(6)]
        p = _in_proj(x, norm1_g[l], m[1], m[0], w_in[l].astype(BF16), n_ctx, lat_len, tm_proj, tn_in)

        oa = _nbr_attention(p, ck, cv, _na_bias_table(na_rpb[l]), n_ctx, n_ls, lat_len, l, depth)
        oa = _ctx_attention(p, oa, n_cs, ctx_len)

        rkv = _short_conv(p, rw_conv[l], n_ctx, ctx_len, lat_len)
        kk, lw, b, kd, g, bonus = _rwkv_prep(p, rkv, rw_w0[l], rw_w2[l], rw_a0[l], rw_a2[l], rw_g2[l],
                                             rw_kk[l], rw_ka[l], rw_rk[l], rb)
        s0 = jnp.stack([state_rwkv_fwd[:, l], state_rwkv_bwd[:, l]]).astype(F32)
        o_f, o_b, s_end = _rwkv_scan(rkv, kk, lw, b, kd, s0, n_cs, ctx_len, n_ls, lat_len)
        ob = _rwkv_post(o_f, o_b, bonus, g, rw_ln_g[l], rw_ln_b[l], rb)

        oc = _multiscale_pool(p, pool_w[l], pool_scale[l], n_ctx, ctx_len, lat_len)
        x = _out_proj(oa, ob, oc, w_out[l].astype(BF16), x, m[2], n_ctx, lat_len, tm_proj, tn_out)

        p_layers.append(p)
        sf_new.append(s_end[0, :n_cs])
        sb_new.append(s_end[1, :n_cs])

        i = l // 2
        if l % 2 == 0:
            x = _dense_ffn(x, norm2_g[l], m[4], m[3], m[5], ffn_w1[i].astype(BF16), ffn_w3[i].astype(BF16),
                           ffn_w2[i].astype(BF16), n_ctx, lat_len, tm_dense, tf_dense)
        else:
            h, gates, idx = _norm_mod_router(x, norm2_g[l], m[4], m[3], moe_router[i], n_ctx, lat_len, tm)
            last = l == depth - 1
            x = _moe(x, h, gates, idx, moe_w1[i], moe_w3[i], moe_w2[i], m[5], final_g if last else None,
                     n_ctx, lat_len, tm_moe, tf_moe, rt, split=last)

    if depth % 2 == 0:
        y_ctx, y_lat = x
    else:
        y = _final_norm(x, final_g, tm)
        y_ctx, y_lat = y[:n_ctx], y[n_ctx:]
    k_new, v_new = _kv_outputs(p_layers, n_cs, ctx_len)
    return (y_ctx.reshape(n_cs, ctx_len, d), y_lat.reshape(n_ls, lat_len, d),
            k_new.reshape(n_cs, depth, ctx_len, N_HEADS_A, HEAD_DIM_A),
            v_new.reshape(n_cs, depth, ctx_len, N_HEADS_A, HEAD_DIM_A),
            jnp.stack(sf_new, axis=1), jnp.stack(sb_new, axis=1))


def kernel(x_prompt, x_sample, cache_attn_k, cache_attn_v, state_rwkv_fwd, state_rwkv_bwd, c, c_ctx, norm1_g, norm2_g, ada_w, ada_b, w_in, w_out, na_rpb, rw_conv, rw_w0, rw_w2, rw_a0, rw_a2, rw_g2, rw_kk, rw_ka, rw_rk, rw_ln_g, rw_ln_b, pool_w, pool_scale, ffn_w1, ffn_w3, ffn_w2, moe_router, moe_w1, moe_w3, moe_w2, final_g):
    return _forward(x_prompt, x_sample, cache_attn_k, cache_attn_v, state_rwkv_fwd, state_rwkv_bwd, c, c_ctx,
                    norm1_g, norm2_g, ada_w, ada_b, w_in, w_out, na_rpb, rw_conv, rw_w0, rw_w2, rw_a0, rw_a2,
                    rw_g2, rw_kk, rw_ka, rw_rk, rw_ln_g, rw_ln_b, pool_w, pool_scale, ffn_w1, ffn_w3, ffn_w2,
                    moe_router, moe_w1, moe_w3, moe_w2, final_g,
                    tm=512, tm_proj=1024, tn_in=1152, tn_out=1024, tm_dense=1024, tf_dense=512,
                    tm_moe=1024, tf_moe=512, rt=256, rb=512)
```

```python
import functools

import jax
import jax.numpy as jnp
from jax import lax
from jax.experimental import pallas as pl
from jax.experimental.pallas import tpu as pltpu

F32 = jnp.float32
BF16 = jnp.bfloat16
HI = lax.Precision.HIGHEST

D_MODEL = 2048
DEPTH = 2
GRID_W = 64
WA = D_MODEL // 2
WB = D_MODEL // 4
WC = D_MODEL - WA - WB
HEAD_DIM_A = 64
N_HEADS_A = WA // HEAD_DIM_A
HEAD_DIM_B = 64
N_HEADS_B = WB // HEAD_DIM_B
POOL_WINDOWS = (2, 4, 8, 16)
POOL_GROUP_DIM = WC // len(POOL_WINDOWS)
NA_ROWS = 8
NA_COLS = 16
DECAY_LORA = 96
AAA_LORA = 96
GATE_LORA = 256
OFF_Q = 0
OFF_K = WA
OFF_V = 2 * WA
OFF_RKV = 3 * WA
OFF_WL = OFF_RKV + 3 * WB
OFF_AL = OFF_WL + 2 * DECAY_LORA
OFF_GL = OFF_AL + 2 * AAA_LORA
OFF_POOL = OFF_GL + GATE_LORA
P_IN = OFF_POOL + WC
N_EXPERTS = 8
RMS_EPS = 1e-6
GN_EPS = 64e-5
NEG_INF = -1e30

LANE = 128
VMEM_LIMIT = 56 * 1024 * 1024
SCAN_CHUNK = 64
NORM_ROWS = 256
MOE_ROW_STEP = 256


def _cparams(sem):
    return pltpu.CompilerParams(dimension_semantics=sem, vmem_limit_bytes=VMEM_LIMIT)


def _dot(a, b, precision=None):
    return jnp.dot(a, b, preferred_element_type=F32, precision=precision)


def _dot_nt(a, b, precision=None):
    return lax.dot_general(a, b, (((1,), (1,)), ((), ())), preferred_element_type=F32, precision=precision)


def _dot_tn(a, b, precision=None):
    return lax.dot_general(a, b, (((0,), (0,)), ((), ())), preferred_element_type=F32, precision=precision)


def _sigmoid(x):
    return 1.0 / (1.0 + jnp.exp(-x))


def _ada_kernel(c_ref, w_ref, b_ref, o_ref):
    c = c_ref[...]
    s = c * _sigmoid(c)
    o_ref[0] = _dot(s.astype(BF16), w_ref[0].astype(BF16)) + b_ref[0]


def _ada(cond8, ada_w, ada_b):
    depth, d, n6 = ada_w.shape
    tn = 768
    return pl.pallas_call(
        _ada_kernel,
        grid=(depth, n6 // tn),
        in_specs=[pl.BlockSpec((8, d), lambda l, j: (0, 0)),
                  pl.BlockSpec((1, d, tn), lambda l, j: (l, 0, j)),
                  pl.BlockSpec((1, 1, tn), lambda l, j: (l, 0, j))],
        out_specs=pl.BlockSpec((1, 8, tn), lambda l, j: (l, 0, j)),
        out_shape=jax.ShapeDtypeStruct((depth, 8, n6), F32),
        compiler_params=_cparams(("parallel", "parallel")),
        name="ada_mod",
    )(cond8, ada_w, ada_b.reshape(depth, 1, n6))


def _modulated(x, g, sc, sh):
    y = x * lax.rsqrt(jnp.mean(x * x, axis=-1, keepdims=True) + RMS_EPS)
    return (y * g) * (1.0 + sc) + sh


def _group_of_rows(row0, n_ctx, lat_len):
    return jnp.where(row0 < n_ctx, 0, 1 + (row0 - n_ctx) // lat_len)


def _inproj_kernel(x_ref, g_ref, sc_ref, sh_ref, w_ref, o_ref, h_ref):
    @pl.when(pl.program_id(1) == 0)
    def _():
        sub = NORM_ROWS

        def body(k, carry):
            rows = pl.ds(pl.multiple_of(k * sub, sub), sub)
            h_ref[rows, :] = _modulated(x_ref[rows, :], g_ref[...], sc_ref[0], sh_ref[0]).astype(BF16)
            return carry

        lax.fori_loop(0, x_ref.shape[0] // sub, body, 0)

    o_ref[...] = _dot(h_ref[...], w_ref[...])


def _in_proj(x, g, sc, sh, w, n_ctx, lat_len, tm, tn):
    n, d = x.shape
    pin = w.shape[1]
    grp = lambda i, j: (_group_of_rows(i * tm, n_ctx, lat_len), 0, 0)
    return pl.pallas_call(
        _inproj_kernel,
        grid=(n // tm, pin // tn),
        in_specs=[pl.BlockSpec((tm, d), lambda i, j: (i, 0)),
                  pl.BlockSpec((1, d), lambda i, j: (0, 0)),
                  pl.BlockSpec((1, 1, d), grp),
                  pl.BlockSpec((1, 1, d), grp),
                  pl.BlockSpec((d, tn), lambda i, j: (0, j))],
        out_specs=pl.BlockSpec((tm, tn), lambda i, j: (i, j)),
        out_shape=jax.ShapeDtypeStruct((n, pin), F32),
        scratch_shapes=[pltpu.VMEM((tm, d), BF16)],
        compiler_params=_cparams(("parallel", "arbitrary")),
        name="in_proj",
    )(x, g.reshape(1, d), sc, sh, w)


def _normmod_router_kernel(x_ref, g_ref, sc_ref, sh_ref, rt_ref, h_ref, gate_ref, idx_ref):
    h = _modulated(x_ref[...], g_ref[...], sc_ref[0], sh_ref[0])
    h_ref[...] = h
    logits = _dot(h, rt_ref[...], precision=HI)
    lane = lax.broadcasted_iota(jnp.int32, logits.shape, 1)
    lanef = lane.astype(F32)
    logits = jnp.where(lane < N_EXPERTS, logits, -jnp.inf)
    m1 = jnp.max(logits, axis=-1, keepdims=True)
    i1 = jnp.min(jnp.where(logits == m1, lanef, float(LANE)), axis=-1, keepdims=True)
    rest = jnp.where(lanef == i1, -jnp.inf, logits)
    m2 = jnp.max(rest, axis=-1, keepdims=True)
    i2 = jnp.min(jnp.where(rest == m2, lanef, float(LANE)), axis=-1, keepdims=True)
    e2 = jnp.exp(m2 - m1)
    p1 = 1.0 / (1.0 + e2)
    p2 = e2 / (1.0 + e2)
    gate_ref[...] = jnp.where(lane == 0, p1, jnp.where(lane == 1, p2, 0.0))
    idx_ref[...] = jnp.where(lane == 0, i1, jnp.where(lane == 1, i2, 0.0)).astype(jnp.int32)


def _norm_mod_router(x, g, sc, sh, router, n_ctx, lat_len, tm):
    n, d = x.shape
    grp = lambda i: (_group_of_rows(i * tm, n_ctx, lat_len), 0, 0)
    in_specs = [pl.BlockSpec((tm, d), lambda i: (i, 0)),
                pl.BlockSpec((1, d), lambda i: (0, 0)),
                pl.BlockSpec((1, 1, d), grp),
                pl.BlockSpec((1, 1, d), grp)]
    row_spec = pl.BlockSpec((tm, d), lambda i: (i, 0))
    rt = jnp.zeros((d, LANE), F32).at[:, :N_EXPERTS].set(router)
    lane_spec = pl.BlockSpec((tm, LANE), lambda i: (i, 0))
    return pl.pallas_call(
        _normmod_router_kernel, grid=(n // tm,),
        in_specs=in_specs + [pl.BlockSpec((d, LANE), lambda i: (0, 0))],
        out_specs=[row_spec, lane_spec, lane_spec],
        out_shape=[jax.ShapeDtypeStruct((n, d), F32),
                   jax.ShapeDtypeStruct((n, LANE), F32),
                   jax.ShapeDtypeStruct((n, LANE), jnp.int32)],
        compiler_params=_cparams(("parallel",)), name="norm_mod_router",
    )(x, g.reshape(1, d), sc, sh, rt)


def _final_norm_kernel(x_ref, g_ref, o_ref):
    x = x_ref[...]
    o_ref[...] = (x * lax.rsqrt(jnp.mean(x * x, axis=-1, keepdims=True) + RMS_EPS)) * g_ref[...]


def _final_norm(x, g, tm):
    n, d = x.shape
    return pl.pallas_call(
        _final_norm_kernel, grid=(n // tm,),
        in_specs=[pl.BlockSpec((tm, d), lambda i: (i, 0)), pl.BlockSpec((1, d), lambda i: (0, 0))],
        out_specs=pl.BlockSpec((tm, d), lambda i: (i, 0)),
        out_shape=jax.ShapeDtypeStruct((n, d), F32),
        compiler_params=_cparams(("parallel",)), name="final_norm",
    )(x, g.reshape(1, d))


CTX_ATTN_LANES = 512


def _head_masks(shape):
    lane = lax.broadcasted_iota(jnp.int32, shape, 1)
    return [lane // HEAD_DIM_A == h for h in range(LANE // HEAD_DIM_A)]


def _ctx_attn_kernel(q_ref, k_ref, v_ref, oa_ref, o_ref):
    del oa_ref
    scale = HEAD_DIM_A ** -0.5
    nh = LANE // HEAD_DIM_A
    hm = _head_masks((q_ref.shape[0], LANE))
    chains = []
    for j in range(q_ref.shape[1] // LANE):
        cols = slice(j * LANE, (j + 1) * LANE)
        q, k, v = q_ref[:, cols], k_ref[:, cols].astype(BF16), v_ref[:, cols].astype(BF16)
        chains += [(jnp.where(hm[h], q, 0.0).astype(BF16), k, v) for h in range(nh)]
    s = [_dot_nt(q, k) * scale for q, k, _ in chains]
    e = [jnp.exp(si - jnp.max(si, axis=-1, keepdims=True)) for si in s]
    inv = [1.0 / jnp.sum(ei, axis=-1, keepdims=True) for ei in e]
    pv = [_dot((ei * ii).astype(BF16), v) for ei, ii, (_, _, v) in zip(e, inv, chains)]
    for j in range(q_ref.shape[1] // LANE):
        out = jnp.where(hm[0], pv[j * nh], pv[j * nh + 1])
        o_ref[:, j * LANE:(j + 1) * LANE] = out.astype(o_ref.dtype)


def _ctx_attention(p, oa, n_seq, seq_len):
    w = CTX_ATTN_LANES
    return pl.pallas_call(
        _ctx_attn_kernel, grid=(n_seq, WA // w),
        in_specs=[pl.BlockSpec((seq_len, w), lambda b, h: (b, OFF_Q // w + h)),
                  pl.BlockSpec((seq_len, w), lambda b, h: (b, OFF_K // w + h)),
                  pl.BlockSpec((seq_len, w), lambda b, h: (b, OFF_V // w + h)),
                  pl.BlockSpec(memory_space=pl.ANY)],
        out_specs=pl.BlockSpec((seq_len, w), lambda b, h: (b, h)),
        out_shape=jax.ShapeDtypeStruct(oa.shape, oa.dtype),
        input_output_aliases={3: 0},
        compiler_params=_cparams(("parallel", "parallel")), name="ctx_attention",
    )(p, p, p, oa)


NA_WIN = (NA_ROWS + 2) * GRID_W


def _na_windows(rows):
    pairs = []
    for r in range(rows):
        rs = min(max(r - NA_ROWS // 2, 0), rows - NA_ROWS)
        at = min(rs // 2, rows // 2 - NA_WIN // LANE)
        pair = (rs - r + NA_ROWS - 1, rs - 2 * at)
        if pair not in pairs:
            pairs.append(pair)
    return pairs


def _na_bias_table(rpb, rows):
    nh, nr, nc = rpb.shape
    w = GRID_W
    rpb = rpb.astype(F32)
    lo = jnp.broadcast_to(rpb[..., :1], (nh, nr, w - NA_COLS))
    hi = jnp.broadcast_to(rpb[..., -1:], (nh, nr, 2 * w - (w - NA_COLS) - nc))
    ext = jnp.concatenate([lo, rpb, hi], axis=-1)
    toep = jnp.tile(ext, (1, 1, w))[..., :w * (2 * w - 1)].reshape(nh, nr, w, 2 * w - 1)[..., w - 1:]
    col = jnp.arange(w)
    cs = jnp.clip(col - NA_COLS // 2, 0, w - NA_COLS)
    valid = (col[None, :] >= cs[:, None]) & (col[None, :] < cs[:, None] + NA_COLS)
    toep = jnp.where(valid, toep, NEG_INF)
    variants = []
    for d0, off in _na_windows(rows):
        band = toep[:, d0:d0 + NA_ROWS].transpose(0, 2, 1, 3).reshape(nh, w, NA_ROWS * w)
        pad = ((0, 0), (0, 0), (off * w, NA_WIN - NA_ROWS * w - off * w))
        variants.append(jnp.pad(band, pad, constant_values=NEG_INF))
    return jnp.stack(variants, axis=1)


NA_ROWS_PER_STEP = 4
NA_CTX_CHUNK = 512


def _na_kernel(q_ref, k_ref, v_ref, ck_ref, cv_ref, tab_ref, o_ref, stat_ref, octx_ref, kt_ref, vbf_ref, qm_ref, *,
               rows, n_fill):
    @pl.when(pl.program_id(0) < n_fill)
    def _():
        o_ref[...] = jnp.zeros(o_ref.shape, o_ref.dtype)

    @pl.when(pl.program_id(0) >= n_fill)
    def _():
        _na_body(q_ref, k_ref, v_ref, ck_ref, cv_ref, tab_ref, o_ref, stat_ref, octx_ref, kt_ref, vbf_ref, qm_ref,
                 rows)


def _na_body(q_ref, k_ref, v_ref, ck_ref, cv_ref, tab_ref, o_ref, stat_ref, octx_ref, kt_ref, vbf_ref, qm_ref, rows):
    assert HEAD_DIM_A & (HEAD_DIM_A - 1) == 0 and HEAD_DIM_A.bit_length() % 2 == 1
    scale = HEAD_DIM_A ** -0.5
    nh = LANE // HEAD_DIM_A
    win_tiles = NA_WIN // LANE
    tiles_per_chunk = NA_CTX_CHUNK // LANE
    variants = _na_windows(rows)
    ck = ck_ref[0].astype(BF16)
    cv = cv_ref[0].astype(BF16)
    vbf_ref[...] = v_ref[...].astype(BF16)

    def ctx_body(c, carry):
        rsl = pl.ds(pl.multiple_of(c * NA_CTX_CHUNK, NA_CTX_CHUNK), NA_CTX_CHUNK)
        kt = k_ref[rsl, :].T
        for i in range(tiles_per_chunk):
            kt_ref[c * tiles_per_chunk + i] = kt[:, i * LANE:(i + 1) * LANE].astype(BF16)
        q = q_ref[rsl, :] * scale
        hm = _head_masks(q.shape)
        lane = lax.broadcasted_iota(jnp.int32, q.shape, 1)
        stat = jnp.zeros(q.shape, F32)
        out = jnp.zeros(q.shape, F32)
        for h in range(nh):
            qm = jnp.where(hm[h], q, 0.0).astype(BF16)
            qm_ref[h, rsl, :] = qm
            s = _dot_nt(qm, ck)
            m = jnp.max(s, axis=-1, keepdims=True)
            e = jnp.exp(s - m)
            l = jnp.sum(e, axis=-1, keepdims=True)
            stat = jnp.where(lane == 2 * h, m, jnp.where(lane == 2 * h + 1, l, stat))
            out = jnp.where(hm[h], _dot(e.astype(BF16), cv), out)
        stat_ref[rsl, :] = stat
        octx_ref[rsl, :] = out
        return carry

    lax.fori_loop(0, q_ref.shape[0] // NA_CTX_CHUNK, ctx_body, 0)

    nr = NA_ROWS_PER_STEP

    def body(it, carry):
        hm = _head_masks((GRID_W, LANE))
        rows_in = []
        chains = []
        for j in range(nr):
            r = it * nr + j
            rs = jnp.clip(r - NA_ROWS // 2, 0, rows - NA_ROWS)
            at = jnp.minimum(rs // 2, rows // 2 - win_tiles)
            d0, off = rs - r + NA_ROWS - 1, rs - 2 * at
            var = functools.reduce(lambda acc, kv: jnp.where((d0 == kv[1][0]) & (off == kv[1][1]), kv[0], acc),
                                   enumerate(variants), 0)
            qsl = pl.ds(pl.multiple_of(r * GRID_W, GRID_W), GRID_W)
            st = stat_ref[qsl, :]
            kwin = jnp.concatenate([kt_ref[at + i] for i in range(win_tiles)], axis=1)
            vwin = vbf_ref[pl.ds(pl.multiple_of(at * LANE, LANE), NA_WIN), :]
            rows_in.append((qsl, octx_ref[qsl, :]))
            for h in range(nh):
                chains.append(dict(q=qm_ref[h, qsl, :], k=kwin, v=vwin, tab=tab_ref[h, var],
                                   m_c=st[:, 2 * h:2 * h + 1], l_c=st[:, 2 * h + 1:2 * h + 2]))
        s = [_dot(c["q"], c["k"]) + c["tab"] for c in chains]
        m = [jnp.maximum(jnp.max(si, axis=-1, keepdims=True), c["m_c"]) for si, c in zip(s, chains)]
        e = [jnp.exp(si - mi) for si, mi in zip(s, m)]
        a_c = [jnp.exp(c["m_c"] - mi) for c, mi in zip(chains, m)]
        inv = [1.0 / (jnp.sum(ei, axis=-1, keepdims=True) + ai * c["l_c"]) for ei, ai, c in zip(e, a_c, chains)]
        pv = [_dot(ei.astype(BF16), c["v"]) for ei, c in zip(e, chains)]
        for j, (qsl, oc) in enumerate(rows_in):
            out = jnp.zeros((GRID_W, LANE), F32)
            for h in range(nh):
                i = j * nh + h
                out = jnp.where(hm[h], (pv[i] + a_c[i] * oc) * inv[i], out)
            o_ref[qsl, :] = out.astype(o_ref.dtype)
        return carry

    lax.fori_loop(0, rows // nr, body, 0)


def _nbr_attention(p, ck, cv, tab, n_ctx, n_seq, seq_len, layer, depth):
    nb = WA // LANE
    past = ck.shape[1]
    nf = n_ctx // seq_len
    hp = LANE // HEAD_DIM_A
    tok = lambda off: pl.BlockSpec((seq_len, LANE), lambda b, h: (jnp.maximum(b, nf), off // LANE + h))
    cache = pl.BlockSpec((1, past, LANE), lambda b, h: (jnp.maximum(b - nf, 0) * depth + layer, 0, h))
    return pl.pallas_call(
        functools.partial(_na_kernel, rows=seq_len // GRID_W, n_fill=nf), grid=(nf + n_seq, nb),
        in_specs=[tok(OFF_Q), tok(OFF_K), tok(OFF_V), cache, cache,
                  pl.BlockSpec((hp,) + tab.shape[1:], lambda b, h: (h, 0, 0, 0))],
        out_specs=pl.BlockSpec((seq_len, LANE), lambda b, h: (b, h)),
        out_shape=jax.ShapeDtypeStruct((p.shape[0], WA), BF16),
        scratch_shapes=[pltpu.VMEM((seq_len, LANE), F32), pltpu.VMEM((seq_len, LANE), F32),
                        pltpu.VMEM((seq_len // LANE, LANE, LANE), BF16), pltpu.VMEM((seq_len, LANE), BF16),
                        pltpu.VMEM((hp, seq_len, LANE), BF16)],
        compiler_params=_cparams(("parallel", "parallel")), name="nbr_attention",
    )(p, p, p, ck, cv, tab)


def _seq_pos(shape, row0, seq_len):
    return (row0 + lax.broadcasted_iota(jnp.int32, shape, 0)) & (seq_len - 1)


def _shifted(x, t, d, seq_len):
    n = x.shape[0]
    y = pltpu.roll(x, (-d) % n, 0)
    return jnp.where((t + d >= 0) & (t + d < seq_len), y, 0.0)


def _conv_kernel(x_ref, w_ref, o_ref, *, n_ctx, ctx_len, lat_len):
    rb = x_ref.shape[0]
    row0 = pl.program_id(0) * rb
    seq_len = jnp.where(row0 < n_ctx, ctx_len, lat_len)
    x = x_ref[...]
    t = _seq_pos(x.shape, row0, seq_len)
    w = w_ref[...]
    o_ref[...] = (_shifted(x, t, -1, seq_len) * w[0:1] + x * w[1:2] + _shifted(x, t, 1, seq_len) * w[2:3])


def _short_conv(p, conv_w, n_ctx, ctx_len, lat_len):
    n = p.shape[0]
    rb = lat_len
    c = conv_w.shape[1]
    w8 = jnp.zeros((8, c), F32).at[:3].set(conv_w)
    return pl.pallas_call(
        functools.partial(_conv_kernel, n_ctx=n_ctx, ctx_len=ctx_len, lat_len=lat_len),
        grid=(n // rb, c // LANE),
        in_specs=[pl.BlockSpec((rb, LANE), lambda i, j: (i, OFF_RKV // LANE + j)),
                  pl.BlockSpec((8, LANE), lambda i, j: (0, j))],
        out_specs=pl.BlockSpec((rb, LANE), lambda i, j: (i, j)),
        out_shape=jax.ShapeDtypeStruct((n, c), F32),
        compiler_params=_cparams(("parallel", "parallel")), name="short_conv",
    )(p, w8)


def _pool_kernel(u0_ref, u1_ref, u2_ref, u3_ref, w_ref, sc_ref, o_ref, *, n_ctx, ctx_len, lat_len):
    rb = o_ref.shape[0]
    row0 = pl.program_id(0) * rb
    seq_len = jnp.where(row0 < n_ctx, ctx_len, lat_len)
    t = _seq_pos((rb, POOL_GROUP_DIM), row0, seq_len)
    for gi, (win, u_ref) in enumerate(zip(POOL_WINDOWS, (u0_ref, u1_ref, u2_ref, u3_ref))):
        sl = slice(gi * POOL_GROUP_DIM, (gi + 1) * POOL_GROUP_DIM)
        u = u_ref[...]
        acc = u
        for d in range(-(win // 2), win - win // 2):
            if d != 0:
                acc = acc + _shifted(u, t, d, seq_len)
        lo = jnp.maximum(t - win // 2, 0)
        hi = jnp.minimum(t + win - win // 2, seq_len)
        pooled = acc / (hi - lo).astype(F32) - u
        y = _dot(pooled.astype(BF16), w_ref[gi].astype(BF16))
        o_ref[:, sl] = (y * sc_ref[:, sl]).astype(o_ref.dtype)


def _multiscale_pool(p, pool_w, pool_scale, n_ctx, ctx_len, lat_len):
    n = p.shape[0]
    rb = lat_len
    gd = POOL_GROUP_DIM
    group = lambda gi: pl.BlockSpec((rb, gd), lambda i: (i, OFF_POOL // gd + gi))
    return pl.pallas_call(
        functools.partial(_pool_kernel, n_ctx=n_ctx, ctx_len=ctx_len, lat_len=lat_len),
        grid=(n // rb,),
        in_specs=[group(0), group(1), group(2), group(3),
                  pl.BlockSpec(pool_w.shape, lambda i: (0, 0, 0)),
                  pl.BlockSpec((1, WC), lambda i: (0, 0))],
        out_specs=pl.BlockSpec((rb, WC), lambda i: (i, 0)),
        out_shape=jax.ShapeDtypeStruct((n, WC), BF16),
        compiler_params=_cparams(("parallel",)), name="multiscale_pool",
    )(p, p, p, p, pool_w, pool_scale.reshape(1, WC))


def _head_ones():
    r = jnp.arange(LANE) // HEAD_DIM_B
    return (r[:, None] == r[None, :]).astype(F32)


def _rwkv_prep_kernel(r_ref, k_ref, v_ref, xwa_ref, xg0_ref, xg1_ref, wl_ref, g2a_ref, g2b_ref, pt_ref,
                      ones_ref, kk_o, lw_o, b_o, kd_o, g_o, bonus_o):
    r, k, v = r_ref[...], k_ref[...], v_ref[...]
    xwa = xwa_ref[...]
    lane = lax.broadcasted_iota(jnp.int32, xwa.shape, 1)
    act = jnp.where(lane < 2 * DECAY_LORA, jnp.tanh(xwa), xwa)
    lora = _dot(act.astype(BF16), wl_ref[...].astype(BF16))
    pt = pt_ref[...]
    ones = ones_ref[...]
    kkr = k * pt[4:5]
    kk = kkr * lax.rsqrt(_dot(kkr * kkr, ones, precision=HI) + 1e-12)
    kk_o[...] = kk
    for d in range(2):
        z = -(pt[d:d + 1] + lora[:, d * LANE:(d + 1) * LANE])
        softplus = jnp.maximum(z, 0.0) + jnp.log(1.0 + jnp.exp(-jnp.abs(z)))
        lw_o[d] = -jnp.exp(-softplus - 0.5)
        a = _sigmoid(pt[2 + d:3 + d] + lora[:, (2 + d) * LANE:(3 + d) * LANE])
        kd_o[d] = k * (1.0 + (a - 1.0) * pt[5:6])
        b_o[d] = kk * a
    g_o[...] = (_dot(_sigmoid(xg0_ref[...]).astype(BF16), g2a_ref[...].astype(BF16))
                + _dot(_sigmoid(xg1_ref[...]).astype(BF16), g2b_ref[...].astype(BF16)))
    bonus_o[...] = _dot(r * k * pt[6:7], ones, precision=HI) * v


def _rwkv_prep(p, rkv, w0, w2, a0, a2, g2, k_k, k_a, r_k, rb):
    n = p.shape[0]
    nhp = WB // LANE
    wl = jnp.zeros((4, DECAY_LORA, 4, WB), F32)
    wl = wl.at[0, :, 0].set(w2[0]).at[1, :, 1].set(w2[1]).at[2, :, 2].set(a2[0]).at[3, :, 3].set(a2[1])
    wl = wl.reshape(4 * DECAY_LORA, 4, nhp, LANE).transpose(0, 2, 1, 3).reshape(4 * DECAY_LORA, nhp * 4 * LANE)
    pt = jnp.stack([w0[0], w0[1], a0[0], a0[1], k_k, k_a, r_k, jnp.zeros_like(k_k)]).astype(F32)
    nlo = 4 * DECAY_LORA
    col = lambda off: (lambda i, h: (i, off + h))
    tok = jax.ShapeDtypeStruct((n, WB), F32)
    tok2 = jax.ShapeDtypeStruct((2, n, WB), F32)
    spec1 = pl.BlockSpec((rb, LANE), lambda i, h: (i, h))
    spec2 = pl.BlockSpec((2, rb, LANE), lambda i, h: (0, i, h))
    return pl.pallas_call(
        _rwkv_prep_kernel, grid=(n // rb, nhp),
        in_specs=[pl.BlockSpec((rb, LANE), col(0)),
                  pl.BlockSpec((rb, LANE), col(nhp)),
                  pl.BlockSpec((rb, LANE), col(2 * nhp)),
                  pl.BlockSpec((rb, nlo), lambda i, h: (i, OFF_WL // nlo)),
                  pl.BlockSpec((rb, LANE), lambda i, h: (i, OFF_GL // LANE)),
                  pl.BlockSpec((rb, LANE), lambda i, h: (i, OFF_GL // LANE + 1)),
                  pl.BlockSpec((nlo, 4 * LANE), lambda i, h: (0, h)),
                  pl.BlockSpec((LANE, LANE), lambda i, h: (0, h)),
                  pl.BlockSpec((LANE, LANE), lambda i, h: (1, h)),
                  pl.BlockSpec((8, LANE), lambda i, h: (0, h)),
                  pl.BlockSpec((LANE, LANE), lambda i, h: (0, 0))],
        out_specs=[spec1, spec2, spec2, spec2, spec1, spec1],
        out_shape=[tok, tok2, tok2, tok2, tok, tok],
        compiler_params=_cparams(("parallel", "parallel")), name="rwkv_prep",
    )(rkv, rkv, rkv, p, p, p, wl, g2, g2, pt, _head_ones())


def _split_bf16(a):
    hi = pltpu.bitcast(pltpu.bitcast(a, jnp.uint32) & jnp.uint32(0xFFFF0000), F32)
    return hi.astype(BF16), (a - hi).astype(BF16)


_NN = (((1,), (0,)), ((), ()))
_NT = (((1,), (1,)), ((), ()))
_TN = (((0,), (0,)), ((), ()))


def _mm(a, b, passes, dims=_NN):
    if passes == 6:
        return lax.dot_general(a, b, dims, preferred_element_type=F32, precision=HI)
    dg = lambda p, q: lax.dot_general(p, q, dims, preferred_element_type=F32)
    if passes == 1:
        return dg(a.astype(BF16), b.astype(BF16))
    ah, al = _split_bf16(a)
    bh, bl = _split_bf16(b)
    return dg(ah, bh) + (dg(ah, bl) + dg(al, bh))


def _scan_kernel(rf_ref, vf_ref, kkf_ref, rb_ref, vb_ref, kkb_ref, lwf_ref, bf_ref, kdf_ref, lwb_ref, bb_ref,
                 kdb_ref, s0_ref, of_ref, ob_ref, se_ref, s_ref, *, ctx_steps, ctx_chunks, lat_chunks, passes):
    p_gram, p_inv, p_state = passes
    c = SCAN_CHUNK
    hd = HEAD_DIM_B
    j = pl.program_id(0)
    is_lat = j >= ctx_steps
    ci = jnp.where(is_lat, (j - ctx_steps) & (lat_chunks - 1), j & (ctx_chunks - 1))
    last = jnp.where(is_lat, lat_chunks, ctx_chunks) - 1

    @pl.when((ci == 0) & jnp.logical_not(is_lat))
    def _():
        s_ref[...] = jnp.zeros(s_ref.shape, F32)

    @pl.when((ci == 0) & is_lat)
    def _():
        s_ref[...] = s0_ref[:, 0]

    row = lax.broadcasted_iota(jnp.int32, (c, c), 0)
    col = lax.broadcasted_iota(jnp.int32, (c, c), 1)
    row2 = lax.broadcasted_iota(jnp.int32, (c, 2 * c), 0)
    col2 = lax.broadcasted_iota(jnp.int32, (c, 2 * c), 1) & (c - 1)
    eye = (col == row).astype(F32)
    sls = [slice(h * hd, (h + 1) * hd) for h in range(N_HEADS_B)]

    z, y, ye, vh, g_end, strict, incl2, sidx = [], [], [], [], [], [], [], []
    for d, (r_ref, v_ref, kk_ref, lw_ref, b_ref, kd_ref) in enumerate(
            ((rf_ref, vf_ref, kkf_ref, lwf_ref, bf_ref, kdf_ref), (rb_ref, vb_ref, kkb_ref, lwb_ref, bb_ref, kdb_ref))):
        before = (col < row) if d == 0 else (col > row)
        before2 = (col2 <= row2) if d == 0 else (col2 >= row2)
        lw = lw_ref[0]
        cum = _dot((before | (col == row)).astype(F32), lw, precision=HI)
        tot = jnp.sum(lw, axis=0, keepdims=True)
        r, v, kk, b, kd = r_ref[...], v_ref[...], kk_ref[...], b_ref[0], kd_ref[0]
        e_neg = jnp.exp(-cum)
        e_end = jnp.exp(tot - cum)
        kk_h = kk * jnp.exp(cum - lw)
        r_h = r * jnp.exp(cum)
        b_t, k_t = b * e_neg, kd * e_neg
        b_e, k_e = b * e_end, kd * e_end
        ge = jnp.exp(tot)
        for h, sl in enumerate(sls):
            z.append(jnp.concatenate([kk_h[:, sl], r_h[:, sl]], axis=0))
            y.append(jnp.concatenate([b_t[:, sl], k_t[:, sl]], axis=0))
            ye.append(jnp.concatenate([b_e[:, sl], k_e[:, sl]], axis=0))
            vh.append(v[:, sl])
            g_end.append(ge[:, sl])
            strict.append(before)
            incl2.append(before2)
            sidx.append((d, h))
    n = range(len(z))
    g = [_mm(z[i], y[i], p_gram, _NT) for i in n]
    x = [jnp.where(strict[i], -g[i][:c, :c], 0.0) for i in n]
    m_k = [jnp.where(strict[i], g[i][:c, c:], 0.0) for i in n]
    p_bk = [jnp.where(incl2[i], g[i][c:], 0.0) for i in n]
    tinv = [eye + x[i] for i in n]
    for _ in range(c.bit_length() - 2):
        x = [_mm(x[i], x[i], p_inv) for i in n]
        tinv = [tinv[i] + _mm(tinv[i], x[i], p_inv) for i in n]
    s = [s_ref[sidx[i]] for i in n]
    zs = [_mm(z[i], s[i], p_state, _NT) for i in n]
    mkv = [_mm(m_k[i], vh[i], p_state) for i in n]
    u = [-_mm(tinv[i], zs[i][:c] + mkv[i], p_state) for i in n]
    w = [jnp.concatenate([u[i], vh[i]], axis=0) for i in n]
    o = [zs[i][c:] + _mm(p_bk[i], w[i], p_state) for i in n]
    of_ref[...] = jnp.concatenate(o[:N_HEADS_B], axis=-1)
    ob_ref[...] = jnp.concatenate(o[N_HEADS_B:], axis=-1)
    for i in n:
        s_ref[sidx[i]] = s[i] * g_end[i] + _mm(w[i], ye[i], p_state, _TN)

    @pl.when(ci == last)
    def _():
        se_ref[:, 0] = s_ref[...]


def _rwkv_scan(rkv, kk, lw, b, kd, s0, n_cs, ctx_len, n_ls, lat_len, passes=(1, 1, 1)):
    c = SCAN_CHUNK
    cc, lc = ctx_len // c, lat_len // c
    assert cc & (cc - 1) == 0 and lc & (lc - 1) == 0
    ctx_steps = n_cs * cc
    n_steps = ctx_steps + n_ls * lc
    nb = WB // WB
    n = kk.shape[0]

    def where(j):
        jl = jnp.maximum(j - ctx_steps, 0)
        is_lat = j >= ctx_steps
        seq = jnp.where(is_lat, n_cs + jl // lc, j // cc)
        start = jnp.where(is_lat, ctx_steps + (jl // lc) * lc, (j // cc) * cc)
        ch = jnp.where(is_lat, jl % lc, j % cc)
        return seq, start, ch, jnp.where(is_lat, lc, cc)

    def rb(d, j):
        _, start, ch, nch = where(j)
        return start + ch + d * (nch - 1 - 2 * ch)

    tok = lambda d, colblk: pl.BlockSpec((c, WB), lambda j: (rb(d, j), colblk))
    tok2 = lambda d: pl.BlockSpec((1, c, WB), lambda j: (d, rb(d, j), 0))
    st_shape = (2, 1, N_HEADS_B, HEAD_DIM_B, HEAD_DIM_B)
    return pl.pallas_call(
        functools.partial(_scan_kernel, ctx_steps=ctx_steps, ctx_chunks=cc, lat_chunks=lc, passes=passes),
        grid=(n_steps,),
        in_specs=[tok(0, 0), tok(0, 2 * nb), tok(0, 0), tok(1, 0), tok(1, 2 * nb), tok(1, 0),
                  tok2(0), tok2(0), tok2(0), tok2(1), tok2(1), tok2(1),
                  pl.BlockSpec(st_shape, lambda j: (0, jnp.maximum(where(j)[0] - n_cs, 0), 0, 0, 0))],
        out_specs=[tok(0, 0), tok(1, 0), pl.BlockSpec(st_shape, lambda j: (0, where(j)[0], 0, 0, 0))],
        out_shape=[jax.ShapeDtypeStruct((n, WB), F32), jax.ShapeDtypeStruct((n, WB), F32),
                   jax.ShapeDtypeStruct((2, n_cs + n_ls, N_HEADS_B, HEAD_DIM_B, HEAD_DIM_B), F32)],
        scratch_shapes=[pltpu.VMEM((2, N_HEADS_B, HEAD_DIM_B, HEAD_DIM_B), F32)],
        compiler_params=_cparams(("arbitrary",)), name="rwkv_scan",
    )(rkv, rkv, kk, rkv, rkv, kk, lw, b, kd, lw, b, kd, s0)


def _rwkv_post_kernel(of_ref, ob_ref, bonus_ref, g_ref, lng_ref, lnb_ref, ones_ref, y_ref):
    o = of_ref[...] + ob_ref[...]
    mean_mat = ones_ref[...] * (1.0 / HEAD_DIM_B)
    mu = _dot(o, mean_mat, precision=HI)
    var = _dot(jnp.square(o - mu), mean_mat, precision=HI)
    y = (o - mu) * lax.rsqrt(var + GN_EPS) * lng_ref[...] + lnb_ref[...]
    y_ref[...] = ((y + bonus_ref[...]) * g_ref[...]).astype(y_ref.dtype)


def _rwkv_post(o_f, o_b, bonus, g, ln_g, ln_b, rb):
    n = bonus.shape[0]
    nhp = WB // LANE
    spec = pl.BlockSpec((rb, LANE), lambda i, h: (i, h))
    vec = pl.BlockSpec((1, LANE), lambda i, h: (0, h))
    return pl.pallas_call(
        _rwkv_post_kernel, grid=(n // rb, nhp),
        in_specs=[spec, spec, spec, spec, vec, vec, pl.BlockSpec((LANE, LANE), lambda i, h: (0, 0))],
        out_specs=spec,
        out_shape=jax.ShapeDtypeStruct((n, WB), BF16),
        compiler_params=_cparams(("parallel", "parallel")), name="rwkv_post",
    )(o_f, o_b, bonus, g, ln_g.reshape(1, WB), ln_b.reshape(1, WB), _head_ones())


def _outproj_kernel(a_ref, b_ref, c_ref, w_ref, x_ref, gt_ref, o_ref):
    y = (_dot(a_ref[...], w_ref[0:WA])
         + _dot(b_ref[...], w_ref[WA:WA + WB])
         + _dot(c_ref[...], w_ref[WA + WB:]))
    o_ref[...] = x_ref[...] + gt_ref[0] * y


def _out_proj(oa, ob, oc, w, x, gt, n_ctx, lat_len, tm, tn):
    n, d = x.shape
    grp = lambda i, j: (_group_of_rows(i * tm, n_ctx, lat_len), 0, j)
    return pl.pallas_call(
        _outproj_kernel, grid=(n // tm, d // tn),
        in_specs=[pl.BlockSpec((tm, WA), lambda i, j: (i, 0)),
                  pl.BlockSpec((tm, WB), lambda i, j: (i, 0)),
                  pl.BlockSpec((tm, WC), lambda i, j: (i, 0)),
                  pl.BlockSpec((d, tn), lambda i, j: (0, j)),
                  pl.BlockSpec((tm, tn), lambda i, j: (i, j)),
                  pl.BlockSpec((1, 1, tn), grp)],
        out_specs=pl.BlockSpec((tm, tn), lambda i, j: (i, j)),
        out_shape=jax.ShapeDtypeStruct((n, d), F32),
        compiler_params=_cparams(("parallel", "parallel")), name="out_proj",
    )(oa, ob, oc, w, x, gt)


def _swiglu_kernel(te_ref, nv_ref, h_ref, w1_ref, w3_ref, w2_ref, o_ref):
    tm = o_ref.shape[0]
    nv = nv_ref[pl.program_id(0)]

    @pl.when(pl.program_id(1) == 0)
    def _():
        o_ref[...] = jnp.zeros(o_ref.shape, F32)

    def accumulate(rows):
        h = h_ref[0:rows, :]
        a = _dot(h, w1_ref[0].astype(BF16))
        b = _dot(h, w3_ref[0].astype(BF16))
        act = (a * _sigmoid(a) * b).astype(BF16)
        o_ref[0:rows, :] += _dot(act, w2_ref[0].astype(BF16))

    for rows in range(MOE_ROW_STEP, tm + 1, MOE_ROW_STEP):
        pl.when((nv > rows - MOE_ROW_STEP) & (nv <= rows))(functools.partial(accumulate, rows))


def _swiglu(h, tile_expert, tile_valid, w1, w3, w2, tm, tf, out_buffers=2):
    n, d = h.shape
    ff = w1.shape[2]
    return pl.pallas_call(
        _swiglu_kernel,
        grid_spec=pltpu.PrefetchScalarGridSpec(
            num_scalar_prefetch=2, grid=(n // tm, ff // tf),
            in_specs=[pl.BlockSpec((tm, d), lambda i, f, te, nt: (i, 0)),
                      pl.BlockSpec((1, d, tf), lambda i, f, te, nt: (te[i], 0, f)),
                      pl.BlockSpec((1, d, tf), lambda i, f, te, nt: (te[i], 0, f)),
                      pl.BlockSpec((1, tf, d), lambda i, f, te, nt: (te[i], f, 0))],
            out_specs=pl.BlockSpec((tm, d), lambda i, f, te, nt: (i, 0),
                                   pipeline_mode=pl.Buffered(out_buffers))),
        out_shape=jax.ShapeDtypeStruct((n, d), F32),
        compiler_params=_cparams(("parallel", "arbitrary")), name="swiglu",
    )(tile_expert, tile_valid, h, w1, w3, w2)


def _dense_ffn_kernel(x_ref, g_ref, sc_ref, sh_ref, gt_ref, w1_ref, w3_ref, w2_ref, o_ref, h_ref):
    f = pl.program_id(1)

    @pl.when(f == 0)
    def _():
        sub = NORM_ROWS

        def body(k, carry):
            rows = pl.ds(pl.multiple_of(k * sub, sub), sub)
            h_ref[rows, :] = _modulated(x_ref[rows, :], g_ref[...], sc_ref[0], sh_ref[0]).astype(BF16)
            return carry

        lax.fori_loop(0, x_ref.shape[0] // sub, body, 0)
        o_ref[...] = jnp.zeros(o_ref.shape, F32)

    h = h_ref[...]
    a = _dot(h, w1_ref[...])
    b = _dot(h, w3_ref[...])
    act = (a * _sigmoid(a) * b).astype(BF16)
    o_ref[...] += _dot(act, w2_ref[...])

    @pl.when(f == pl.num_programs(1) - 1)
    def _():
        o_ref[...] = x_ref[...] + gt_ref[0] * o_ref[...]


def _dense_ffn(x, g, sc, sh, gt, w1, w3, w2, n_ctx, lat_len, tm, tf):
    n, d = x.shape
    ff = w1.shape[1]
    grp = lambda i, f: (_group_of_rows(i * tm, n_ctx, lat_len), 0, 0)
    vec = pl.BlockSpec((1, 1, d), grp)
    row = pl.BlockSpec((tm, d), lambda i, f: (i, 0))
    return pl.pallas_call(
        _dense_ffn_kernel, grid=(n // tm, ff // tf),
        in_specs=[row, pl.BlockSpec((1, d), lambda i, f: (0, 0)), vec, vec, vec,
                  pl.BlockSpec((d, tf), lambda i, f: (0, f)),
                  pl.BlockSpec((d, tf), lambda i, f: (0, f)),
                  pl.BlockSpec((tf, d), lambda i, f: (f, 0))],
        out_specs=row,
        out_shape=jax.ShapeDtypeStruct((n, d), F32),
        scratch_shapes=[pltpu.VMEM((tm, d), BF16)],
        compiler_params=_cparams(("parallel", "arbitrary")), name="dense_ffn",
    )(x, g.reshape(1, d), sc, sh, gt, w1, w3, w2)


GATHER_UNROLL = 8


def _gather_into(idx_ref, base, src_ref, buf_ref, sem, skip_negative=False):
    rt = buf_ref.shape[0]

    def copy(r):
        return pltpu.make_async_copy(src_ref.at[pl.ds(idx_ref[base + r], 1)], buf_ref.at[pl.ds(r, 1)], sem)

    def guarded(r, action):
        if skip_negative:
            pl.when(idx_ref[base + r] >= 0)(action)
        else:
            action()

    def start(r, carry):
        guarded(r, lambda: copy(r).start())
        return carry

    def wait(r, carry):
        guarded(r, lambda: copy(r).wait())
        return carry

    lax.fori_loop(0, rt, start, 0, unroll=GATHER_UNROLL)
    lax.fori_loop(0, rt, wait, 0, unroll=GATHER_UNROLL)


def _gather_rows_kernel(idx_ref, src_ref, o_ref, buf_ref, sem):
    buf_ref[...] = jnp.zeros(buf_ref.shape, buf_ref.dtype)
    _gather_into(idx_ref, pl.program_id(0) * o_ref.shape[0], src_ref, buf_ref, sem, skip_negative=True)
    o_ref[...] = buf_ref[...].astype(o_ref.dtype)


def _gather_rows(src, idx, rt, out_dtype):
    n_out = idx.shape[0]
    d = src.shape[1]
    return pl.pallas_call(
        _gather_rows_kernel,
        grid_spec=pltpu.PrefetchScalarGridSpec(
            num_scalar_prefetch=1, grid=(n_out // rt,),
            in_specs=[pl.BlockSpec(memory_space=pl.ANY)],
            out_specs=pl.BlockSpec((rt, d), lambda i, idx: (i, 0)),
            scratch_shapes=[pltpu.VMEM((rt, d), src.dtype), pltpu.SemaphoreType.DMA(())]),
        out_shape=jax.ShapeDtypeStruct((n_out, d), out_dtype),
        compiler_params=_cparams(("arbitrary",)), name="gather_rows",
    )(idx, src)


def _combine_kernel(pos_ref, x_ref, gates_ref, gt_ref, fg_ref, ys_ref, *refs, n, final_norm, split_tiles):
    out_refs, (y1_ref, y2_ref, sem1, sem2) = refs[:-4], refs[-4:]
    tm = y1_ref.shape[0]
    base = pl.program_id(0) * tm
    _gather_into(pos_ref, base, ys_ref, y1_ref, sem1)
    _gather_into(pos_ref, n + base, ys_ref, y2_ref, sem2)
    g = gates_ref[...]
    x = x_ref[...] + gt_ref[0] * (g[:, 0:1] * y1_ref[...] + g[:, 1:2] * y2_ref[...])
    if final_norm:
        x = (x * lax.rsqrt(jnp.mean(x * x, axis=-1, keepdims=True) + RMS_EPS)) * fg_ref[...]
    if split_tiles is None:
        out_refs[0][...] = x
    else:
        @pl.when(pl.program_id(0) < split_tiles)
        def _():
            out_refs[0][...] = x

        @pl.when(pl.program_id(0) >= split_tiles)
        def _():
            out_refs[1][...] = x


def _combine(x, ys, pos, gates, gt, final_g, n_ctx, lat_len, tm, split=False):
    n, d = x.shape
    row = pl.BlockSpec((tm, d), lambda i, pos: (i, 0))
    st = n_ctx // tm
    if split:
        out_specs = [pl.BlockSpec((tm, d), lambda i, pos: (jnp.minimum(i, st - 1), 0)),
                     pl.BlockSpec((tm, d), lambda i, pos: (jnp.maximum(i - st, 0), 0))]
        out_shape = [jax.ShapeDtypeStruct((n_ctx, d), F32), jax.ShapeDtypeStruct((n - n_ctx, d), F32)]
    else:
        out_specs, out_shape = row, jax.ShapeDtypeStruct((n, d), F32)
    return pl.pallas_call(
        functools.partial(_combine_kernel, n=n, final_norm=final_g is not None, split_tiles=st if split else None),
        grid_spec=pltpu.PrefetchScalarGridSpec(
            num_scalar_prefetch=1, grid=(n // tm,),
            in_specs=[row, pl.BlockSpec((tm, LANE), lambda i, pos: (i, 0)),
                      pl.BlockSpec((1, 1, d), lambda i, pos: (_group_of_rows(i * tm, n_ctx, lat_len), 0, 0)),
                      pl.BlockSpec((1, d), lambda i, pos: (0, 0)),
                      pl.BlockSpec(memory_space=pl.ANY)],
            out_specs=out_specs,
            scratch_shapes=[pltpu.VMEM((tm, d), F32), pltpu.VMEM((tm, d), F32),
                            pltpu.SemaphoreType.DMA(()), pltpu.SemaphoreType.DMA(())]),
        out_shape=out_shape,
        compiler_params=_cparams(("arbitrary",)), name="moe_combine",
    )(pos, x, gates, gt, (jnp.ones((d,), F32) if final_g is None else final_g).reshape(1, d), ys)


def _moe(x, h, gates, idx, w1, w3, w2, gt, final_g, n_ctx, lat_len, tm, tf, rt, split=False):
    n, d = x.shape
    n_e = w1.shape[0]
    e = jnp.concatenate([idx[:, 0], idx[:, 1]])
    onehot = (e[:, None] == jnp.arange(n_e)[None, :]).astype(jnp.int32)
    rank = jnp.take_along_axis(jnp.cumsum(onehot, axis=0), e[:, None], axis=1)[:, 0] - 1
    counts = jnp.sum(onehot, axis=0)
    tiles = (counts + tm - 1) // tm
    tile_end = jnp.cumsum(tiles)
    start = (tile_end - tiles) * tm
    pos = (start[e] + rank).astype(jnp.int32)
    n_rows = 2 * n + n_e * tm
    src = jnp.full((n_rows,), -1, jnp.int32).at[pos].set(jnp.tile(jnp.arange(n, dtype=jnp.int32), 2))
    tile_ids = jnp.arange(n_rows // tm)
    tile_expert = jnp.minimum(jnp.sum(tile_ids[:, None] >= tile_end[None, :], axis=1), n_e - 1).astype(jnp.int32)
    first_tile = (tile_end - tiles)[tile_expert]
    tile_valid = jnp.clip(counts[tile_expert] - (tile_ids - first_tile) * tm, 0, tm)
    tile_valid = jnp.where(tile_ids < tile_end[-1], tile_valid, 0).astype(jnp.int32)

    hs = _gather_rows(h, src, rt, BF16)
    ys = _swiglu(hs, tile_expert, tile_valid, w1, w3, w2, tm, tf, out_buffers=1)
    return _combine(x, ys, pos, gates, gt, final_g, n_ctx, lat_len, rt, split)


def _kv_kernel(*refs, depth):
    k_refs, v_refs, (ok_ref, ov_ref) = refs[:depth], refs[depth:2 * depth], refs[2 * depth:]
    for l in range(depth):
        @pl.when(pl.program_id(0) == l)
        def _():
            ok_ref[0, 0] = k_refs[l][...]
            ov_ref[0, 0] = v_refs[l][...]


def _kv_outputs(p_layers, n_seq, seq_len):
    depth = len(p_layers)

    def col(l, off):
        return pl.BlockSpec((seq_len, WA), lambda ll, b: (jnp.where(ll == l, b, 0), off // WA))

    out = pl.BlockSpec((1, 1, seq_len, WA), lambda ll, b: (b, ll, 0, 0))
    shape = jax.ShapeDtypeStruct((n_seq, depth, seq_len, WA), F32)
    return pl.pallas_call(
        functools.partial(_kv_kernel, depth=depth), grid=(depth, n_seq),
        in_specs=[col(l, OFF_K) for l in range(depth)] + [col(l, OFF_V) for l in range(depth)],
        out_specs=[out, out], out_shape=[shape, shape],
        compiler_params=_cparams(("parallel", "parallel")), name="kv_outputs",
    )(*p_layers, *p_layers)


def _forward(x_prompt, x_sample, cache_attn_k, cache_attn_v, state_rwkv_fwd, state_rwkv_bwd, c, c_ctx,
             norm1_g, norm2_g, ada_w, ada_b, w_in, w_out, na_rpb, rw_conv, rw_w0, rw_w2, rw_a0, rw_a2,
             rw_g2, rw_kk, rw_ka, rw_rk, rw_ln_g, rw_ln_b, pool_w, pool_scale, ffn_w1, ffn_w3, ffn_w2,
             moe_router, moe_w1, moe_w3, moe_w2, final_g, *, tm, tm_proj, tn_in, tn_out, tm_dense, tf_dense,
             tm_moe, tf_moe, rt, rb):
    n_cs, ctx_len, d = x_prompt.shape
    n_ls, lat_len, _ = x_sample.shape
    n_ctx = n_cs * ctx_len
    n_lat = n_ls * lat_len
    n = n_ctx + n_lat
    depth = w_in.shape[0]
    past = cache_attn_k.shape[2]
    assert ctx_len & (ctx_len - 1) == 0 and lat_len & (lat_len - 1) == 0
    assert lat_len % ctx_len == 0 and n_ctx % lat_len == 0 and lat_len % GRID_W == 0

    x = jnp.concatenate([x_prompt.reshape(n_ctx, d), x_sample.reshape(n_lat, d)], axis=0)
    cond8 = jnp.zeros((8, d), F32).at[0].set(c_ctx).at[1:1 + n_ls].set(c)
    mods = _ada(cond8, ada_w, ada_b).reshape(depth, 8, 6, 1, d)

    ck = cache_attn_k.reshape(n_ls * depth, past, WA)
    cv = cache_attn_v.reshape(n_ls * depth, past, WA)
    p_layers, sf_new, sb_new = [], [], []
    for l in range(depth):
        m = [mods[l, :1 + n_ls, k] for k in range(6)]
        p = _in_proj(x, norm1_g[l], m[1], m[0], w_in[l].astype(BF16), n_ctx, lat_len, tm_proj, tn_in)

        oa = _nbr_attention(p, ck, cv, _na_bias_table(na_rpb[l], lat_len // GRID_W), n_ctx, n_ls, lat_len, l, depth)
        oa = _ctx_attention(p, oa, n_cs, ctx_len)

        rkv = _short_conv(p, rw_conv[l], n_ctx, ctx_len, lat_len)
        kk, lw, b, kd, g, bonus = _rwkv_prep(p, rkv, rw_w0[l], rw_w2[l], rw_a0[l], rw_a2[l], rw_g2[l],
                                             rw_kk[l], rw_ka[l], rw_rk[l], rb)
        s0 = jnp.stack([state_rwkv_fwd[:, l], state_rwkv_bwd[:, l]]).astype(F32)
        o_f, o_b, s_end = _rwkv_scan(rkv, kk, lw, b, kd, s0, n_cs, ctx_len, n_ls, lat_len)
        ob = _rwkv_post(o_f, o_b, bonus, g, rw_ln_g[l], rw_ln_b[l], rb)

        oc = _multiscale_pool(p, pool_w[l], pool_scale[l], n_ctx, ctx_len, lat_len)
        x = _out_proj(oa, ob, oc, w_out[l].astype(BF16), x, m[2], n_ctx, lat_len, tm_proj, tn_out)

        p_layers.append(p)
        sf_new.append(s_end[0, :n_cs])
        sb_new.append(s_end[1, :n_cs])

        i = l // 2
        if l % 2 == 0:
            x = _dense_ffn(x, norm2_g[l], m[4], m[3], m[5], ffn_w1[i].astype(BF16), ffn_w3[i].astype(BF16),
                           ffn_w2[i].astype(BF16), n_ctx, lat_len, tm_dense, tf_dense)
        else:
            h, gates, idx = _norm_mod_router(x, norm2_g[l], m[4], m[3], moe_router[i], n_ctx, lat_len, tm)
            last = l == depth - 1
            x = _moe(x, h, gates, idx, moe_w1[i], moe_w3[i], moe_w2[i], m[5], final_g if last else None,
                     n_ctx, lat_len, tm_moe, tf_moe, rt, split=last)

    if depth % 2 == 0:
        y_ctx, y_lat = x
    else:
        y = _final_norm(x, final_g, tm)
        y_ctx, y_lat = y[:n_ctx], y[n_ctx:]
    k_new, v_new = _kv_outputs(p_layers, n_cs, ctx_len)
    return (y_ctx.reshape(n_cs, ctx_len, d), y_lat.reshape(n_ls, lat_len, d),
            k_new.reshape(n_cs, depth, ctx_len, N_HEADS_A, HEAD_DIM_A),
            v_new.reshape(n_cs, depth, ctx_len, N_HEADS_A, HEAD_DIM_A),
            jnp.stack(sf_new, axis=1), jnp.stack(sb_new, axis=1))


def kernel(x_prompt, x_sample, cache_attn_k, cache_attn_v, state_rwkv_fwd, state_rwkv_bwd, c, c_ctx, norm1_g, norm2_g, ada_w, ada_b, w_in, w_out, na_rpb, rw_conv, rw_w0, rw_w2, rw_a0, rw_a2, rw_g2, rw_kk, rw_ka, rw_rk, rw_ln_g, rw_ln_b, pool_w, pool_scale, ffn_w1, ffn_w3, ffn_w2, moe_router, moe_w1, moe_w3, moe_w2, final_g):
    return _forward(x_prompt, x_sample, cache_attn_k, cache_attn_v, state_rwkv_fwd, state_rwkv_bwd, c, c_ctx,
                    norm1_g, norm2_g, ada_w, ada_b, w_in, w_out, na_rpb, rw_conv, rw_w0, rw_w2, rw_a0, rw_a2,
                    rw_g2, rw_kk, rw_ka, rw_rk, rw_ln_g, rw_ln_b, pool_w, pool_scale, ffn_w1, ffn_w3, ffn_w2,
                    moe_router, moe_w1, moe_w3, moe_w2, final_g,
                    tm=512, tm_proj=1024, tn_in=1152, tn_out=1024, tm_dense=1024, tf_dense=512,
                    tm_moe=1024, tf_moe=512, rt=256, rb=512)
```

```python
import functools

import jax
import jax.numpy as jnp
from jax import lax
from jax.experimental import pallas as pl
from jax.experimental.pallas import tpu as pltpu

F32 = jnp.float32
BF16 = jnp.bfloat16
HI = lax.Precision.HIGHEST

D_MODEL = 2048
DEPTH = 2
GRID_W = 64
WA = D_MODEL // 2
WB = D_MODEL // 4
WC = D_MODEL - WA - WB
HEAD_DIM_A = 64
N_HEADS_A = WA // HEAD_DIM_A
HEAD_DIM_B = 64
N_HEADS_B = WB // HEAD_DIM_B
POOL_WINDOWS = (2, 4, 8, 16)
POOL_GROUP_DIM = WC // len(POOL_WINDOWS)
NA_ROWS = 8
NA_COLS = 16
DECAY_LORA = 96
AAA_LORA = 96
GATE_LORA = 256
OFF_Q = 0
OFF_K = WA
OFF_V = 2 * WA
OFF_RKV = 3 * WA
OFF_WL = OFF_RKV + 3 * WB
OFF_AL = OFF_WL + 2 * DECAY_LORA
OFF_GL = OFF_AL + 2 * AAA_LORA
OFF_POOL = OFF_GL + GATE_LORA
P_IN = OFF_POOL + WC
N_EXPERTS = 8
RMS_EPS = 1e-6
GN_EPS = 64e-5
NEG_INF = -1e30

LANE = 128
VMEM_LIMIT = 56 * 1024 * 1024
SCAN_CHUNK = 64
NORM_ROWS = 256
MOE_ROW_STEP = 256


def _cparams(sem):
    return pltpu.CompilerParams(dimension_semantics=sem, vmem_limit_bytes=VMEM_LIMIT)


def _dot(a, b, precision=None):
    return jnp.dot(a, b, preferred_element_type=F32, precision=precision)


def _dot_nt(a, b, precision=None):
    return lax.dot_general(a, b, (((1,), (1,)), ((), ())), preferred_element_type=F32, precision=precision)


def _dot_tn(a, b, precision=None):
    return lax.dot_general(a, b, (((0,), (0,)), ((), ())), preferred_element_type=F32, precision=precision)


def _sigmoid(x):
    return 1.0 / (1.0 + jnp.exp(-x))


def _ada_kernel(c_ref, w_ref, b_ref, o_ref):
    c = c_ref[...]
    s = c * _sigmoid(c)
    o_ref[0] = _dot(s.astype(BF16), w_ref[0].astype(BF16)) + b_ref[0]


def _ada(cond8, ada_w, ada_b):
    depth, d, n6 = ada_w.shape
    tn = 768
    return pl.pallas_call(
        _ada_kernel,
        grid=(depth, n6 // tn),
        in_specs=[pl.BlockSpec((8, d), lambda l, j: (0, 0)),
                  pl.BlockSpec((1, d, tn), lambda l, j: (l, 0, j)),
                  pl.BlockSpec((1, 1, tn), lambda l, j: (l, 0, j))],
        out_specs=pl.BlockSpec((1, 8, tn), lambda l, j: (l, 0, j)),
        out_shape=jax.ShapeDtypeStruct((depth, 8, n6), F32),
        compiler_params=_cparams(("parallel", "parallel")),
        name="ada_mod",
    )(cond8, ada_w, ada_b.reshape(depth, 1, n6))


def _modulated(x, g, sc, sh):
    y = x * lax.rsqrt(jnp.mean(x * x, axis=-1, keepdims=True) + RMS_EPS)
    return (y * g) * (1.0 + sc) + sh


def _group_of_rows(row0, n_ctx, lat_len):
    return jnp.where(row0 < n_ctx, 0, 1 + (row0 - n_ctx) // lat_len)


def _inproj_kernel(x_ref, g_ref, sc_ref, sh_ref, w_ref, o_ref, h_ref):
    @pl.when(pl.program_id(1) == 0)
    def _():
        sub = NORM_ROWS

        def body(k, carry):
            rows = pl.ds(pl.multiple_of(k * sub, sub), sub)
            h_ref[rows, :] = _modulated(x_ref[rows, :], g_ref[...], sc_ref[0], sh_ref[0]).astype(BF16)
            return carry

        lax.fori_loop(0, x_ref.shape[0] // sub, body, 0)

    o_ref[...] = _dot(h_ref[...], w_ref[...])


def _in_proj(x, g, sc, sh, w, n_ctx, lat_len, tm, tn):
    n, d = x.shape
    pin = w.shape[1]
    grp = lambda i, j: (_group_of_rows(i * tm, n_ctx, lat_len), 0, 0)
    return pl.pallas_call(
        _inproj_kernel,
        grid=(n // tm, pin // tn),
        in_specs=[pl.BlockSpec((tm, d), lambda i, j: (i, 0)),
                  pl.BlockSpec((1, d), lambda i, j: (0, 0)),
                  pl.BlockSpec((1, 1, d), grp),
                  pl.BlockSpec((1, 1, d), grp),
                  pl.BlockSpec((d, tn), lambda i, j: (0, j))],
        out_specs=pl.BlockSpec((tm, tn), lambda i, j: (i, j)),
        out_shape=jax.ShapeDtypeStruct((n, pin), F32),
        scratch_shapes=[pltpu.VMEM((tm, d), BF16)],
        compiler_params=_cparams(("parallel", "arbitrary")),
        name="in_proj",
    )(x, g.reshape(1, d), sc, sh, w)


def _normmod_router_kernel(x_ref, g_ref, sc_ref, sh_ref, rt_ref, h_ref, gate_ref, idx_ref):
    h = _modulated(x_ref[...], g_ref[...], sc_ref[0], sh_ref[0])
    h_ref[...] = h
    logits = _dot(h, rt_ref[...], precision=HI)
    lane = lax.broadcasted_iota(jnp.int32, logits.shape, 1)
    lanef = lane.astype(F32)
    logits = jnp.where(lane < N_EXPERTS, logits, -jnp.inf)
    m1 = jnp.max(logits, axis=-1, keepdims=True)
    i1 = jnp.min(jnp.where(logits == m1, lanef, float(LANE)), axis=-1, keepdims=True)
    rest = jnp.where(lanef == i1, -jnp.inf, logits)
    m2 = jnp.max(rest, axis=-1, keepdims=True)
    i2 = jnp.min(jnp.where(rest == m2, lanef, float(LANE)), axis=-1, keepdims=True)
    e2 = jnp.exp(m2 - m1)
    p1 = 1.0 / (1.0 + e2)
    p2 = e2 / (1.0 + e2)
    gate_ref[...] = jnp.where(lane == 0, p1, jnp.where(lane == 1, p2, 0.0))
    idx_ref[...] = jnp.where(lane == 0, i1, jnp.where(lane == 1, i2, 0.0)).astype(jnp.int32)


def _norm_mod_router(x, g, sc, sh, router, n_ctx, lat_len, tm):
    n, d = x.shape
    grp = lambda i: (_group_of_rows(i * tm, n_ctx, lat_len), 0, 0)
    in_specs = [pl.BlockSpec((tm, d), lambda i: (i, 0)),
                pl.BlockSpec((1, d), lambda i: (0, 0)),
                pl.BlockSpec((1, 1, d), grp),
                pl.BlockSpec((1, 1, d), grp)]
    row_spec = pl.BlockSpec((tm, d), lambda i: (i, 0))
    rt = jnp.zeros((d, LANE), F32).at[:, :N_EXPERTS].set(router)
    lane_spec = pl.BlockSpec((tm, LANE), lambda i: (i, 0))
    return pl.pallas_call(
        _normmod_router_kernel, grid=(n // tm,),
        in_specs=in_specs + [pl.BlockSpec((d, LANE), lambda i: (0, 0))],
        out_specs=[row_spec, lane_spec, lane_spec],
        out_shape=[jax.ShapeDtypeStruct((n, d), F32),
                   jax.ShapeDtypeStruct((n, LANE), F32),
                   jax.ShapeDtypeStruct((n, LANE), jnp.int32)],
        compiler_params=_cparams(("parallel",)), name="norm_mod_router",
    )(x, g.reshape(1, d), sc, sh, rt)


def _final_norm_kernel(x_ref, g_ref, o_ref):
    x = x_ref[...]
    o_ref[...] = (x * lax.rsqrt(jnp.mean(x * x, axis=-1, keepdims=True) + RMS_EPS)) * g_ref[...]


def _final_norm(x, g, tm):
    n, d = x.shape
    return pl.pallas_call(
        _final_norm_kernel, grid=(n // tm,),
        in_specs=[pl.BlockSpec((tm, d), lambda i: (i, 0)), pl.BlockSpec((1, d), lambda i: (0, 0))],
        out_specs=pl.BlockSpec((tm, d), lambda i: (i, 0)),
        out_shape=jax.ShapeDtypeStruct((n, d), F32),
        compiler_params=_cparams(("parallel",)), name="final_norm",
    )(x, g.reshape(1, d))


CTX_ATTN_LANES = 512


def _head_masks(shape):
    lane = lax.broadcasted_iota(jnp.int32, shape, 1)
    return [lane // HEAD_DIM_A == h for h in range(LANE // HEAD_DIM_A)]


def _ctx_attn_kernel(q_ref, k_ref, v_ref, oa_ref, o_ref):
    del oa_ref
    scale = HEAD_DIM_A ** -0.5
    nh = LANE // HEAD_DIM_A
    hm = _head_masks((q_ref.shape[0], LANE))
    chains = []
    for j in range(q_ref.shape[1] // LANE):
        cols = slice(j * LANE, (j + 1) * LANE)
        q, k, v = q_ref[:, cols], k_ref[:, cols].astype(BF16), v_ref[:, cols].astype(BF16)
        chains += [(jnp.where(hm[h], q, 0.0).astype(BF16), k, v) for h in range(nh)]
    s = [_dot_nt(q, k) * scale for q, k, _ in chains]
    e = [jnp.exp(si - jnp.max(si, axis=-1, keepdims=True)) for si in s]
    inv = [1.0 / jnp.sum(ei, axis=-1, keepdims=True) for ei in e]
    pv = [_dot((ei * ii).astype(BF16), v) for ei, ii, (_, _, v) in zip(e, inv, chains)]
    for j in range(q_ref.shape[1] // LANE):
        out = jnp.where(hm[0], pv[j * nh], pv[j * nh + 1])
        o_ref[:, j * LANE:(j + 1) * LANE] = out.astype(o_ref.dtype)


def _ctx_attention(p, oa, n_seq, seq_len):
    w = CTX_ATTN_LANES
    return pl.pallas_call(
        _ctx_attn_kernel, grid=(n_seq, WA // w),
        in_specs=[pl.BlockSpec((seq_len, w), lambda b, h: (b, OFF_Q // w + h)),
                  pl.BlockSpec((seq_len, w), lambda b, h: (b, OFF_K // w + h)),
                  pl.BlockSpec((seq_len, w), lambda b, h: (b, OFF_V // w + h)),
                  pl.BlockSpec(memory_space=pl.ANY)],
        out_specs=pl.BlockSpec((seq_len, w), lambda b, h: (b, h)),
        out_shape=jax.ShapeDtypeStruct(oa.shape, oa.dtype),
        input_output_aliases={3: 0},
        compiler_params=_cparams(("parallel", "parallel")), name="ctx_attention",
    )(p, p, p, oa)


NA_WIN = (NA_ROWS + 2) * GRID_W


def _na_windows(rows):
    pairs = []
    for r in range(rows):
        rs = min(max(r - NA_ROWS // 2, 0), rows - NA_ROWS)
        at = min(rs // 2, rows // 2 - NA_WIN // LANE)
        pair = (rs - r + NA_ROWS - 1, rs - 2 * at)
        if pair not in pairs:
            pairs.append(pair)
    return pairs


def _na_bias_table(rpb, rows):
    nh, nr, nc = rpb.shape
    w = GRID_W
    rpb = rpb.astype(F32)
    lo = jnp.broadcast_to(rpb[..., :1], (nh, nr, w - NA_COLS))
    hi = jnp.broadcast_to(rpb[..., -1:], (nh, nr, 2 * w - (w - NA_COLS) - nc))
    ext = jnp.concatenate([lo, rpb, hi], axis=-1)
    toep = jnp.tile(ext, (1, 1, w))[..., :w * (2 * w - 1)].reshape(nh, nr, w, 2 * w - 1)[..., w - 1:]
    col = jnp.arange(w)
    cs = jnp.clip(col - NA_COLS // 2, 0, w - NA_COLS)
    valid = (col[None, :] >= cs[:, None]) & (col[None, :] < cs[:, None] + NA_COLS)
    toep = jnp.where(valid, toep, NEG_INF)
    extra = NA_WIN // w - NA_ROWS
    flat = jnp.pad(toep.transpose(0, 2, 1, 3).reshape(nh, w, nr * w), ((0, 0), (0, 0), (extra * w, extra * w)))
    pos = jnp.arange(NA_WIN)
    variants = []
    for d0, off in _na_windows(rows):
        start = (d0 - off + extra) * w
        in_band = (pos >= off * w) & (pos < (off + NA_ROWS) * w)
        variants.append(jnp.where(in_band, flat[:, :, start:start + NA_WIN], NEG_INF))
    return jnp.stack(variants, axis=1)


NA_ROWS_PER_STEP = 4
NA_CTX_CHUNK = 512


def _na_kernel(q_ref, k_ref, v_ref, ck_ref, cv_ref, tab_ref, o_ref, stat_ref, octx_ref, kt_ref, vbf_ref, qm_ref, *,
               rows, n_fill):
    @pl.when(pl.program_id(0) < n_fill)
    def _():
        o_ref[...] = jnp.zeros(o_ref.shape, o_ref.dtype)

    @pl.when(pl.program_id(0) >= n_fill)
    def _():
        _na_body(q_ref, k_ref, v_ref, ck_ref, cv_ref, tab_ref, o_ref, stat_ref, octx_ref, kt_ref, vbf_ref, qm_ref,
                 rows)


def _na_body(q_ref, k_ref, v_ref, ck_ref, cv_ref, tab_ref, o_ref, stat_ref, octx_ref, kt_ref, vbf_ref, qm_ref, rows):
    assert HEAD_DIM_A & (HEAD_DIM_A - 1) == 0 and HEAD_DIM_A.bit_length() % 2 == 1
    scale = HEAD_DIM_A ** -0.5
    nh = LANE // HEAD_DIM_A
    win_tiles = NA_WIN // LANE
    tiles_per_chunk = NA_CTX_CHUNK // LANE
    variants = _na_windows(rows)
    ck = ck_ref[0].astype(BF16)
    cv = cv_ref[0].astype(BF16)
    vbf_ref[...] = v_ref[...].astype(BF16)

    def ctx_body(c, carry):
        rsl = pl.ds(pl.multiple_of(c * NA_CTX_CHUNK, NA_CTX_CHUNK), NA_CTX_CHUNK)
        kt = k_ref[rsl, :].T
        for i in range(tiles_per_chunk):
            kt_ref[c * tiles_per_chunk + i] = kt[:, i * LANE:(i + 1) * LANE].astype(BF16)
        q = q_ref[rsl, :] * scale
        hm = _head_masks(q.shape)
        lane = lax.broadcasted_iota(jnp.int32, q.shape, 1)
        stat = jnp.zeros(q.shape, F32)
        out = jnp.zeros(q.shape, F32)
        for h in range(nh):
            qm = jnp.where(hm[h], q, 0.0).astype(BF16)
            qm_ref[h, rsl, :] = qm
            s = _dot_nt(qm, ck)
            m = jnp.max(s, axis=-1, keepdims=True)
            e = jnp.exp(s - m)
            l = jnp.sum(e, axis=-1, keepdims=True)
            stat = jnp.where(lane == 2 * h, m, jnp.where(lane == 2 * h + 1, l, stat))
            out = jnp.where(hm[h], _dot(e.astype(BF16), cv), out)
        stat_ref[rsl, :] = stat
        octx_ref[rsl, :] = out
        return carry

    lax.fori_loop(0, q_ref.shape[0] // NA_CTX_CHUNK, ctx_body, 0)

    nr = NA_ROWS_PER_STEP

    def body(it, carry):
        hm = _head_masks((GRID_W, LANE))
        rows_in = []
        chains = []
        for j in range(nr):
            r = it * nr + j
            rs = jnp.clip(r - NA_ROWS // 2, 0, rows - NA_ROWS)
            at = jnp.minimum(rs // 2, rows // 2 - win_tiles)
            d0, off = rs - r + NA_ROWS - 1, rs - 2 * at
            var = functools.reduce(lambda acc, kv: jnp.where((d0 == kv[1][0]) & (off == kv[1][1]), kv[0], acc),
                                   enumerate(variants), 0)
            qsl = pl.ds(pl.multiple_of(r * GRID_W, GRID_W), GRID_W)
            st = stat_ref[qsl, :]
            kwin = jnp.concatenate([kt_ref[at + i] for i in range(win_tiles)], axis=1)
            vwin = vbf_ref[pl.ds(pl.multiple_of(at * LANE, LANE), NA_WIN), :]
            rows_in.append((qsl, octx_ref[qsl, :]))
            for h in range(nh):
                chains.append(dict(q=qm_ref[h, qsl, :], k=kwin, v=vwin, tab=tab_ref[h, var],
                                   m_c=st[:, 2 * h:2 * h + 1], l_c=st[:, 2 * h + 1:2 * h + 2]))
        s = [_dot(c["q"], c["k"]) + c["tab"] for c in chains]
        m = [jnp.maximum(jnp.max(si, axis=-1, keepdims=True), c["m_c"]) for si, c in zip(s, chains)]
        e = [jnp.exp(si - mi) for si, mi in zip(s, m)]
        a_c = [jnp.exp(c["m_c"] - mi) for c, mi in zip(chains, m)]
        inv = [1.0 / (jnp.sum(ei, axis=-1, keepdims=True) + ai * c["l_c"]) for ei, ai, c in zip(e, a_c, chains)]
        pv = [_dot(ei.astype(BF16), c["v"]) for ei, c in zip(e, chains)]
        for j, (qsl, oc) in enumerate(rows_in):
            out = jnp.zeros((GRID_W, LANE), F32)
            for h in range(nh):
                i = j * nh + h
                out = jnp.where(hm[h], (pv[i] + a_c[i] * oc) * inv[i], out)
            o_ref[qsl, :] = out.astype(o_ref.dtype)
        return carry

    lax.fori_loop(0, rows // nr, body, 0)


def _nbr_attention(p, ck, cv, tab, n_ctx, n_seq, seq_len, layer, depth):
    nb = WA // LANE
    past = ck.shape[1]
    nf = n_ctx // seq_len
    hp = LANE // HEAD_DIM_A
    tok = lambda off: pl.BlockSpec((seq_len, LANE), lambda b, h: (jnp.maximum(b, nf), off // LANE + h))
    cache = pl.BlockSpec((1, past, LANE), lambda b, h: (jnp.maximum(b - nf, 0) * depth + layer, 0, h))
    return pl.pallas_call(
        functools.partial(_na_kernel, rows=seq_len // GRID_W, n_fill=nf), grid=(nf + n_seq, nb),
        in_specs=[tok(OFF_Q), tok(OFF_K), tok(OFF_V), cache, cache,
                  pl.BlockSpec((hp,) + tab.shape[1:], lambda b, h: (h, 0, 0, 0))],
        out_specs=pl.BlockSpec((seq_len, LANE), lambda b, h: (b, h)),
        out_shape=jax.ShapeDtypeStruct((p.shape[0], WA), BF16),
        scratch_shapes=[pltpu.VMEM((seq_len, LANE), F32), pltpu.VMEM((seq_len, LANE), F32),
                        pltpu.VMEM((seq_len // LANE, LANE, LANE), BF16), pltpu.VMEM((seq_len, LANE), BF16),
                        pltpu.VMEM((hp, seq_len, LANE), BF16)],
        compiler_params=_cparams(("parallel", "parallel")), name="nbr_attention",
    )(p, p, p, ck, cv, tab)


def _seq_pos(shape, row0, seq_len):
    return (row0 + lax.broadcasted_iota(jnp.int32, shape, 0)) & (seq_len - 1)


def _shifted(x, t, d, seq_len):
    n = x.shape[0]
    y = pltpu.roll(x, (-d) % n, 0)
    return jnp.where((t + d >= 0) & (t + d < seq_len), y, 0.0)


def _conv_kernel(x_ref, w_ref, o_ref, *, n_ctx, ctx_len, lat_len):
    rb = x_ref.shape[0]
    row0 = pl.program_id(0) * rb
    seq_len = jnp.where(row0 < n_ctx, ctx_len, lat_len)
    x = x_ref[...]
    t = _seq_pos(x.shape, row0, seq_len)
    w = w_ref[...]
    o_ref[...] = (_shifted(x, t, -1, seq_len) * w[0:1] + x * w[1:2] + _shifted(x, t, 1, seq_len) * w[2:3])


def _short_conv(p, conv_w, n_ctx, ctx_len, lat_len):
    n = p.shape[0]
    rb = lat_len
    c = conv_w.shape[1]
    w8 = jnp.zeros((8, c), F32).at[:3].set(conv_w)
    return pl.pallas_call(
        functools.partial(_conv_kernel, n_ctx=n_ctx, ctx_len=ctx_len, lat_len=lat_len),
        grid=(n // rb, c // LANE),
        in_specs=[pl.BlockSpec((rb, LANE), lambda i, j: (i, OFF_RKV // LANE + j)),
                  pl.BlockSpec((8, LANE), lambda i, j: (0, j))],
        out_specs=pl.BlockSpec((rb, LANE), lambda i, j: (i, j)),
        out_shape=jax.ShapeDtypeStruct((n, c), F32),
        compiler_params=_cparams(("parallel", "parallel")), name="short_conv",
    )(p, w8)


def _pool_kernel(u0_ref, u1_ref, u2_ref, u3_ref, w_ref, sc_ref, o_ref, *, n_ctx, ctx_len, lat_len):
    rb = o_ref.shape[0]
    row0 = pl.program_id(0) * rb
    seq_len = jnp.where(row0 < n_ctx, ctx_len, lat_len)
    t = _seq_pos((rb, POOL_GROUP_DIM), row0, seq_len)
    for gi, (win, u_ref) in enumerate(zip(POOL_WINDOWS, (u0_ref, u1_ref, u2_ref, u3_ref))):
        sl = slice(gi * POOL_GROUP_DIM, (gi + 1) * POOL_GROUP_DIM)
        u = u_ref[...]
        acc = u
        for d in range(-(win // 2), win - win // 2):
            if d != 0:
                acc = acc + _shifted(u, t, d, seq_len)
        lo = jnp.maximum(t - win // 2, 0)
        hi = jnp.minimum(t + win - win // 2, seq_len)
        pooled = acc / (hi - lo).astype(F32) - u
        y = _dot(pooled.astype(BF16), w_ref[gi].astype(BF16))
        o_ref[:, sl] = (y * sc_ref[:, sl]).astype(o_ref.dtype)


def _multiscale_pool(p, pool_w, pool_scale, n_ctx, ctx_len, lat_len):
    n = p.shape[0]
    rb = lat_len
    gd = POOL_GROUP_DIM
    group = lambda gi: pl.BlockSpec((rb, gd), lambda i: (i, OFF_POOL // gd + gi))
    return pl.pallas_call(
        functools.partial(_pool_kernel, n_ctx=n_ctx, ctx_len=ctx_len, lat_len=lat_len),
        grid=(n // rb,),
        in_specs=[group(0), group(1), group(2), group(3),
                  pl.BlockSpec(pool_w.shape, lambda i: (0, 0, 0)),
                  pl.BlockSpec((1, WC), lambda i: (0, 0))],
        out_specs=pl.BlockSpec((rb, WC), lambda i: (i, 0)),
        out_shape=jax.ShapeDtypeStruct((n, WC), BF16),
        compiler_params=_cparams(("parallel",)), name="multiscale_pool",
    )(p, p, p, p, pool_w, pool_scale.reshape(1, WC))


def _head_ones():
    r = jnp.arange(LANE) // HEAD_DIM_B
    return (r[:, None] == r[None, :]).astype(F32)


def _rwkv_prep_kernel(r_ref, k_ref, v_ref, xwa_ref, xg0_ref, xg1_ref, wl_ref, g2a_ref, g2b_ref, pt_ref,
                      ones_ref, kk_o, lw_o, b_o, kd_o, g_o, bonus_o):
    r, k, v = r_ref[...], k_ref[...], v_ref[...]
    xwa = xwa_ref[...]
    lane = lax.broadcasted_iota(jnp.int32, xwa.shape, 1)
    act = jnp.where(lane < 2 * DECAY_LORA, jnp.tanh(xwa), xwa)
    lora = _dot(act.astype(BF16), wl_ref[...].astype(BF16))
    pt = pt_ref[...]
    ones = ones_ref[...]
    kkr = k * pt[4:5]
    kk = kkr * lax.rsqrt(_dot(kkr * kkr, ones, precision=HI) + 1e-12)
    kk_o[...] = kk
    for d in range(2):
        z = -(pt[d:d + 1] + lora[:, d * LANE:(d + 1) * LANE])
        softplus = jnp.maximum(z, 0.0) + jnp.log(1.0 + jnp.exp(-jnp.abs(z)))
        lw_o[d] = -jnp.exp(-softplus - 0.5)
        a = _sigmoid(pt[2 + d:3 + d] + lora[:, (2 + d) * LANE:(3 + d) * LANE])
        kd_o[d] = k * (1.0 + (a - 1.0) * pt[5:6])
        b_o[d] = kk * a
    g_o[...] = (_dot(_sigmoid(xg0_ref[...]).astype(BF16), g2a_ref[...].astype(BF16))
                + _dot(_sigmoid(xg1_ref[...]).astype(BF16), g2b_ref[...].astype(BF16)))
    bonus_o[...] = _dot(r * k * pt[6:7], ones, precision=HI) * v


def _rwkv_prep(p, rkv, w0, w2, a0, a2, g2, k_k, k_a, r_k, rb):
    n = p.shape[0]
    nhp = WB // LANE
    wl = jnp.zeros((4, DECAY_LORA, 4, WB), F32)
    wl = wl.at[0, :, 0].set(w2[0]).at[1, :, 1].set(w2[1]).at[2, :, 2].set(a2[0]).at[3, :, 3].set(a2[1])
    wl = wl.reshape(4 * DECAY_LORA, 4, nhp, LANE).transpose(0, 2, 1, 3).reshape(4 * DECAY_LORA, nhp * 4 * LANE)
    pt = jnp.stack([w0[0], w0[1], a0[0], a0[1], k_k, k_a, r_k, jnp.zeros_like(k_k)]).astype(F32)
    nlo = 4 * DECAY_LORA
    col = lambda off: (lambda i, h: (i, off + h))
    tok = jax.ShapeDtypeStruct((n, WB), F32)
    tok2 = jax.ShapeDtypeStruct((2, n, WB), F32)
    spec1 = pl.BlockSpec((rb, LANE), lambda i, h: (i, h))
    spec2 = pl.BlockSpec((2, rb, LANE), lambda i, h: (0, i, h))
    return pl.pallas_call(
        _rwkv_prep_kernel, grid=(n // rb, nhp),
        in_specs=[pl.BlockSpec((rb, LANE), col(0)),
                  pl.BlockSpec((rb, LANE), col(nhp)),
                  pl.BlockSpec((rb, LANE), col(2 * nhp)),
                  pl.BlockSpec((rb, nlo), lambda i, h: (i, OFF_WL // nlo)),
                  pl.BlockSpec((rb, LANE), lambda i, h: (i, OFF_GL // LANE)),
                  pl.BlockSpec((rb, LANE), lambda i, h: (i, OFF_GL // LANE + 1)),
                  pl.BlockSpec((nlo, 4 * LANE), lambda i, h: (0, h)),
                  pl.BlockSpec((LANE, LANE), lambda i, h: (0, h)),
                  pl.BlockSpec((LANE, LANE), lambda i, h: (1, h)),
                  pl.BlockSpec((8, LANE), lambda i, h: (0, h)),
                  pl.BlockSpec((LANE, LANE), lambda i, h: (0, 0))],
        out_specs=[spec1, spec2, spec2, spec2, spec1, spec1],
        out_shape=[tok, tok2, tok2, tok2, tok, tok],
        compiler_params=_cparams(("parallel", "parallel")), name="rwkv_prep",
    )(rkv, rkv, rkv, p, p, p, wl, g2, g2, pt, _head_ones())


def _split_bf16(a):
    hi = pltpu.bitcast(pltpu.bitcast(a, jnp.uint32) & jnp.uint32(0xFFFF0000), F32)
    return hi.astype(BF16), (a - hi).astype(BF16)


_NN = (((1,), (0,)), ((), ()))
_NT = (((1,), (1,)), ((), ()))
_TN = (((0,), (0,)), ((), ()))


def _mm(a, b, passes, dims=_NN):
    if passes == 6:
        return lax.dot_general(a, b, dims, preferred_element_type=F32, precision=HI)
    dg = lambda p, q: lax.dot_general(p, q, dims, preferred_element_type=F32)
    if passes == 1:
        return dg(a.astype(BF16), b.astype(BF16))
    ah, al = _split_bf16(a)
    bh, bl = _split_bf16(b)
    return dg(ah, bh) + (dg(ah, bl) + dg(al, bh))


def _scan_kernel(rf_ref, vf_ref, kkf_ref, rb_ref, vb_ref, kkb_ref, lwf_ref, bf_ref, kdf_ref, lwb_ref, bb_ref,
                 kdb_ref, s0_ref, of_ref, ob_ref, se_ref, s_ref, *, ctx_steps, ctx_chunks, lat_chunks, passes):
    p_gram, p_inv, p_state = passes
    c = SCAN_CHUNK
    hd = HEAD_DIM_B
    j = pl.program_id(0)
    is_lat = j >= ctx_steps
    ci = jnp.where(is_lat, (j - ctx_steps) & (lat_chunks - 1), j & (ctx_chunks - 1))
    last = jnp.where(is_lat, lat_chunks, ctx_chunks) - 1

    @pl.when((ci == 0) & jnp.logical_not(is_lat))
    def _():
        s_ref[...] = jnp.zeros(s_ref.shape, F32)

    @pl.when((ci == 0) & is_lat)
    def _():
        s_ref[...] = s0_ref[:, 0]

    row = lax.broadcasted_iota(jnp.int32, (c, c), 0)
    col = lax.broadcasted_iota(jnp.int32, (c, c), 1)
    row2 = lax.broadcasted_iota(jnp.int32, (c, 2 * c), 0)
    col2 = lax.broadcasted_iota(jnp.int32, (c, 2 * c), 1) & (c - 1)
    eye = (col == row).astype(F32)
    sls = [slice(h * hd, (h + 1) * hd) for h in range(N_HEADS_B)]

    z, y, ye, vh, g_end, strict, incl2, sidx = [], [], [], [], [], [], [], []
    for d, (r_ref, v_ref, kk_ref, lw_ref, b_ref, kd_ref) in enumerate(
            ((rf_ref, vf_ref, kkf_ref, lwf_ref, bf_ref, kdf_ref), (rb_ref, vb_ref, kkb_ref, lwb_ref, bb_ref, kdb_ref))):
        before = (col < row) if d == 0 else (col > row)
        before2 = (col2 <= row2) if d == 0 else (col2 >= row2)
        lw = lw_ref[0]
        cum = _dot((before | (col == row)).astype(F32), lw, precision=HI)
        tot = jnp.sum(lw, axis=0, keepdims=True)
        r, v, kk, b, kd = r_ref[...], v_ref[...], kk_ref[...], b_ref[0], kd_ref[0]
        e_neg = jnp.exp(-cum)
        e_end = jnp.exp(tot - cum)
        kk_h = kk * jnp.exp(cum - lw)
        r_h = r * jnp.exp(cum)
        b_t, k_t = b * e_neg, kd * e_neg
        b_e, k_e = b * e_end, kd * e_end
        ge = jnp.exp(tot)
        for h, sl in enumerate(sls):
            z.append(jnp.concatenate([kk_h[:, sl], r_h[:, sl]], axis=0))
            y.append(jnp.concatenate([b_t[:, sl], k_t[:, sl]], axis=0))
            ye.append(jnp.concatenate([b_e[:, sl], k_e[:, sl]], axis=0))
            vh.append(v[:, sl])
            g_end.append(ge[:, sl])
            strict.append(before)
            incl2.append(before2)
            sidx.append((d, h))
    n = range(len(z))
    g = [_mm(z[i], y[i], p_gram, _NT) for i in n]
    x = [jnp.where(strict[i], -g[i][:c, :c], 0.0) for i in n]
    m_k = [jnp.where(strict[i], g[i][:c, c:], 0.0) for i in n]
    p_bk = [jnp.where(incl2[i], g[i][c:], 0.0) for i in n]
    tinv = [eye + x[i] for i in n]
    for _ in range(c.bit_length() - 2):
        x = [_mm(x[i], x[i], p_inv) for i in n]
        tinv = [tinv[i] + _mm(tinv[i], x[i], p_inv) for i in n]
    s = [s_ref[sidx[i]] for i in n]
    zs = [_mm(z[i], s[i], p_state, _NT) for i in n]
    mkv = [_mm(m_k[i], vh[i], p_state) for i in n]
    u = [-_mm(tinv[i], zs[i][:c] + mkv[i], p_state) for i in n]
    w = [jnp.concatenate([u[i], vh[i]], axis=0) for i in n]
    o = [zs[i][c:] + _mm(p_bk[i], w[i], p_state) for i in n]
    of_ref[...] = jnp.concatenate(o[:N_HEADS_B], axis=-1)
    ob_ref[...] = jnp.concatenate(o[N_HEADS_B:], axis=-1)
    for i in n:
        s_ref[sidx[i]] = s[i] * g_end[i] + _mm(w[i], ye[i], p_state, _TN)

    @pl.when(ci == last)
    def _():
        se_ref[:, 0] = s_ref[...]


def _rwkv_scan(rkv, kk, lw, b, kd, s0, n_cs, ctx_len, n_ls, lat_len, passes=(1, 1, 1)):
    c = SCAN_CHUNK
    cc, lc = ctx_len // c, lat_len // c
    assert cc & (cc - 1) == 0 and lc & (lc - 1) == 0
    ctx_steps = n_cs * cc
    n_steps = ctx_steps + n_ls * lc
    nb = WB // WB
    n = kk.shape[0]

    def where(j):
        jl = jnp.maximum(j - ctx_steps, 0)
        is_lat = j >= ctx_steps
        seq = jnp.where(is_lat, n_cs + jl // lc, j // cc)
        start = jnp.where(is_lat, ctx_steps + (jl // lc) * lc, (j // cc) * cc)
        ch = jnp.where(is_lat, jl % lc, j % cc)
        return seq, start, ch, jnp.where(is_lat, lc, cc)

    def rb(d, j):
        _, start, ch, nch = where(j)
        return start + ch + d * (nch - 1 - 2 * ch)

    tok = lambda d, colblk: pl.BlockSpec((c, WB), lambda j: (rb(d, j), colblk))
    tok2 = lambda d: pl.BlockSpec((1, c, WB), lambda j: (d, rb(d, j), 0))
    st_shape = (2, 1, N_HEADS_B, HEAD_DIM_B, HEAD_DIM_B)
    return pl.pallas_call(
        functools.partial(_scan_kernel, ctx_steps=ctx_steps, ctx_chunks=cc, lat_chunks=lc, passes=passes),
        grid=(n_steps,),
        in_specs=[tok(0, 0), tok(0, 2 * nb), tok(0, 0), tok(1, 0), tok(1, 2 * nb), tok(1, 0),
                  tok2(0), tok2(0), tok2(0), tok2(1), tok2(1), tok2(1),
                  pl.BlockSpec(st_shape, lambda j: (0, jnp.maximum(where(j)[0] - n_cs, 0), 0, 0, 0))],
        out_specs=[tok(0, 0), tok(1, 0), pl.BlockSpec(st_shape, lambda j: (0, where(j)[0], 0, 0, 0))],
        out_shape=[jax.ShapeDtypeStruct((n, WB), F32), jax.ShapeDtypeStruct((n, WB), F32),
                   jax.ShapeDtypeStruct((2, n_cs + n_ls, N_HEADS_B, HEAD_DIM_B, HEAD_DIM_B), F32)],
        scratch_shapes=[pltpu.VMEM((2, N_HEADS_B, HEAD_DIM_B, HEAD_DIM_B), F32)],
        compiler_params=_cparams(("arbitrary",)), name="rwkv_scan",
    )(rkv, rkv, kk, rkv, rkv, kk, lw, b, kd, lw, b, kd, s0)


def _rwkv_post_kernel(of_ref, ob_ref, bonus_ref, g_ref, lng_ref, lnb_ref, ones_ref, y_ref):
    o = of_ref[...] + ob_ref[...]
    mean_mat = ones_ref[...] * (1.0 / HEAD_DIM_B)
    mu = _dot(o, mean_mat, precision=HI)
    var = _dot(jnp.square(o - mu), mean_mat, precision=HI)
    y = (o - mu) * lax.rsqrt(var + GN_EPS) * lng_ref[...] + lnb_ref[...]
    y_ref[...] = ((y + bonus_ref[...]) * g_ref[...]).astype(y_ref.dtype)


def _rwkv_post(o_f, o_b, bonus, g, ln_g, ln_b, rb):
    n = bonus.shape[0]
    nhp = WB // LANE
    spec = pl.BlockSpec((rb, LANE), lambda i, h: (i, h))
    vec = pl.BlockSpec((1, LANE), lambda i, h: (0, h))
    return pl.pallas_call(
        _rwkv_post_kernel, grid=(n // rb, nhp),
        in_specs=[spec, spec, spec, spec, vec, vec, pl.BlockSpec((LANE, LANE), lambda i, h: (0, 0))],
        out_specs=spec,
        out_shape=jax.ShapeDtypeStruct((n, WB), BF16),
        compiler_params=_cparams(("parallel", "parallel")), name="rwkv_post",
    )(o_f, o_b, bonus, g, ln_g.reshape(1, WB), ln_b.reshape(1, WB), _head_ones())


def _outproj_kernel(a_ref, b_ref, c_ref, w_ref, x_ref, gt_ref, o_ref):
    y = (_dot(a_ref[...], w_ref[0:WA])
         + _dot(b_ref[...], w_ref[WA:WA + WB])
         + _dot(c_ref[...], w_ref[WA + WB:]))
    o_ref[...] = x_ref[...] + gt_ref[0] * y


def _out_proj(oa, ob, oc, w, x, gt, n_ctx, lat_len, tm, tn):
    n, d = x.shape
    grp = lambda i, j: (_group_of_rows(i * tm, n_ctx, lat_len), 0, j)
    return pl.pallas_call(
        _outproj_kernel, grid=(n // tm, d // tn),
        in_specs=[pl.BlockSpec((tm, WA), lambda i, j: (i, 0)),
                  pl.BlockSpec((tm, WB), lambda i, j: (i, 0)),
                  pl.BlockSpec((tm, WC), lambda i, j: (i, 0)),
                  pl.BlockSpec((d, tn), lambda i, j: (0, j)),
                  pl.BlockSpec((tm, tn), lambda i, j: (i, j)),
                  pl.BlockSpec((1, 1, tn), grp)],
        out_specs=pl.BlockSpec((tm, tn), lambda i, j: (i, j)),
        out_shape=jax.ShapeDtypeStruct((n, d), F32),
        compiler_params=_cparams(("parallel", "parallel")), name="out_proj",
    )(oa, ob, oc, w, x, gt)


def _swiglu_kernel(te_ref, nv_ref, h_ref, w1_ref, w3_ref, w2_ref, o_ref):
    tm = o_ref.shape[0]
    nv = nv_ref[pl.program_id(0)]

    @pl.when(pl.program_id(1) == 0)
    def _():
        o_ref[...] = jnp.zeros(o_ref.shape, F32)

    def accumulate(rows):
        h = h_ref[0:rows, :]
        a = _dot(h, w1_ref[0].astype(BF16))
        b = _dot(h, w3_ref[0].astype(BF16))
        act = (a * _sigmoid(a) * b).astype(BF16)
        o_ref[0:rows, :] += _dot(act, w2_ref[0].astype(BF16))

    for rows in range(MOE_ROW_STEP, tm + 1, MOE_ROW_STEP):
        pl.when((nv > rows - MOE_ROW_STEP) & (nv <= rows))(functools.partial(accumulate, rows))


def _swiglu(h, tile_expert, tile_valid, w1, w3, w2, tm, tf, out_buffers=2):
    n, d = h.shape
    ff = w1.shape[2]
    return pl.pallas_call(
        _swiglu_kernel,
        grid_spec=pltpu.PrefetchScalarGridSpec(
            num_scalar_prefetch=2, grid=(n // tm, ff // tf),
            in_specs=[pl.BlockSpec((tm, d), lambda i, f, te, nt: (i, 0)),
                      pl.BlockSpec((1, d, tf), lambda i, f, te, nt: (te[i], 0, f)),
                      pl.BlockSpec((1, d, tf), lambda i, f, te, nt: (te[i], 0, f)),
                      pl.BlockSpec((1, tf, d), lambda i, f, te, nt: (te[i], f, 0))],
            out_specs=pl.BlockSpec((tm, d), lambda i, f, te, nt: (i, 0),
                                   pipeline_mode=pl.Buffered(out_buffers))),
        out_shape=jax.ShapeDtypeStruct((n, d), F32),
        compiler_params=_cparams(("parallel", "arbitrary")), name="swiglu",
    )(tile_expert, tile_valid, h, w1, w3, w2)


def _dense_ffn_kernel(x_ref, g_ref, sc_ref, sh_ref, gt_ref, w1_ref, w3_ref, w2_ref, o_ref, h_ref):
    f = pl.program_id(1)

    @pl.when(f == 0)
    def _():
        sub = NORM_ROWS

        def body(k, carry):
            rows = pl.ds(pl.multiple_of(k * sub, sub), sub)
            h_ref[rows, :] = _modulated(x_ref[rows, :], g_ref[...], sc_ref[0], sh_ref[0]).astype(BF16)
            return carry

        lax.fori_loop(0, x_ref.shape[0] // sub, body, 0)
        o_ref[...] = jnp.zeros(o_ref.shape, F32)

    h = h_ref[...]
    a = _dot(h, w1_ref[...])
    b = _dot(h, w3_ref[...])
    act = (a * _sigmoid(a) * b).astype(BF16)
    o_ref[...] += _dot(act, w2_ref[...])

    @pl.when(f == pl.num_programs(1) - 1)
    def _():
        o_ref[...] = x_ref[...] + gt_ref[0] * o_ref[...]


def _dense_ffn(x, g, sc, sh, gt, w1, w3, w2, n_ctx, lat_len, tm, tf):
    n, d = x.shape
    ff = w1.shape[1]
    grp = lambda i, f: (_group_of_rows(i * tm, n_ctx, lat_len), 0, 0)
    vec = pl.BlockSpec((1, 1, d), grp)
    row = pl.BlockSpec((tm, d), lambda i, f: (i, 0))
    return pl.pallas_call(
        _dense_ffn_kernel, grid=(n // tm, ff // tf),
        in_specs=[row, pl.BlockSpec((1, d), lambda i, f: (0, 0)), vec, vec, vec,
                  pl.BlockSpec((d, tf), lambda i, f: (0, f)),
                  pl.BlockSpec((d, tf), lambda i, f: (0, f)),
                  pl.BlockSpec((tf, d), lambda i, f: (f, 0))],
        out_specs=row,
        out_shape=jax.ShapeDtypeStruct((n, d), F32),
        scratch_shapes=[pltpu.VMEM((tm, d), BF16)],
        compiler_params=_cparams(("parallel", "arbitrary")), name="dense_ffn",
    )(x, g.reshape(1, d), sc, sh, gt, w1, w3, w2)


GATHER_UNROLL = 8


def _gather_into(idx_ref, base, src_ref, buf_ref, sem, skip_negative=False):
    rt = buf_ref.shape[0]

    def copy(r):
        return pltpu.make_async_copy(src_ref.at[pl.ds(idx_ref[base + r], 1)], buf_ref.at[pl.ds(r, 1)], sem)

    def guarded(r, action):
        if skip_negative:
            pl.when(idx_ref[base + r] >= 0)(action)
        else:
            action()

    def start(r, carry):
        guarded(r, lambda: copy(r).start())
        return carry

    def wait(r, carry):
        guarded(r, lambda: copy(r).wait())
        return carry

    lax.fori_loop(0, rt, start, 0, unroll=GATHER_UNROLL)
    lax.fori_loop(0, rt, wait, 0, unroll=GATHER_UNROLL)


def _gather_rows_kernel(idx_ref, src_ref, o_ref, buf_ref, sem):
    buf_ref[...] = jnp.zeros(buf_ref.shape, buf_ref.dtype)
    _gather_into(idx_ref, pl.program_id(0) * o_ref.shape[0], src_ref, buf_ref, sem, skip_negative=True)
    o_ref[...] = buf_ref[...].astype(o_ref.dtype)


def _gather_rows(src, idx, rt, out_dtype):
    n_out = idx.shape[0]
    d = src.shape[1]
    return pl.pallas_call(
        _gather_rows_kernel,
        grid_spec=pltpu.PrefetchScalarGridSpec(
            num_scalar_prefetch=1, grid=(n_out // rt,),
            in_specs=[pl.BlockSpec(memory_space=pl.ANY)],
            out_specs=pl.BlockSpec((rt, d), lambda i, idx: (i, 0)),
            scratch_shapes=[pltpu.VMEM((rt, d), src.dtype), pltpu.SemaphoreType.DMA(())]),
        out_shape=jax.ShapeDtypeStruct((n_out, d), out_dtype),
        compiler_params=_cparams(("arbitrary",)), name="gather_rows",
    )(idx, src)


def _combine_kernel(pos_ref, x_ref, gates_ref, gt_ref, fg_ref, ys_ref, *refs, n, final_norm, split_tiles):
    out_refs, (y1_ref, y2_ref, sem1, sem2) = refs[:-4], refs[-4:]
    tm = y1_ref.shape[0]
    base = pl.program_id(0) * tm
    _gather_into(pos_ref, base, ys_ref, y1_ref, sem1)
    _gather_into(pos_ref, n + base, ys_ref, y2_ref, sem2)
    g = gates_ref[...]
    x = x_ref[...] + gt_ref[0] * (g[:, 0:1] * y1_ref[...] + g[:, 1:2] * y2_ref[...])
    if final_norm:
        x = (x * lax.rsqrt(jnp.mean(x * x, axis=-1, keepdims=True) + RMS_EPS)) * fg_ref[...]
    if split_tiles is None:
        out_refs[0][...] = x
    else:
        @pl.when(pl.program_id(0) < split_tiles)
        def _():
            out_refs[0][...] = x

        @pl.when(pl.program_id(0) >= split_tiles)
        def _():
            out_refs[1][...] = x


def _combine(x, ys, pos, gates, gt, final_g, n_ctx, lat_len, tm, split=False):
    n, d = x.shape
    row = pl.BlockSpec((tm, d), lambda i, pos: (i, 0))
    st = n_ctx // tm
    if split:
        out_specs = [pl.BlockSpec((tm, d), lambda i, pos: (jnp.minimum(i, st - 1), 0)),
                     pl.BlockSpec((tm, d), lambda i, pos: (jnp.maximum(i - st, 0), 0))]
        out_shape = [jax.ShapeDtypeStruct((n_ctx, d), F32), jax.ShapeDtypeStruct((n - n_ctx, d), F32)]
    else:
        out_specs, out_shape = row, jax.ShapeDtypeStruct((n, d), F32)
    return pl.pallas_call(
        functools.partial(_combine_kernel, n=n, final_norm=final_g is not None, split_tiles=st if split else None),
        grid_spec=pltpu.PrefetchScalarGridSpec(
            num_scalar_prefetch=1, grid=(n // tm,),
            in_specs=[row, pl.BlockSpec((tm, LANE), lambda i, pos: (i, 0)),
                      pl.BlockSpec((1, 1, d), lambda i, pos: (_group_of_rows(i * tm, n_ctx, lat_len), 0, 0)),
                      pl.BlockSpec((1, d), lambda i, pos: (0, 0)),
                      pl.BlockSpec(memory_space=pl.ANY)],
            out_specs=out_specs,
            scratch_shapes=[pltpu.VMEM((tm, d), F32), pltpu.VMEM((tm, d), F32),
                            pltpu.SemaphoreType.DMA(()), pltpu.SemaphoreType.DMA(())]),
        out_shape=out_shape,
        compiler_params=_cparams(("arbitrary",)), name="moe_combine",
    )(pos, x, gates, gt, (jnp.ones((d,), F32) if final_g is None else final_g).reshape(1, d), ys)


def _moe(x, h, gates, idx, w1, w3, w2, gt, final_g, n_ctx, lat_len, tm, tf, rt, split=False):
    n, d = x.shape
    n_e = w1.shape[0]
    e = jnp.concatenate([idx[:, 0], idx[:, 1]])
    onehot = (e[:, None] == jnp.arange(n_e)[None, :]).astype(jnp.int32)
    rank = jnp.take_along_axis(jnp.cumsum(onehot, axis=0), e[:, None], axis=1)[:, 0] - 1
    counts = jnp.sum(onehot, axis=0)
    tiles = (counts + tm - 1) // tm
    tile_end = jnp.cumsum(tiles)
    start = (tile_end - tiles) * tm
    pos = (start[e] + rank).astype(jnp.int32)
    n_rows = 2 * n + n_e * tm
    src = jnp.full((n_rows,), -1, jnp.int32).at[pos].set(jnp.tile(jnp.arange(n, dtype=jnp.int32), 2))
    tile_ids = jnp.arange(n_rows // tm)
    tile_expert = jnp.minimum(jnp.sum(tile_ids[:, None] >= tile_end[None, :], axis=1), n_e - 1).astype(jnp.int32)
    first_tile = (tile_end - tiles)[tile_expert]
    tile_valid = jnp.clip(counts[tile_expert] - (tile_ids - first_tile) * tm, 0, tm)
    tile_valid = jnp.where(tile_ids < tile_end[-1], tile_valid, 0).astype(jnp.int32)

    hs = _gather_rows(h, src, rt, BF16)
    ys = _swiglu(hs, tile_expert, tile_valid, w1, w3, w2, tm, tf, out_buffers=1)
    return _combine(x, ys, pos, gates, gt, final_g, n_ctx, lat_len, rt, split)


def _kv_kernel(*refs, depth):
    k_refs, v_refs, (ok_ref, ov_ref) = refs[:depth], refs[depth:2 * depth], refs[2 * depth:]
    for l in range(depth):
        @pl.when(pl.program_id(0) == l)
        def _():
            ok_ref[0, 0] = k_refs[l][...]
            ov_ref[0, 0] = v_refs[l][...]


def _kv_outputs(p_layers, n_seq, seq_len):
    depth = len(p_layers)

    def col(l, off):
        return pl.BlockSpec((seq_len, WA), lambda ll, b: (jnp.where(ll == l, b, 0), off // WA))

    out = pl.BlockSpec((1, 1, seq_len, WA), lambda ll, b: (b, ll, 0, 0))
    shape = jax.ShapeDtypeStruct((n_seq, depth, seq_len, WA), F32)
    return pl.pallas_call(
        functools.partial(_kv_kernel, depth=depth), grid=(depth, n_seq),
        in_specs=[col(l, OFF_K) for l in range(depth)] + [col(l, OFF_V) for l in range(depth)],
        out_specs=[out, out], out_shape=[shape, shape],
        compiler_params=_cparams(("parallel", "parallel")), name="kv_outputs",
    )(*p_layers, *p_layers)


def _forward(x_prompt, x_sample, cache_attn_k, cache_attn_v, state_rwkv_fwd, state_rwkv_bwd, c, c_ctx,
             norm1_g, norm2_g, ada_w, ada_b, w_in, w_out, na_rpb, rw_conv, rw_w0, rw_w2, rw_a0, rw_a2,
             rw_g2, rw_kk, rw_ka, rw_rk, rw_ln_g, rw_ln_b, pool_w, pool_scale, ffn_w1, ffn_w3, ffn_w2,
             moe_router, moe_w1, moe_w3, moe_w2, final_g, *, tm, tm_proj, tn_in, tn_out, tm_dense, tf_dense,
             tm_moe, tf_moe, rt, rb):
    n_cs, ctx_len, d = x_prompt.shape
    n_ls, lat_len, _ = x_sample.shape
    n_ctx = n_cs * ctx_len
    n_lat = n_ls * lat_len
    n = n_ctx + n_lat
    depth = w_in.shape[0]
    past = cache_attn_k.shape[2]
    assert ctx_len & (ctx_len - 1) == 0 and lat_len & (lat_len - 1) == 0
    assert lat_len % ctx_len == 0 and n_ctx % lat_len == 0 and lat_len % GRID_W == 0

    x = jnp.concatenate([x_prompt.reshape(n_ctx, d), x_sample.reshape(n_lat, d)], axis=0)
    cond8 = jnp.zeros((8, d), F32).at[0].set(c_ctx).at[1:1 + n_ls].set(c)
    mods = _ada(cond8, ada_w, ada_b).reshape(depth, 8, 6, 1, d)

    ck = cache_attn_k.reshape(n_ls * depth, past, WA)
    cv = cache_attn_v.reshape(n_ls * depth, past, WA)
    p_layers, sf_new, sb_new = [], [], []
    for l in range(depth):
        m = [mods[l, :1 + n_ls, k] for k in range(6)]
        p = _in_proj(x, norm1_g[l], m[1], m[0], w_in[l].astype(BF16), n_ctx, lat_len, tm_proj, tn_in)

        oa = _nbr_attention(p, ck, cv, _na_bias_table(na_rpb[l], lat_len // GRID_W), n_ctx, n_ls, lat_len, l, depth)
        oa = _ctx_attention(p, oa, n_cs, ctx_len)

        rkv = _short_conv(p, rw_conv[l], n_ctx, ctx_len, lat_len)
        kk, lw, b, kd, g, bonus = _rwkv_prep(p, rkv, rw_w0[l], rw_w2[l], rw_a0[l], rw_a2[l], rw_g2[l],
                                             rw_kk[l], rw_ka[l], rw_rk[l], rb)
        s0 = jnp.stack([state_rwkv_fwd[:, l], state_rwkv_bwd[:, l]]).astype(F32)
        o_f, o_b, s_end = _rwkv_scan(rkv, kk, lw, b, kd, s0, n_cs, ctx_len, n_ls, lat_len)
        ob = _rwkv_post(o_f, o_b, bonus, g, rw_ln_g[l], rw_ln_b[l], rb)

        oc = _multiscale_pool(p, pool_w[l], pool_scale[l], n_ctx, ctx_len, lat_len)
        x = _out_proj(oa, ob, oc, w_out[l].astype(BF16), x, m[2], n_ctx, lat_len, tm_proj, tn_out)

        p_layers.append(p)
        sf_new.append(s_end[0, :n_cs])
        sb_new.append(s_end[1, :n_cs])

        i = l // 2
        if l % 2 == 0:
            x = _dense_ffn(x, norm2_g[l], m[4], m[3], m[5], ffn_w1[i].astype(BF16), ffn_w3[i].astype(BF16),
                           ffn_w2[i].astype(BF16), n_ctx, lat_len, tm_dense, tf_dense)
        else:
            h, gates, idx = _norm_mod_router(x, norm2_g[l], m[4], m[3], moe_router[i], n_ctx, lat_len, tm)
            last = l == depth - 1
            x = _moe(x, h, gates, idx, moe_w1[i], moe_w3[i], moe_w2[i], m[5], final_g if last else None,
                     n_ctx, lat_len, tm_moe, tf_moe, rt, split=last)

    if depth % 2 == 0:
        y_ctx, y_lat = x
    else:
        y = _final_norm(x, final_g, tm)
        y_ctx, y_lat = y[:n_ctx], y[n_ctx:]
    k_new, v_new = _kv_outputs(p_layers, n_cs, ctx_len)
    return (y_ctx.reshape(n_cs, ctx_len, d), y_lat.reshape(n_ls, lat_len, d),
            k_new.reshape(n_cs, depth, ctx_len, N_HEADS_A, HEAD_DIM_A),
            v_new.reshape(n_cs, depth, ctx_len, N_HEADS_A, HEAD_DIM_A),
            jnp.stack(sf_new, axis=1), jnp.stack(sb_new, axis=1))


def kernel(x_prompt, x_sample, cache_attn_k, cache_attn_v, state_rwkv_fwd, state_rwkv_bwd, c, c_ctx, norm1_g, norm2_g, ada_w, ada_b, w_in, w_out, na_rpb, rw_conv, rw_w0, rw_w2, rw_a0, rw_a2, rw_g2, rw_kk, rw_ka, rw_rk, rw_ln_g, rw_ln_b, pool_w, pool_scale, ffn_w1, ffn_w3, ffn_w2, moe_router, moe_w1, moe_w3, moe_w2, final_g):
    return _forward(x_prompt, x_sample, cache_attn_k, cache_attn_v, state_rwkv_fwd, state_rwkv_bwd, c, c_ctx,
                    norm1_g, norm2_g, ada_w, ada_b, w_in, w_out, na_rpb, rw_conv, rw_w0, rw_w2, rw_a0, rw_a2,
                    rw_g2, rw_kk, rw_ka, rw_rk, rw_ln_g, rw_ln_b, pool_w, pool_scale, ffn_w1, ffn_w3, ffn_w2,
                    moe_router, moe_w1, moe_w3, moe_w2, final_g,
                    tm=512, tm_proj=1024, tn_in=1920, tn_out=1024, tm_dense=1024, tf_dense=512,
                    tm_moe=1024, tf_moe=512, rt=256, rb=512)
```
